```python
import jax, jax.numpy as jnp
from jax import lax
import numpy as np

D_MODEL = 2048
BATCH = 1
SEQ = 8192
DEPTH = 4

D_FF = 5632
N_BRANCH = 4
BRANCH_W = 512
POOL_WINDOWS = (2, 4, 8, 16)
POOL_GROUPS = len(POOL_WINDOWS)
POOL_GW = BRANCH_W // POOL_GROUPS
DIL_PATTERNS = ((128, 1), (512, 4), (2048, 16))
DIL_GROUPS = len(DIL_PATTERNS)
DIL_HEADS_PER_GROUP = 4
DIL_HEADS = DIL_GROUPS * DIL_HEADS_PER_GROUP
DIL_HEAD_DIM = BRANCH_W // DIL_HEADS_PER_GROUP
DIL_QKV = 3 * DIL_HEADS * DIL_HEAD_DIM
SGU_CHUNK = 128
SGU_GROUPS = 4
SGU_GW = BRANCH_W // SGU_GROUPS
MLA_HEADS = 4
MLA_Q_LORA = 384
MLA_KV_LORA = 128
MLA_NOPE = 128
MLA_ROPE = 64
MLA_V = BRANCH_W // MLA_HEADS
MLA_QK = MLA_NOPE + MLA_ROPE
MLA_QBLOCK = 128
ROPE_THETA = 10000.0
EPS = 1e-6

OFF_POOL = 0
OFF_DIL = OFF_POOL + BRANCH_W
OFF_SGU = OFF_DIL + DIL_QKV
OFF_MLA_Q = OFF_SGU + 2 * BRANCH_W
OFF_MLA_KV = OFF_MLA_Q + MLA_Q_LORA
OFF_MLA_KR = OFF_MLA_KV + MLA_KV_LORA
OFF_GATE = OFF_MLA_KR + MLA_ROPE
N_IN = OFF_GATE + N_BRANCH * D_MODEL

kernel_name = 'hybrid_gated_encoder_block'


def rmsnorm(x, g):
    xf = x.astype(jnp.float32)
    y = xf * lax.rsqrt(jnp.mean(xf * xf, axis=-1, keepdims=True) + EPS)
    return (y * g.astype(jnp.float32)).astype(x.dtype)


def swiglu(x, w_gate, w_up, w_down):
    return (jax.nn.silu(x @ w_gate) * (x @ w_up)) @ w_down


def alibi_slopes(n):
    return jnp.exp2(-8.0 * jnp.arange(1, n + 1, dtype=jnp.float32) / n)


def rope(x, cos, sin):
    half = x.shape[-1] // 2
    x1 = x[..., :half].astype(jnp.float32)
    x2 = x[..., half:].astype(jnp.float32)
    return jnp.concatenate([x1 * cos - x2 * sin, x1 * sin + x2 * cos], axis=-1).astype(x.dtype)


def pool_mixer(a, pool_w, pool_scale):
    b, s, _ = a.shape
    af = a.astype(jnp.float32)
    cs = jnp.concatenate([jnp.zeros_like(af[:, :1]), jnp.cumsum(af, axis=1)], axis=1)
    t = jnp.arange(s)
    outs = []
    for g, w in enumerate(POOL_WINDOWS):
        lo = jnp.clip(t - w // 2, 0, s)
        hi = jnp.clip(t + w // 2, 0, s)
        sl = slice(g * POOL_GW, (g + 1) * POOL_GW)
        csg = cs[:, :, sl]
        mean = (csg[:, hi] - csg[:, lo]) / (hi - lo).astype(jnp.float32)[None, :, None]
        outs.append(mean - af[:, :, sl])
    d = jnp.stack(outs, axis=2).astype(a.dtype)
    y = jnp.einsum('bsgc,gcd->bsgd', d, pool_w)
    return y.reshape(b, s, BRANCH_W) * pool_scale


def banded_attention(q, k, v, n_side, step_penalty):
    n, l, h, dh = q.shape
    blk = n_side
    nb = -(-l // blk)
    lp = nb * blk
    padq = lp - l
    qb = jnp.pad(q, ((0, 0), (0, padq), (0, 0), (0, 0))).reshape(n, nb, blk, h, dh)

    def windows(z):
        zp = jnp.pad(z, ((0, 0), (blk, padq + blk), (0, 0), (0, 0))).reshape(n, nb + 2, blk, h, z.shape[-1])
        return jnp.concatenate([zp[:, :-2], zp[:, 1:-1], zp[:, 2:]], axis=2)

    kw = windows(k)
    vw = windows(v)
    s = jnp.einsum('nbqhd,nbkhd->nbhqk', qb, kw).astype(jnp.float32) * (dh ** -0.5)
    rel = jnp.arange(3 * blk)[None, :] - blk - jnp.arange(blk)[:, None]
    kpos = jnp.arange(nb)[:, None] * blk + jnp.arange(3 * blk)[None, :] - blk
    valid = (jnp.abs(rel) <= n_side)[None] & ((kpos >= 0) & (kpos < l))[:, None, :]
    s = s - step_penalty[:, None, None] * jnp.abs(rel).astype(jnp.float32)
    s = jnp.where(valid[None, :, None], s, -jnp.inf)
    m = jnp.max(s, axis=-1, keepdims=True)
    p = jnp.exp(s - m)
    den = jnp.sum(p, axis=-1, keepdims=True)
    o = jnp.einsum('nbhqk,nbkhd->nbqhd', (p / den).astype(v.dtype), vw)
    lse = (m + jnp.log(den))[..., 0]
    o = o.reshape(n, lp, h, dh)[:, :l]
    lse = lse.transpose(0, 1, 3, 2).reshape(n, lp, h)[:, :l]
    return o, lse


def dilated_mixer(qkv, q_norm_g, k_norm_g):
    b, s, _ = qkv.shape
    q, k, v = jnp.split(qkv.reshape(b, s, 3, DIL_HEADS, DIL_HEAD_DIM), 3, axis=2)
    q = rmsnorm(q[:, :, 0], q_norm_g)
    k = rmsnorm(k[:, :, 0], k_norm_g)
    v = v[:, :, 0]
    slopes = alibi_slopes(DIL_HEADS)
    outs, lses = [], []
    for g, (w, d) in enumerate(DIL_PATTERNS):
        hs = slice(g * DIL_HEADS_PER_GROUP, (g + 1) * DIL_HEADS_PER_GROUP)

        def to_sub(z):
            return z.reshape(b, s // d, d, *z.shape[2:]).swapaxes(1, 2).reshape(b * d, s // d, *z.shape[2:])

        o, lse = banded_attention(to_sub(q[:, :, hs]), to_sub(k[:, :, hs]), to_sub(v[:, :, hs]),
                                  w // (2 * d), slopes[hs] * d)
        outs.append(o.reshape(b, d, s // d, DIL_HEADS_PER_GROUP, DIL_HEAD_DIM).swapaxes(1, 2).reshape(b, s, DIL_HEADS_PER_GROUP, DIL_HEAD_DIM))
        lses.append(lse.reshape(b, d, s // d, DIL_HEADS_PER_GROUP).swapaxes(1, 2).reshape(b, s, DIL_HEADS_PER_GROUP))
    o = jnp.stack(outs, axis=2)
    wts = jax.nn.softmax(jnp.stack(lses, axis=2), axis=2)
    out = jnp.sum(wts[..., None].astype(o.dtype) * o, axis=2)
    return out.reshape(b, s, BRANCH_W)


def sgu_mixer(z, v_norm_g, w_s, b_s):
    b, s, _ = z.shape
    z = jax.nn.gelu(z)
    u, vv = z[..., :BRANCH_W], z[..., BRANCH_W:]
    vv = rmsnorm(vv, v_norm_g).reshape(b, s // SGU_CHUNK, SGU_CHUNK, SGU_GROUPS, SGU_GW)
    mixed = jnp.einsum('gts,bnsgc->bntgc', w_s, vv) + b_s.T[None, None, :, :, None]
    return u * mixed.reshape(b, s, BRANCH_W)


def mla_mixer(cq, ckv, kr, q_a_norm, w_uq, kv_a_norm, w_ukv, q_norm_g, k_norm_g, cos, sin):
    b, s, _ = cq.shape
    q = (rmsnorm(cq, q_a_norm) @ w_uq).reshape(b, s, MLA_HEADS, MLA_QK)
    kv = (rmsnorm(ckv, kv_a_norm) @ w_ukv).reshape(b, s, MLA_HEADS, MLA_NOPE + MLA_V)
    k_nope, v = kv[..., :MLA_NOPE], kv[..., MLA_NOPE:]
    k = jnp.concatenate([k_nope, jnp.broadcast_to(kr[:, :, None, :], (b, s, MLA_HEADS, MLA_ROPE))], axis=-1)
    q = rmsnorm(q, q_norm_g)
    k = rmsnorm(k, k_norm_g)
    q = jnp.concatenate([q[..., :MLA_NOPE], rope(q[..., MLA_NOPE:], cos, sin)], axis=-1)
    k = jnp.concatenate([k[..., :MLA_NOPE], rope(k[..., MLA_NOPE:], cos, sin)], axis=-1)
    scale = MLA_QK ** -0.5
    qb = q.reshape(b, s // MLA_QBLOCK, MLA_QBLOCK, MLA_HEADS, MLA_QK).swapaxes(0, 1)

    def attend(qblk):
        sc = jnp.einsum('bqhd,bkhd->bhqk', qblk, k).astype(jnp.float32) * scale
        p = jax.nn.softmax(sc, axis=-1)
        return jnp.einsum('bhqk,bkhd->bqhd', p.astype(v.dtype), v)

    o = lax.map(attend, qb)
    return o.swapaxes(0, 1).reshape(b, s, MLA_HEADS * MLA_V)


def setup_inputs(seed: int = 0) -> dict:
    key = jax.random.key(seed)
    ks = jax.random.split(key, 26)
    L = DEPTH

    def dense(k, shape, fan_in):
        return jax.random.normal(k, shape, jnp.float32) * (fan_in ** -0.5)

    def gain(k, shape):
        return 1.0 + 0.02 * jax.random.normal(k, shape, jnp.float32)

    return {
        'x': jax.random.normal(ks[0], (BATCH, SEQ, D_MODEL), jnp.float32),
        'ffn1_norm': gain(ks[1], (L, D_MODEL)),
        'ffn1_w_gate': dense(ks[2], (L, D_MODEL, D_FF), D_MODEL),
        'ffn1_w_up': dense(ks[3], (L, D_MODEL, D_FF), D_MODEL),
        'ffn1_w_down': dense(ks[4], (L, D_FF, D_MODEL), D_FF),
        'mix_norm': gain(ks[5], (L, D_MODEL)),
        'w_in': dense(ks[6], (L, D_MODEL, N_IN), D_MODEL),
        'pool_w': dense(ks[7], (L, POOL_GROUPS, POOL_GW, POOL_GW), POOL_GW),
        'pool_scale': gain(ks[8], (L, BRANCH_W)),
        'dil_q_norm': gain(ks[9], (L, DIL_HEAD_DIM)),
        'dil_k_norm': gain(ks[10], (L, DIL_HEAD_DIM)),
        'sgu_v_norm': gain(ks[11], (L, BRANCH_W)),
        'sgu_w': dense(ks[12], (L, SGU_GROUPS, SGU_CHUNK, SGU_CHUNK), SGU_CHUNK),
        'sgu_b': gain(ks[13], (L, SGU_GROUPS, SGU_CHUNK)),
        'mla_q_a_norm': gain(ks[14], (L, MLA_Q_LORA)),
        'mla_w_uq': dense(ks[15], (L, MLA_Q_LORA, MLA_HEADS * MLA_QK), MLA_Q_LORA),
        'mla_kv_a_norm': gain(ks[16], (L, MLA_KV_LORA)),
        'mla_w_ukv': dense(ks[17], (L, MLA_KV_LORA, MLA_HEADS * (MLA_NOPE + MLA_V)), MLA_KV_LORA),
        'mla_q_norm': gain(ks[18], (L, MLA_QK)),
        'mla_k_norm': gain(ks[19], (L, MLA_QK)),
        'w_branch': dense(ks[20], (L, N_BRANCH, BRANCH_W, D_MODEL), BRANCH_W),
        'w_out': dense(ks[21], (L, D_MODEL, D_MODEL), D_MODEL),
        'ffn2_norm': gain(ks[22], (L, D_MODEL)),
        'ffn2_w_gate': dense(ks[23], (L, D_MODEL, D_FF), D_MODEL),
        'ffn2_w_up': dense(ks[24], (L, D_MODEL, D_FF), D_MODEL),
        'ffn2_w_down': dense(ks[25], (L, D_FF, D_MODEL), D_FF),
    }


def reference(x, ffn1_norm, ffn1_w_gate, ffn1_w_up, ffn1_w_down, mix_norm, w_in, pool_w, pool_scale,
              dil_q_norm, dil_k_norm, sgu_v_norm, sgu_w, sgu_b, mla_q_a_norm, mla_w_uq, mla_kv_a_norm,
              mla_w_ukv, mla_q_norm, mla_k_norm, w_branch, w_out, ffn2_norm, ffn2_w_gate, ffn2_w_up,
              ffn2_w_down):
    b, s, _ = x.shape
    pos = jnp.arange(s, dtype=jnp.float32)
    inv_freq = ROPE_THETA ** (-jnp.arange(0, MLA_ROPE, 2, dtype=jnp.float32) / MLA_ROPE)
    ang = pos[:, None] * inv_freq[None, :]
    cos = jnp.cos(ang)[None, :, None, :]
    sin = jnp.sin(ang)[None, :, None, :]
    for l in range(DEPTH):
        x = x + 0.5 * swiglu(rmsnorm(x, ffn1_norm[l]), ffn1_w_gate[l], ffn1_w_up[l], ffn1_w_down[l])
        h = rmsnorm(x, mix_norm[l])
        p = h @ w_in[l]
        z_a = pool_mixer(p[..., OFF_POOL:OFF_DIL], pool_w[l], pool_scale[l])
        z_b = dilated_mixer(p[..., OFF_DIL:OFF_SGU], dil_q_norm[l], dil_k_norm[l])
        z_c = sgu_mixer(p[..., OFF_SGU:OFF_MLA_Q], sgu_v_norm[l], sgu_w[l], sgu_b[l])
        z_d = mla_mixer(p[..., OFF_MLA_Q:OFF_MLA_KV], p[..., OFF_MLA_KV:OFF_MLA_KR], p[..., OFF_MLA_KR:OFF_GATE],
                        mla_q_a_norm[l], mla_w_uq[l], mla_kv_a_norm[l], mla_w_ukv[l],
                        mla_q_norm[l], mla_k_norm[l], cos, sin)
        gates = jax.nn.sigmoid(p[..., OFF_GATE:].reshape(b, s, N_BRANCH, D_MODEL))
        z = jnp.stack([z_a, z_b, z_c, z_d], axis=2)
        merged = jnp.sum(gates * jnp.einsum('bsnc,ncd->bsnd', z, w_branch[l]), axis=2)
        x = x + merged @ w_out[l]
        x = x + 0.5 * swiglu(rmsnorm(x, ffn2_norm[l]), ffn2_w_gate[l], ffn2_w_up[l], ffn2_w_down[l])
    return x
```

```python
import functools

import numpy as np
import jax
import jax.numpy as jnp
from jax import lax
from jax.experimental import pallas as pl
from jax.experimental.pallas import tpu as pltpu

F32 = jnp.float32
BF16 = jnp.bfloat16

D_MODEL = 2048
DEPTH = 4
D_FF = 5632
N_BRANCH = 4
BRANCH_W = 512
POOL_WINDOWS = (2, 4, 8, 16)
POOL_HALO = 8
DIL_PATTERNS = ((128, 1), (512, 4), (2048, 16))
DIL_HEADS = 12
DIL_HEADS_PER_GROUP = 4
DIL_HEAD_DIM = 128
DIL_QKV = 3 * DIL_HEADS * DIL_HEAD_DIM
DIL_SIDE = 64
SGU_CHUNK = 128
SGU_GROUPS = 4
MLA_HEADS = 4
MLA_Q_LORA = 384
MLA_KV_LORA = 128
MLA_NOPE = 128
MLA_ROPE = 64
MLA_V = 128
MLA_QK = MLA_NOPE + MLA_ROPE
MLA_PAD_QK = 256
ROPE_THETA = 10000.0
EPS = 1e-6
NEG_BIG = -1e30

OFF_POOL = 0
OFF_DIL = OFF_POOL + BRANCH_W
OFF_SGU = OFF_DIL + DIL_QKV
OFF_MLA_Q = OFF_SGU + 2 * BRANCH_W
OFF_MLA_KV = OFF_MLA_Q + MLA_Q_LORA
OFF_MLA_KR = OFF_MLA_KV + MLA_KV_LORA
OFF_GATE = OFF_MLA_KR + MLA_ROPE

REST_POOL = 0
REST_SGU_U = 512
REST_SGU_V = 1024
REST_CQ = 1536
REST_CKV = 1920
REST_KR = 2048
REST_W = 2304

LANE = 128
VMEM_LIMIT = 56 * 1024 * 1024


def _params(*sem):
    return pltpu.CompilerParams(dimension_semantics=sem, vmem_limit_bytes=VMEM_LIMIT)


def _rms(xf, g):
    return xf * lax.rsqrt(jnp.mean(xf * xf, axis=-1, keepdims=True) + EPS) * g


def _ffn_body(x_ref, g_ref, wg_ref, wu_ref, wd_ref, o_ref, h_ref):
    @pl.when(pl.program_id(1) == 0)
    def _():
        x = x_ref[...]
        h_ref[...] = _rms(x, g_ref[...]).astype(BF16)
        o_ref[...] = x

    h = h_ref[...]
    g = jnp.dot(h, wg_ref[...], preferred_element_type=F32)
    u = jnp.dot(h, wu_ref[...], preferred_element_type=F32)
    a = (0.5 * (jax.nn.silu(g) * u)).astype(BF16)
    o_ref[...] += jnp.dot(a, wd_ref[...], preferred_element_type=F32)


def ffn(x, norm_g, wg, wu, wd, *, tm=512, tf=512):
    s, d = x.shape
    f = wg.shape[1]
    return pl.pallas_call(
        _ffn_body,
        grid=(s // tm, f // tf),
        in_specs=[
            pl.BlockSpec((tm, d), lambda i, j: (i, 0)),
            pl.BlockSpec((1, d), lambda i, j: (0, 0)),
            pl.BlockSpec((d, tf), lambda i, j: (0, j)),
            pl.BlockSpec((d, tf), lambda i, j: (0, j)),
            pl.BlockSpec((tf, d), lambda i, j: (j, 0)),
        ],
        out_specs=pl.BlockSpec((tm, d), lambda i, j: (i, 0)),
        out_shape=jax.ShapeDtypeStruct((s, d), F32),
        scratch_shapes=[pltpu.VMEM((tm, d), BF16)],
        compiler_params=_params("parallel", "arbitrary"),
        name="ffn",
    )(x, norm_g.reshape(1, d), wg, wu, wd)


def _proj_rest_body(x_ref, g_ref, w_ref, p_ref, h_ref):
    h = _rms(x_ref[...], g_ref[...]).astype(BF16)
    h_ref[...] = h
    p_ref[...] = jnp.dot(h, w_ref[...], preferred_element_type=F32)


def proj_rest(x, norm_g, w, *, tm=512):
    s, d = x.shape
    n = w.shape[1]
    return pl.pallas_call(
        _proj_rest_body,
        grid=(s // tm,),
        in_specs=[
            pl.BlockSpec((tm, d), lambda i: (i, 0)),
            pl.BlockSpec((1, d), lambda i: (0, 0)),
            pl.BlockSpec((d, n), lambda i: (0, 0)),
        ],
        out_specs=[
            pl.BlockSpec((tm, n), lambda i: (i, 0)),
            pl.BlockSpec((tm, d), lambda i: (i, 0)),
        ],
        out_shape=[jax.ShapeDtypeStruct((s, n), F32), jax.ShapeDtypeStruct((s, d), BF16)],
        compiler_params=_params("parallel"),
        name="proj_rest",
    )(x, norm_g.reshape(1, d), w)


def _proj_dil_body(h_ref, w_ref, g_ref, o_ref, *, n_norm_tiles):
    acc = jnp.dot(h_ref[...], w_ref[...], preferred_element_type=F32)
    normed = pl.program_id(1) < n_norm_tiles
    for a in range(acc.shape[1] // DIL_HEAD_DIM):
        sl = slice(a * DIL_HEAD_DIM, (a + 1) * DIL_HEAD_DIM)
        xh = acc[:, sl]
        r = lax.rsqrt(jnp.mean(xh * xh, axis=-1, keepdims=True) + EPS)
        r = jnp.where(normed, r, 1.0)
        o_ref[:, sl] = (xh * r * g_ref[:, sl]).astype(BF16)


def proj_dil(h, w, gains, *, tm=1024, tn=512):
    s, d = h.shape
    n = w.shape[1]
    body = functools.partial(_proj_dil_body, n_norm_tiles=2 * (n // 3) // tn)
    return pl.pallas_call(
        body,
        grid=(s // tm, n // tn),
        in_specs=[
            pl.BlockSpec((tm, d), lambda i, j: (i, 0)),
            pl.BlockSpec((d, tn), lambda i, j: (0, j)),
            pl.BlockSpec((1, tn), lambda i, j: (0, j)),
        ],
        out_specs=pl.BlockSpec((tm, tn), lambda i, j: (i, j)),
        out_shape=jax.ShapeDtypeStruct((s, n), BF16),
        compiler_params=_params("parallel", "arbitrary"),
        name="proj_dil",
    )(h, w, gains)


def _pool_body(prev_ref, cur_ref, next_ref, w_ref, scale_ref, o_ref, ext_ref, *, seq):
    i = pl.program_id(0)
    tm = cur_ref.shape[0]
    cur = cur_ref[...]
    ext_ref[0:POOL_HALO, :] = jnp.where(i > 0, prev_ref[...], 0.0)
    ext_ref[POOL_HALO:POOL_HALO + tm, :] = cur
    ext_ref[POOL_HALO + tm:2 * POOL_HALO + tm, :] = jnp.where(
        i < pl.num_programs(0) - 1, next_ref[...], 0.0)
    t = i * tm + lax.broadcasted_iota(jnp.int32, (tm, 1), 0)
    for g, w in enumerate(POOL_WINDOWS):
        sl = slice(g * LANE, (g + 1) * LANE)
        tot = jnp.zeros((tm, LANE), F32)
        for o in range(-(w // 2), w // 2):
            tot = tot + ext_ref[POOL_HALO + o:POOL_HALO + o + tm, sl]
        cnt = (jnp.minimum(t + w // 2, seq) - jnp.maximum(t - w // 2, 0)).astype(F32)
        dev = (tot / cnt - cur[:, sl]).astype(BF16)
        y = jnp.dot(dev, w_ref[g], preferred_element_type=F32) * scale_ref[:, sl]
        o_ref[:, sl] = y.astype(BF16)


def pool_mixer(p_rest, pool_w, pool_scale, *, tm=512):
    s = p_rest.shape[0]
    hb = tm // POOL_HALO
    last = s // POOL_HALO - 1
    return pl.pallas_call(
        functools.partial(_pool_body, seq=s),
        grid=(s // tm,),
        in_specs=[
            pl.BlockSpec((POOL_HALO, BRANCH_W), lambda i: (jnp.maximum(i * hb - 1, 0), 0)),
            pl.BlockSpec((tm, BRANCH_W), lambda i: (i, 0)),
            pl.BlockSpec((POOL_HALO, BRANCH_W), lambda i: (jnp.minimum((i + 1) * hb, last), 0)),
            pl.BlockSpec((len(POOL_WINDOWS), LANE, LANE), lambda i: (0, 0, 0)),
            pl.BlockSpec((1, BRANCH_W), lambda i: (0, 0)),
        ],
        out_specs=pl.BlockSpec((tm, BRANCH_W), lambda i: (i, 0)),
        out_shape=jax.ShapeDtypeStruct((s, BRANCH_W), BF16),
        scratch_shapes=[pltpu.VMEM((tm + 2 * POOL_HALO, BRANCH_W), F32)],
        compiler_params=_params("parallel"),
        name="pool_mixer",
    )(p_rest, p_rest, p_rest, pool_w, pool_scale.reshape(1, BRANCH_W))


def _alibi_slopes():
    n = DIL_HEADS
    return np.exp2(np.float32(-8.0) * np.arange(1, n + 1, dtype=np.float32) / np.float32(n))


def _dil_attn_body(q_ref, kp_ref, kc_ref, kn_ref, vp_ref, vc_ref, vn_ref, o_ref, lse_ref,
                   *, sub_len, penalties):
    i = pl.program_id(1)
    tq = q_ref.shape[0]
    half = tq - DIL_SIDE
    nk = tq + 2 * DIL_SIDE
    row = lax.broadcasted_iota(jnp.int32, (tq, nk), 0)
    col = lax.broadcasted_iota(jnp.int32, (tq, nk), 1)
    rel = col - DIL_SIDE - row
    kpos = i * tq + col - DIL_SIDE
    dist = jnp.abs(rel)
    valid = (dist <= DIL_SIDE) & (kpos >= 0) & (kpos < sub_len)
    distf = dist.astype(F32)
    for h in range(DIL_HEADS_PER_GROUP):
        sl = slice(h * DIL_HEAD_DIM, (h + 1) * DIL_HEAD_DIM)
        k = jnp.concatenate([kp_ref[half:, sl], kc_ref[:, sl], kn_ref[:DIL_SIDE, sl]], axis=0)
        v = jnp.concatenate([vp_ref[half:, sl], vc_ref[:, sl], vn_ref[:DIL_SIDE, sl]], axis=0)
        s = lax.dot_general(q_ref[:, sl], k, (((1,), (1,)), ((), ())),
                            preferred_element_type=F32) * (DIL_HEAD_DIM ** -0.5)
        s = s - penalties[h] * distf
        s = jnp.where(valid, s, NEG_BIG)
        m = jnp.max(s, axis=-1, keepdims=True)
        p = jnp.exp(s - m)
        den = jnp.sum(p, axis=-1, keepdims=True)
        o_ref[:, sl] = jnp.dot((p / den).astype(BF16), v, preferred_element_type=F32)
        lse_ref[:, sl] = jnp.broadcast_to(m + jnp.log(den), (tq, DIL_HEAD_DIM))


def dil_attn(qkv, group, *, tq=128):
    s = qkv.shape[0]
    _, d = DIL_PATTERNS[group]
    sub_len = s // d
    nb = sub_len // tq
    blocks_per_row = DIL_QKV // BRANCH_W
    n_groups = len(DIL_PATTERNS)
    view = qkv.reshape(sub_len, d * DIL_QKV)
    slopes = _alibi_slopes()[group * DIL_HEADS_PER_GROUP:(group + 1) * DIL_HEADS_PER_GROUP]
    penalties = tuple(float(np.float32(x) * np.float32(d)) for x in slopes)

    def spec(which, shift):
        def index(r, i):
            return (jnp.clip(i + shift, 0, nb - 1), blocks_per_row * r + which * n_groups + group)
        return pl.BlockSpec((tq, BRANCH_W), index)

    out_spec = pl.BlockSpec((tq, BRANCH_W), lambda r, i: (i, r))
    o, lse = pl.pallas_call(
        functools.partial(_dil_attn_body, sub_len=sub_len, penalties=penalties),
        grid=(d, nb),
        in_specs=[spec(0, 0), spec(1, -1), spec(1, 0), spec(1, 1), spec(2, -1), spec(2, 0), spec(2, 1)],
        out_specs=[out_spec, out_spec],
        out_shape=[jax.ShapeDtypeStruct((sub_len, d * BRANCH_W), F32)] * 2,
        compiler_params=_params("parallel", "parallel"),
        name=f"dil_attn_g{group}",
    )(view, view, view, view, view, view, view)
    return o.reshape(s, BRANCH_W), lse.reshape(s, BRANCH_W)


def _dil_combine_body(o0, o1, o2, l0, l1, l2, z_ref):
    a, b, c = l0[...], l1[...], l2[...]
    m = jnp.maximum(jnp.maximum(a, b), c)
    ea, eb, ec = jnp.exp(a - m), jnp.exp(b - m), jnp.exp(c - m)
    tot = ea + eb + ec
    z = (ea / tot) * o0[...] + (eb / tot) * o1[...] + (ec / tot) * o2[...]
    z_ref[...] = z.astype(BF16)


def dil_combine(outs, lses, *, tm=1024):
    s = outs[0].shape[0]
    spec = pl.BlockSpec((tm, BRANCH_W), lambda i: (i, 0))
    return pl.pallas_call(
        _dil_combine_body,
        grid=(s // tm,),
        in_specs=[spec] * 6,
        out_specs=spec,
        out_shape=jax.ShapeDtypeStruct((s, BRANCH_W), BF16),
        compiler_params=_params("parallel"),
        name="dil_combine",
    )(*outs, *lses)


def _sgu_body(u_ref, v_ref, g_ref, ws_ref, b_ref, o_ref):
    tm = u_ref.shape[0]
    u = jax.nn.gelu(u_ref[...])
    vn = _rms(jax.nn.gelu(v_ref[...]), g_ref[...]).astype(BF16)
    for n in range(tm // SGU_CHUNK):
        rows = slice(n * SGU_CHUNK, (n + 1) * SGU_CHUNK)
        for g in range(SGU_GROUPS):
            cols = slice(g * LANE, (g + 1) * LANE)
            mixed = jnp.dot(ws_ref[g], vn[rows, cols], preferred_element_type=F32) + b_ref[g]
            o_ref[rows, cols] = (u[rows, cols] * mixed).astype(BF16)


def sgu_mixer(p_rest, v_norm_g, ws, b_full, *, tm=512):
    s = p_rest.shape[0]
    return pl.pallas_call(
        _sgu_body,
        grid=(s // tm,),
        in_specs=[
            pl.BlockSpec((tm, BRANCH_W), lambda i: (i, REST_SGU_U // BRANCH_W)),
            pl.BlockSpec((tm, BRANCH_W), lambda i: (i, REST_SGU_V // BRANCH_W)),
            pl.BlockSpec((1, BRANCH_W), lambda i: (0, 0)),
            pl.BlockSpec((SGU_GROUPS, SGU_CHUNK, SGU_CHUNK), lambda i: (0, 0, 0)),
            pl.BlockSpec((SGU_GROUPS, SGU_CHUNK, LANE), lambda i: (0, 0, 0)),
        ],
        out_specs=pl.BlockSpec((tm, BRANCH_W), lambda i: (i, 0)),
        out_shape=jax.ShapeDtypeStruct((s, BRANCH_W), BF16),
        compiler_params=_params("parallel"),
        name="sgu_mixer",
    )(p_rest, p_rest, v_norm_g.reshape(1, BRANCH_W), ws, b_full)


def _swap_rope_halves(y):
    width = y.shape[1]
    lane = lax.broadcasted_iota(jnp.int32, y.shape, 1)
    first_half = (lane % MLA_ROPE) < MLA_ROPE // 2
    return jnp.where(first_half, pltpu.roll(y, width - MLA_ROPE // 2, 1),
                     pltpu.roll(y, MLA_ROPE // 2, 1))


def _mla_prep_body(cq_ref, ckv_ref, kr_ref, cos_ref, sin_ref, qa_g, wuq_ref, kva_g, wukv_ref,
                   gqn_ref, gqr_ref, gkn_ref, gkr_ref, q_ref, k_ref, v_ref):
    nope_w = MLA_HEADS * MLA_NOPE
    qall = jnp.dot(_rms(cq_ref[...], qa_g[...]).astype(BF16), wuq_ref[...], preferred_element_type=F32)
    kv = jnp.dot(_rms(ckv_ref[...], kva_g[...]).astype(BF16), wukv_ref[...], preferred_element_type=F32)
    qn, qr = qall[:, :nope_w], qall[:, nope_w:]
    kn = kv[:, :nope_w]
    v_ref[...] = kv[:, nope_w:].astype(BF16)
    kr = kr_ref[...]

    rope_w = MLA_HEADS * MLA_ROPE
    head_of_lane = lax.broadcasted_iota(jnp.int32, (1, rope_w), 1) // MLA_ROPE
    qr_sq = qr * qr
    kr_ssq = jnp.sum(jnp.where(head_of_lane == 0, kr * kr, 0.0), axis=-1, keepdims=True)
    rq, rk = [], []
    for h in range(MLA_HEADS):
        sl = slice(h * MLA_NOPE, (h + 1) * MLA_NOPE)
        q_ssq = (jnp.sum(qn[:, sl] * qn[:, sl], axis=-1, keepdims=True)
                 + jnp.sum(jnp.where(head_of_lane == h, qr_sq, 0.0), axis=-1, keepdims=True))
        k_ssq = jnp.sum(kn[:, sl] * kn[:, sl], axis=-1, keepdims=True) + kr_ssq
        rq.append(lax.rsqrt(q_ssq / MLA_QK + EPS))
        rk.append(lax.rsqrt(k_ssq / MLA_QK + EPS))

    def per_lane(rs):
        out = jnp.zeros((kr.shape[0], rope_w), F32)
        for h in range(MLA_HEADS):
            out = jnp.where(head_of_lane == h, rs[h], out)
        return out

    def rotate(x, rs, g_ref):
        y = x * per_lane(rs) * g_ref[...]
        return y * cos_ref[...] + _swap_rope_halves(y) * sin_ref[...]

    q_rope = rotate(qr, rq, gqr_ref)
    k_rope = rotate(kr, rk, gkr_ref)
    lane_tile = lax.broadcasted_iota(jnp.int32, (1, LANE), 1) // MLA_ROPE
    for h in range(MLA_HEADS):
        sl = slice(h * MLA_NOPE, (h + 1) * MLA_NOPE)
        base = h * MLA_PAD_QK
        tile = slice((h // 2) * LANE, (h // 2 + 1) * LANE)
        q_ref[:, base:base + MLA_NOPE] = (qn[:, sl] * rq[h] * gqn_ref[:, sl]).astype(BF16)
        k_ref[:, base:base + MLA_NOPE] = (kn[:, sl] * rk[h] * gkn_ref[:, sl]).astype(BF16)
        q_ref[:, base + MLA_NOPE:base + MLA_PAD_QK] = jnp.where(
            lane_tile == h % 2, q_rope[:, tile], 0.0).astype(BF16)
        k_ref[:, base + MLA_NOPE:base + MLA_PAD_QK] = k_rope[:, tile].astype(BF16)


def mla_prep(p_rest, cos4, sin4, qa_g, wuq, kva_g, wukv, gqn, gqr, gkn, gkr, *, tm=512):
    s = p_rest.shape[0]
    rope_w = MLA_HEADS * MLA_ROPE

    def const(shape):
        return pl.BlockSpec(shape, lambda i: (0,) * len(shape))

    return pl.pallas_call(
        _mla_prep_body,
        grid=(s // tm,),
        in_specs=[
            pl.BlockSpec((tm, MLA_Q_LORA), lambda i: (i, REST_CQ // MLA_Q_LORA)),
            pl.BlockSpec((tm, MLA_KV_LORA), lambda i: (i, REST_CKV // MLA_KV_LORA)),
            pl.BlockSpec((tm, rope_w), lambda i: (i, REST_KR // rope_w)),
            pl.BlockSpec((tm, rope_w), lambda i: (i, 0)),
            pl.BlockSpec((tm, rope_w), lambda i: (i, 0)),
            const((1, MLA_Q_LORA)), const(wuq.shape), const((1, MLA_KV_LORA)), const(wukv.shape),
            const(gqn.shape), const(gqr.shape), const(gkn.shape), const(gkr.shape),
        ],
        out_specs=[
            pl.BlockSpec((tm, MLA_HEADS * MLA_PAD_QK), lambda i: (i, 0)),
            pl.BlockSpec((tm, MLA_HEADS * MLA_PAD_QK), lambda i: (i, 0)),
            pl.BlockSpec((tm, MLA_HEADS * MLA_V), lambda i: (i, 0)),
        ],
        out_shape=[
            jax.ShapeDtypeStruct((s, MLA_HEADS * MLA_PAD_QK), BF16),
            jax.ShapeDtypeStruct((s, MLA_HEADS * MLA_PAD_QK), BF16),
            jax.ShapeDtypeStruct((s, MLA_HEADS * MLA_V), BF16),
        ],
        compiler_params=_params("parallel"),
        name="mla_prep",
    )(p_rest, p_rest, p_rest, cos4, sin4, qa_g.reshape(1, -1), wuq, kva_g.reshape(1, -1), wukv,
      gqn, gqr, gkn, gkr)


def _mla_attn_body(q_ref, k_ref, v_ref, o_ref, m_ref, l_ref, acc_ref):
    j = pl.program_id(1)

    @pl.when(j == 0)
    def _():
        m_ref[...] = jnp.full(m_ref.shape, NEG_BIG, F32)
        l_ref[...] = jnp.zeros(l_ref.shape, F32)
        acc_ref[...] = jnp.zeros(acc_ref.shape, F32)

    for h in range(MLA_HEADS):
        qk = slice(h * MLA_PAD_QK, (h + 1) * MLA_PAD_QK)
        vs = slice(h * MLA_V, (h + 1) * MLA_V)
        s = lax.dot_general(q_ref[:, qk], k_ref[:, qk], (((1,), (1,)), ((), ())),
                            preferred_element_type=F32) * (MLA_QK ** -0.5)
        m_prev = m_ref[h]
        m_new = jnp.maximum(m_prev, jnp.max(s, axis=-1, keepdims=True))
        alpha = jnp.exp(m_prev - m_new)
        p = jnp.exp(s - m_new)
        l_ref[h] = alpha * l_ref[h] + jnp.sum(p, axis=-1, keepdims=True)
        acc_ref[:, vs] = alpha * acc_ref[:, vs] + jnp.dot(p.astype(BF16), v_ref[:, vs],
                                                          preferred_element_type=F32)
        m_ref[h] = m_new

    @pl.when(j == pl.num_programs(1) - 1)
    def _():
        for h in range(MLA_HEADS):
            vs = slice(h * MLA_V, (h + 1) * MLA_V)
            o_ref[:, vs] = (acc_ref[:, vs] / l_ref[h]).astype(BF16)


def mla_attn(q, k, v, *, tq=1024, tk=512):
    s = q.shape[0]
    return pl.pallas_call(
        _mla_attn_body,
        grid=(s // tq, s // tk),
        in_specs=[
            pl.BlockSpec((tq, MLA_HEADS * MLA_PAD_QK), lambda i, j: (i, 0)),
            pl.BlockSpec((tk, MLA_HEADS * MLA_PAD_QK), lambda i, j: (j, 0)),
            pl.BlockSpec((tk, MLA_HEADS * MLA_V), lambda i, j: (j, 0)),
        ],
        out_specs=pl.BlockSpec((tq, MLA_HEADS * MLA_V), lambda i, j: (i, 0)),
        out_shape=jax.ShapeDtypeStruct((s, MLA_HEADS * MLA_V), BF16),
        scratch_shapes=[
            pltpu.VMEM((MLA_HEADS, tq, 1), F32),
            pltpu.VMEM((MLA_HEADS, tq, 1), F32),
            pltpu.VMEM((tq, MLA_HEADS * MLA_V), F32),
        ],
        compiler_params=_params("parallel", "arbitrary"),
        name="mla_attn",
    )(q, k, v)


def _merge_body(h_ref, za_ref, zb_ref, zc_ref, zd_ref, wg_ref, wb_ref, o_ref, acc_ref):
    n = pl.program_id(2)

    @pl.when(n == 0)
    def _():
        acc_ref[...] = jnp.zeros(acc_ref.shape, F32)

    gate = jax.nn.sigmoid(jnp.dot(h_ref[...], wg_ref[...], preferred_element_type=F32))
    for b, z_ref in enumerate((za_ref, zb_ref, zc_ref, zd_ref)):
        @pl.when(n == b)
        def _(z_ref=z_ref):
            acc_ref[...] += gate * jnp.dot(z_ref[...], wb_ref[...], preferred_element_type=F32)

    @pl.when(n == N_BRANCH - 1)
    def _():
        o_ref[...] = acc_ref[...].astype(BF16)


def merge(h, zs, w_gate, w_branch, *, tm=1024, tn=512):
    s, d = h.shape
    ct = d // tn
    z_spec = pl.BlockSpec((tm, BRANCH_W), lambda i, c, n: (i, 0))
    return pl.pallas_call(
        _merge_body,
        grid=(s // tm, ct, N_BRANCH),
        in_specs=[
            pl.BlockSpec((tm, d), lambda i, c, n: (i, 0)),
            z_spec, z_spec, z_spec, z_spec,
            pl.BlockSpec((d, tn), lambda i, c, n: (0, n * ct + c)),
            pl.BlockSpec((None, BRANCH_W, tn), lambda i, c, n: (n, 0, c)),
        ],
        out_specs=pl.BlockSpec((tm, tn), lambda i, c, n: (i, c)),
        out_shape=jax.ShapeDtypeStruct((s, d), BF16),
        scratch_shapes=[pltpu.VMEM((tm, tn), F32)],
        compiler_params=_params("parallel", "arbitrary", "arbitrary"),
        name="merge",
    )(h, *zs, w_gate, w_branch)


def _out_proj_body(x_ref, y_ref, w_ref, o_ref):
    o_ref[...] = x_ref[...] + jnp.dot(y_ref[...], w_ref[...], preferred_element_type=F32)


def out_proj(x, y, w, *, tm=1024, tn=512):
    s, d = x.shape
    return pl.pallas_call(
        _out_proj_body,
        grid=(s // tm, d // tn),
        in_specs=[
            pl.BlockSpec((tm, tn), lambda i, j: (i, j)),
            pl.BlockSpec((tm, d), lambda i, j: (i, 0)),
            pl.BlockSpec((d, tn), lambda i, j: (0, j)),
        ],
        out_specs=pl.BlockSpec((tm, tn), lambda i, j: (i, j)),
        out_shape=jax.ShapeDtypeStruct((s, d), F32),
        compiler_params=_params("parallel", "arbitrary"),
        name="out_proj",
    )(x, y, w)


def _rope_tables(s):
    pos = jnp.arange(s, dtype=F32)
    inv_freq = ROPE_THETA ** (-jnp.arange(0, MLA_ROPE, 2, dtype=F32) / MLA_ROPE)
    ang = pos[:, None] * inv_freq[None, :]
    cos, sin = jnp.cos(ang), jnp.sin(ang)
    cos4 = jnp.tile(jnp.concatenate([cos, cos], axis=-1), (1, MLA_HEADS))
    sin4 = jnp.tile(jnp.concatenate([-sin, sin], axis=-1), (1, MLA_HEADS))
    return cos4, sin4


def _split_heads(w, widths):
    per_head = sum(widths)
    w3 = w.reshape(w.shape[0], MLA_HEADS, per_head)
    parts, off = [], 0
    for wd in widths:
        parts.append(w3[:, :, off:off + wd].reshape(w.shape[0], MLA_HEADS * wd))
        off += wd
    return jnp.concatenate(parts, axis=1)


def mixer_block(x, cos4, sin4, mix_norm, w_in, pool_w, pool_scale, dil_q_norm, dil_k_norm, sgu_v_norm,
                sgu_w, sgu_b, mla_q_a_norm, mla_w_uq, mla_kv_a_norm, mla_w_ukv, mla_q_norm, mla_k_norm,
                w_branch, w_out):
    w_kr = w_in[:, OFF_MLA_KR:OFF_GATE]
    w_rest = jnp.concatenate(
        [w_in[:, OFF_POOL:OFF_DIL], w_in[:, OFF_SGU:OFF_MLA_KR]] + [w_kr] * MLA_HEADS, axis=1).astype(BF16)
    p_rest, h = proj_rest(x, mix_norm, w_rest)

    ones = jnp.ones((DIL_HEADS * DIL_HEAD_DIM,), F32)
    dil_gains = jnp.concatenate([jnp.tile(dil_q_norm, DIL_HEADS), jnp.tile(dil_k_norm, DIL_HEADS), ones])
    qkv = proj_dil(h, w_in[:, OFF_DIL:OFF_SGU].astype(BF16), dil_gains.reshape(1, DIL_QKV))

    z_a = pool_mixer(p_rest, pool_w.astype(BF16), pool_scale)

    outs, lses = zip(*[dil_attn(qkv, g) for g in range(len(DIL_PATTERNS))])
    z_b = dil_combine(outs, lses)

    b_full = jnp.broadcast_to(sgu_b[:, :, None], (SGU_GROUPS, SGU_CHUNK, LANE))
    z_c = sgu_mixer(p_rest, sgu_v_norm, sgu_w.astype(BF16), b_full)

    q, k, v = mla_prep(
        p_rest, cos4, sin4, mla_q_a_norm, _split_heads(mla_w_uq, (MLA_NOPE, MLA_ROPE)).astype(BF16),
        mla_kv_a_norm, _split_heads(mla_w_ukv, (MLA_NOPE, MLA_V)).astype(BF16),
        jnp.tile(mla_q_norm[:MLA_NOPE], MLA_HEADS).reshape(1, -1),
        jnp.tile(mla_q_norm[MLA_NOPE:], MLA_HEADS).reshape(1, -1),
        jnp.tile(mla_k_norm[:MLA_NOPE], MLA_HEADS).reshape(1, -1),
        jnp.tile(mla_k_norm[MLA_NOPE:], MLA_HEADS).reshape(1, -1))
    z_d = mla_attn(q, k, v)

    merged = merge(h, (z_a, z_b, z_c, z_d), w_in[:, OFF_GATE:].astype(BF16), w_branch.astype(BF16))
    return out_proj(x, merged, w_out.astype(BF16))


def kernel(x, ffn1_norm, ffn1_w_gate, ffn1_w_up, ffn1_w_down, mix_norm, w_in, pool_w, pool_scale,
           dil_q_norm, dil_k_norm, sgu_v_norm, sgu_w, sgu_b, mla_q_a_norm, mla_w_uq, mla_kv_a_norm,
           mla_w_ukv, mla_q_norm, mla_k_norm, w_branch, w_out, ffn2_norm, ffn2_w_gate, ffn2_w_up,
           ffn2_w_down):
    b, s, d = x.shape
    cos4, sin4 = _rope_tables(s)
    outs = []
    for bi in range(b):
        xb = x[bi]
        for l in range(DEPTH):
            xb = ffn(xb, ffn1_norm[l], ffn1_w_gate[l].astype(BF16), ffn1_w_up[l].astype(BF16),
                     ffn1_w_down[l].astype(BF16))
            xb = mixer_block(xb, cos4, sin4, mix_norm[l], w_in[l], pool_w[l], pool_scale[l],
                             dil_q_norm[l], dil_k_norm[l], sgu_v_norm[l], sgu_w[l], sgu_b[l],
                             mla_q_a_norm[l], mla_w_uq[l], mla_kv_a_norm[l], mla_w_ukv[l],
                             mla_q_norm[l], mla_k_norm[l], w_branch[l], w_out[l])
            xb = ffn(xb, ffn2_norm[l], ffn2_w_gate[l].astype(BF16), ffn2_w_up[l].astype(BF16),
                     ffn2_w_down[l].astype(BF16))
        outs.append(xb)
    return jnp.stack(outs, axis=0)
```

```python
import functools
import math

import numpy as np
import jax
import jax.numpy as jnp
from jax import lax
from jax.experimental import pallas as pl
from jax.experimental.pallas import tpu as pltpu

F32 = jnp.float32
BF16 = jnp.bfloat16

D_MODEL = 2048
DEPTH = 4
D_FF = 5632
N_BRANCH = 4
BRANCH_W = 512
POOL_WINDOWS = (2, 4, 8, 16)
POOL_HALO = 8
DIL_PATTERNS = ((128, 1), (512, 4), (2048, 16))
DIL_GROUPS = len(DIL_PATTERNS)
DIL_HEADS = 12
DIL_HEADS_PER_GROUP = 4
DIL_HEAD_DIM = 128
DIL_QKV = 3 * DIL_HEADS * DIL_HEAD_DIM
DIL_SIDE = 64
SGU_CHUNK = 128
SGU_GROUPS = 4
MLA_HEADS = 4
MLA_Q_LORA = 384
MLA_KV_LORA = 128
MLA_NOPE = 128
MLA_ROPE = 64
MLA_V = 128
MLA_QK = MLA_NOPE + MLA_ROPE
MLA_PAD_QK = 256
MLA_HEADS_PER_STEP = 2
ROPE_THETA = 10000.0
EPS = 1e-6
NEG_BIG = -1e30
MLA_EXP2_SCALE = (MLA_QK ** -0.5) * math.log2(math.e)

OFF_POOL = 0
OFF_DIL = OFF_POOL + BRANCH_W
OFF_SGU = OFF_DIL + DIL_QKV
OFF_MLA_Q = OFF_SGU + 2 * BRANCH_W
OFF_MLA_KV = OFF_MLA_Q + MLA_Q_LORA
OFF_MLA_KR = OFF_MLA_KV + MLA_KV_LORA
OFF_GATE = OFF_MLA_KR + MLA_ROPE

LANE = 128
REST_POOL = 0
REST_SGU = 512
REST_CQ = 1536
REST_CKV = 1920
REST_KR = 2048
REST_W = REST_KR + LANE
W_MAIN_COLS = OFF_MLA_KR + LANE

VMEM_LIMIT = 56 * 1024 * 1024


def _params(*sem):
    return pltpu.CompilerParams(dimension_semantics=sem, vmem_limit_bytes=VMEM_LIMIT)


def _rms(xf, g):
    return xf * lax.rsqrt(jnp.mean(xf * xf, axis=-1, keepdims=True) + EPS) * g


def _ffn_body(x_ref, g_ref, wg_ref, wu_ref, wd_ref, o_ref, h_ref):
    @pl.when(pl.program_id(1) == 0)
    def _():
        x = x_ref[...]
        h_ref[...] = _rms(x, g_ref[...]).astype(BF16)
        o_ref[...] = x

    h = h_ref[...]
    g = jnp.dot(h, wg_ref[...], preferred_element_type=F32)
    u = jnp.dot(h, wu_ref[...], preferred_element_type=F32)
    a = (0.5 * (jax.nn.silu(g) * u)).astype(BF16)
    o_ref[...] += jnp.dot(a, wd_ref[...], preferred_element_type=F32)


def ffn(x, norm_g, wg, wu, wd, layer, *, tm=512, tf=512):
    s, d = x.shape
    f = wg.shape[2]
    return pl.pallas_call(
        _ffn_body,
        grid=(s // tm, f // tf),
        in_specs=[
            pl.BlockSpec((tm, d), lambda i, j: (i, 0)),
            pl.BlockSpec((1, d), lambda i, j: (0, 0)),
            pl.BlockSpec((None, d, tf), lambda i, j: (layer, 0, j)),
            pl.BlockSpec((None, d, tf), lambda i, j: (layer, 0, j)),
            pl.BlockSpec((None, tf, d), lambda i, j: (layer, j, 0)),
        ],
        out_specs=pl.BlockSpec((tm, d), lambda i, j: (i, 0)),
        out_shape=jax.ShapeDtypeStruct((s, d), F32),
        scratch_shapes=[pltpu.VMEM((tm, d), BF16)],
        compiler_params=_params("parallel", "arbitrary"),
        name="ffn",
    )(x, norm_g.reshape(1, d), wg, wu, wd)


def _proj_rest_body(x_ref, g_ref, wp_ref, ws_ref, wq_ref, wkv_ref, wkr_ref, p_ref, h_ref):
    h = _rms(x_ref[...], g_ref[...]).astype(BF16)
    h_ref[...] = h
    for w_ref, off in ((wp_ref, REST_POOL), (ws_ref, REST_SGU), (wq_ref, REST_CQ), (wkv_ref, REST_CKV),
                       (wkr_ref, REST_KR)):
        p_ref[:, off:off + w_ref.shape[1]] = jnp.dot(h, w_ref[...], preferred_element_type=F32)


def proj_rest(x, norm_g, w_main, layer, *, tm=512):
    s, d = x.shape

    def wspec(off, width):
        return pl.BlockSpec((None, d, width), lambda i: (layer, 0, off // width))

    return pl.pallas_call(
        _proj_rest_body,
        grid=(s // tm,),
        in_specs=[
            pl.BlockSpec((tm, d), lambda i: (i, 0)),
            pl.BlockSpec((1, d), lambda i: (0, 0)),
            wspec(OFF_POOL, BRANCH_W),
            wspec(OFF_SGU, 2 * BRANCH_W),
            wspec(OFF_MLA_Q, MLA_Q_LORA),
            wspec(OFF_MLA_KV, MLA_KV_LORA),
            wspec(OFF_MLA_KR, LANE),
        ],
        out_specs=[
            pl.BlockSpec((tm, REST_W), lambda i: (i, 0)),
            pl.BlockSpec((tm, d), lambda i: (i, 0)),
        ],
        out_shape=[jax.ShapeDtypeStruct((s, REST_W), F32), jax.ShapeDtypeStruct((s, d), BF16)],
        compiler_params=_params("parallel"),
        name="proj_rest",
    )(x, norm_g.reshape(1, d), w_main, w_main, w_main, w_main, w_main)


def _proj_dil_body(h_ref, w0_ref, w1_ref, w2_ref, g_ref, o0_ref, o1_ref, o2_ref, scr1_ref, scr2_ref):
    normed = pl.program_id(1) < 2
    h = h_ref[...]
    tm = h.shape[0]
    for w_ref, o_ref, scr_ref, (_, d) in zip((w0_ref, w1_ref, w2_ref), (o0_ref, o1_ref, o2_ref),
                                             (None, scr1_ref, scr2_ref), DIL_PATTERNS):
        acc = jnp.dot(h, w_ref[...], preferred_element_type=F32)
        for a in range(DIL_HEADS_PER_GROUP):
            sl = slice(a * DIL_HEAD_DIM, (a + 1) * DIL_HEAD_DIM)
            xh = acc[:, sl]
            r = lax.rsqrt(jnp.mean(xh * xh, axis=-1, keepdims=True) + EPS)
            y = xh * jnp.where(normed, r, 1.0) * g_ref[:, sl]
            if d == 1:
                o_ref[0, :, sl] = y.astype(BF16)
            else:
                scr_ref[a] = y
        if d > 1:
            for r in range(d):
                for a in range(DIL_HEADS_PER_GROUP):
                    sl = slice(a * DIL_HEAD_DIM, (a + 1) * DIL_HEAD_DIM)
                    o_ref[r, :, sl] = scr_ref[a, pl.ds(r, tm // d, stride=d), :].astype(BF16)


def proj_dil(h, w_main, gains, layer, *, tm=1024):
    s, dm = h.shape
    first = OFF_DIL // BRANCH_W

    def wspec(g):
        return pl.BlockSpec((None, dm, BRANCH_W), lambda i, j: (layer, 0, first + j * DIL_GROUPS + g))

    out_specs, out_shapes, scratch = [], [], []
    for _, d in DIL_PATTERNS:
        out_specs.append(pl.BlockSpec((d, tm // d, BRANCH_W), lambda i, j: (0, i, j)))
        out_shapes.append(jax.ShapeDtypeStruct((d, s // d, 3 * BRANCH_W), BF16))
        if d > 1:
            scratch.append(pltpu.VMEM((DIL_HEADS_PER_GROUP, tm, DIL_HEAD_DIM), F32))
    return pl.pallas_call(
        _proj_dil_body,
        grid=(s // tm, 3),
        in_specs=[
            pl.BlockSpec((tm, dm), lambda i, j: (i, 0)),
            wspec(0), wspec(1), wspec(2),
            pl.BlockSpec((None, 1, BRANCH_W), lambda i, j: (j, 0, 0)),
        ],
        out_specs=out_specs,
        out_shape=out_shapes,
        scratch_shapes=scratch,
        compiler_params=_params("parallel", "arbitrary"),
        name="proj_dil",
    )(h, w_main, w_main, w_main, gains)


def _pool_body(prev_ref, cur_ref, next_ref, w_ref, scale_ref, o_ref, ext_ref, *, seq):
    i = pl.program_id(0)
    tm = cur_ref.shape[0]
    cur = cur_ref[...]
    ext_ref[0:POOL_HALO, :] = jnp.where(i > 0, prev_ref[...], 0.0)
    ext_ref[POOL_HALO:POOL_HALO + tm, :] = cur
    ext_ref[POOL_HALO + tm:2 * POOL_HALO + tm, :] = jnp.where(
        i < pl.num_programs(0) - 1, next_ref[...], 0.0)
    t = i * tm + lax.broadcasted_iota(jnp.int32, (tm, 1), 0)
    for g, w in enumerate(POOL_WINDOWS):
        sl = slice(g * LANE, (g + 1) * LANE)
        tot = jnp.zeros((tm, LANE), F32)
        for o in range(-(w // 2), w // 2):
            tot = tot + ext_ref[POOL_HALO + o:POOL_HALO + o + tm, sl]
        cnt = (jnp.minimum(t + w // 2, seq) - jnp.maximum(t - w // 2, 0)).astype(F32)
        dev = (tot / cnt - cur[:, sl]).astype(BF16)
        y = jnp.dot(dev, w_ref[g], preferred_element_type=F32) * scale_ref[:, sl]
        o_ref[:, sl] = y.astype(BF16)


def pool_mixer(p_rest, pool_w, pool_scale, layer, *, tm=512):
    s = p_rest.shape[0]
    hb = tm // POOL_HALO
    last = s // POOL_HALO - 1
    n_win = len(POOL_WINDOWS)
    return pl.pallas_call(
        functools.partial(_pool_body, seq=s),
        grid=(s // tm,),
        in_specs=[
            pl.BlockSpec((POOL_HALO, BRANCH_W), lambda i: (jnp.maximum(i * hb - 1, 0), 0)),
            pl.BlockSpec((tm, BRANCH_W), lambda i: (i, 0)),
            pl.BlockSpec((POOL_HALO, BRANCH_W), lambda i: (jnp.minimum((i + 1) * hb, last), 0)),
            pl.BlockSpec((None, n_win, LANE, LANE), lambda i: (layer, 0, 0, 0)),
            pl.BlockSpec((1, BRANCH_W), lambda i: (0, 0)),
        ],
        out_specs=pl.BlockSpec((tm, BRANCH_W), lambda i: (i, 0)),
        out_shape=jax.ShapeDtypeStruct((s, BRANCH_W), BF16),
        scratch_shapes=[pltpu.VMEM((tm + 2 * POOL_HALO, BRANCH_W), F32)],
        compiler_params=_params("parallel"),
        name="pool_mixer",
    )(p_rest, p_rest, p_rest, pool_w, pool_scale.reshape(1, BRANCH_W))


def _alibi_slopes():
    n = DIL_HEADS
    return np.exp2(np.float32(-8.0) * np.arange(1, n + 1, dtype=np.float32) / np.float32(n))


def _dil_attn_body(q_ref, kp_ref, kc_ref, kn_ref, vp_ref, vc_ref, vn_ref, o_ref, lse_ref,
                   *, sub_len, penalties):
    i = pl.program_id(1)
    tq = q_ref.shape[0]
    half = tq - DIL_SIDE
    nk = tq + 2 * DIL_SIDE
    row = lax.broadcasted_iota(jnp.int32, (tq, nk), 0)
    col = lax.broadcasted_iota(jnp.int32, (tq, nk), 1)
    rel = col - DIL_SIDE - row
    kpos = i * tq + col - DIL_SIDE
    dist = jnp.abs(rel)
    valid = (dist <= DIL_SIDE) & (kpos >= 0) & (kpos < sub_len)
    distf = dist.astype(F32)
    for h in range(DIL_HEADS_PER_GROUP):
        sl = slice(h * DIL_HEAD_DIM, (h + 1) * DIL_HEAD_DIM)
        k = jnp.concatenate([kp_ref[half:, sl], kc_ref[:, sl], kn_ref[:DIL_SIDE, sl]], axis=0)
        v = jnp.concatenate([vp_ref[half:, sl], vc_ref[:, sl], vn_ref[:DIL_SIDE, sl]], axis=0)
        s = lax.dot_general(q_ref[:, sl], k, (((1,), (1,)), ((), ())),
                            preferred_element_type=F32) * (DIL_HEAD_DIM ** -0.5)
        s = s - penalties[h] * distf
        s = jnp.where(valid, s, NEG_BIG)
        m = jnp.max(s, axis=-1, keepdims=True)
        p = jnp.exp(s - m)
        den = jnp.sum(p, axis=-1, keepdims=True)
        o_ref[:, sl] = jnp.dot((p / den).astype(BF16), v, preferred_element_type=F32)
        lse_ref[:, sl] = jnp.broadcast_to(m + jnp.log(den), (tq, DIL_HEAD_DIM))


def dil_attn(qkv, group, *, tq=128):
    d, sub_len, _ = qkv.shape
    nb = sub_len // tq
    slopes = _alibi_slopes()[group * DIL_HEADS_PER_GROUP:(group + 1) * DIL_HEADS_PER_GROUP]
    penalties = tuple(float(np.float32(x) * np.float32(d)) for x in slopes)

    def spec(which, shift):
        return pl.BlockSpec((None, tq, BRANCH_W), lambda r, i: (r, jnp.clip(i + shift, 0, nb - 1), which))

    out_spec = pl.BlockSpec((None, tq, BRANCH_W), lambda r, i: (r, i, 0))
    return pl.pallas_call(
        functools.partial(_dil_attn_body, sub_len=sub_len, penalties=penalties),
        grid=(d, nb),
        in_specs=[spec(0, 0), spec(1, -1), spec(1, 0), spec(1, 1), spec(2, -1), spec(2, 0), spec(2, 1)],
        out_specs=[out_spec, out_spec],
        out_shape=[jax.ShapeDtypeStruct((d, sub_len, BRANCH_W), F32)] * 2,
        compiler_params=_params("parallel", "parallel"),
        name=f"dil_attn_g{group}",
    )(qkv, qkv, qkv, qkv, qkv, qkv, qkv)


def _dil_combine_body(o0, o1, o2, l0, l1, l2, z_ref, *scratch):
    tm = z_ref.shape[0]
    nat = {}
    for idx, (src, (_, d)) in enumerate(((o1, DIL_PATTERNS[1]), (o2, DIL_PATTERNS[2]),
                                         (l1, DIL_PATTERNS[1]), (l2, DIL_PATTERNS[2]))):
        scr = scratch[idx]
        for r in range(d):
            for a in range(DIL_HEADS_PER_GROUP):
                sl = slice(a * DIL_HEAD_DIM, (a + 1) * DIL_HEAD_DIM)
                scr[a, pl.ds(r, tm // d, stride=d), :] = src[r, :, sl]
        nat[idx] = scr
    for a in range(DIL_HEADS_PER_GROUP):
        sl = slice(a * DIL_HEAD_DIM, (a + 1) * DIL_HEAD_DIM)
        la, lb, lc = l0[0, :, sl], nat[2][a], nat[3][a]
        m = jnp.maximum(jnp.maximum(la, lb), lc)
        ea, eb, ec = jnp.exp(la - m), jnp.exp(lb - m), jnp.exp(lc - m)
        tot = ea + eb + ec
        z = (ea / tot) * o0[0, :, sl] + (eb / tot) * nat[0][a] + (ec / tot) * nat[1][a]
        z_ref[:, sl] = z.astype(BF16)


def dil_combine(outs, lses, *, tm=512):
    s = outs[0].shape[1]
    specs = [pl.BlockSpec((d, tm // d, BRANCH_W), lambda i: (0, i, 0)) for _, d in DIL_PATTERNS]
    return pl.pallas_call(
        _dil_combine_body,
        grid=(s // tm,),
        in_specs=specs + specs,
        out_specs=pl.BlockSpec((tm, BRANCH_W), lambda i: (i, 0)),
        out_shape=jax.ShapeDtypeStruct((s, BRANCH_W), BF16),
        scratch_shapes=[pltpu.VMEM((DIL_HEADS_PER_GROUP, tm, DIL_HEAD_DIM), F32)] * 4,
        compiler_params=_params("parallel"),
        name="dil_combine",
    )(*outs, *lses)


def _sgu_body(u_ref, v_ref, g_ref, ws_ref, b_ref, o_ref):
    tm = u_ref.shape[0]
    u = jax.nn.gelu(u_ref[...])
    vn = _rms(jax.nn.gelu(v_ref[...]), g_ref[...]).astype(BF16)
    for n in range(tm // SGU_CHUNK):
        rows = slice(n * SGU_CHUNK, (n + 1) * SGU_CHUNK)
        for g in range(SGU_GROUPS):
            cols = slice(g * LANE, (g + 1) * LANE)
            mixed = jnp.dot(ws_ref[g], vn[rows, cols], preferred_element_type=F32) + b_ref[g]
            o_ref[rows, cols] = (u[rows, cols] * mixed).astype(BF16)


def sgu_mixer(p_rest, v_norm_g, ws, b_full, layer, *, tm=512):
    s = p_rest.shape[0]
    return pl.pallas_call(
        _sgu_body,
        grid=(s // tm,),
        in_specs=[
            pl.BlockSpec((tm, BRANCH_W), lambda i: (i, REST_SGU // BRANCH_W)),
            pl.BlockSpec((tm, BRANCH_W), lambda i: (i, REST_SGU // BRANCH_W + 1)),
            pl.BlockSpec((1, BRANCH_W), lambda i: (0, 0)),
            pl.BlockSpec((None, SGU_GROUPS, SGU_CHUNK, SGU_CHUNK), lambda i: (layer, 0, 0, 0)),
            pl.BlockSpec((SGU_GROUPS, SGU_CHUNK, LANE), lambda i: (0, 0, 0)),
        ],
        out_specs=pl.BlockSpec((tm, BRANCH_W), lambda i: (i, 0)),
        out_shape=jax.ShapeDtypeStruct((s, BRANCH_W), BF16),
        compiler_params=_params("parallel"),
        name="sgu_mixer",
    )(p_rest, p_rest, v_norm_g.reshape(1, BRANCH_W), ws, b_full)


def _swap_rope_halves(y):
    width = y.shape[1]
    lane = lax.broadcasted_iota(jnp.int32, y.shape, 1)
    first_half = (lane % MLA_ROPE) < MLA_ROPE // 2
    return jnp.where(first_half, pltpu.roll(y, width - MLA_ROPE // 2, 1),
                     pltpu.roll(y, MLA_ROPE // 2, 1))


def _mla_prep_body(cq_ref, ckv_ref, kr_ref, cos_ref, sin_ref, qa_g, wuq_ref, kva_g, wukv_ref,
                   gqn_ref, gqr_ref, gkn_ref, gkr_ref, q_ref, k_ref, v_ref):
    nope_w = MLA_HEADS * MLA_NOPE
    qall = jnp.dot(_rms(cq_ref[...], qa_g[...]).astype(BF16), wuq_ref[...], preferred_element_type=F32)
    kv = jnp.dot(_rms(ckv_ref[...], kva_g[...]).astype(BF16), wukv_ref[...], preferred_element_type=F32)
    qn, qr = qall[:, :nope_w], qall[:, nope_w:]
    kn = kv[:, :nope_w]
    v_ref[...] = kv[:, nope_w:].astype(BF16)
    kr_blk = kr_ref[...]
    lane128 = lax.broadcasted_iota(jnp.int32, kr_blk.shape, 1)
    kr2 = jnp.where(lane128 < MLA_ROPE, kr_blk, pltpu.roll(kr_blk, MLA_ROPE, 1))
    kr = jnp.concatenate([kr2] * (MLA_HEADS // 2), axis=1)

    rope_w = MLA_HEADS * MLA_ROPE
    head_of_lane = lax.broadcasted_iota(jnp.int32, (1, rope_w), 1) // MLA_ROPE
    qr_sq = qr * qr
    kr_ssq = jnp.sum(jnp.where(head_of_lane == 0, kr * kr, 0.0), axis=-1, keepdims=True)
    rq, rk = [], []
    for h in range(MLA_HEADS):
        sl = slice(h * MLA_NOPE, (h + 1) * MLA_NOPE)
        q_ssq = (jnp.sum(qn[:, sl] * qn[:, sl], axis=-1, keepdims=True)
                 + jnp.sum(jnp.where(head_of_lane == h, qr_sq, 0.0), axis=-1, keepdims=True))
        k_ssq = jnp.sum(kn[:, sl] * kn[:, sl], axis=-1, keepdims=True) + kr_ssq
        rq.append(lax.rsqrt(q_ssq / MLA_QK + EPS))
        rk.append(lax.rsqrt(k_ssq / MLA_QK + EPS))

    def per_lane(rs):
        out = jnp.zeros((kr.shape[0], rope_w), F32)
        for h in range(MLA_HEADS):
            out = jnp.where(head_of_lane == h, rs[h], out)
        return out

    def rotate(x, rs, g_ref):
        y = x * per_lane(rs) * g_ref[...]
        return y * cos_ref[...] + _swap_rope_halves(y) * sin_ref[...]

    q_rope = rotate(qr, rq, gqr_ref) * MLA_EXP2_SCALE
    k_rope = rotate(kr, rk, gkr_ref)
    lane_tile = lax.broadcasted_iota(jnp.int32, (1, LANE), 1) // MLA_ROPE
    for h in range(MLA_HEADS):
        sl = slice(h * MLA_NOPE, (h + 1) * MLA_NOPE)
        base = h * MLA_PAD_QK
        tile = slice((h // 2) * LANE, (h // 2 + 1) * LANE)
        q_ref[:, base:base + MLA_NOPE] = (qn[:, sl] * rq[h] * gqn_ref[:, sl] * MLA_EXP2_SCALE).astype(BF16)
        k_ref[:, base:base + MLA_NOPE] = (kn[:, sl] * rk[h] * gkn_ref[:, sl]).astype(BF16)
        q_ref[:, base + MLA_NOPE:base + MLA_PAD_QK] = jnp.where(
            lane_tile == h % 2, q_rope[:, tile], 0.0).astype(BF16)
        k_ref[:, base + MLA_NOPE:base + MLA_PAD_QK] = k_rope[:, tile].astype(BF16)


def mla_prep(p_rest, cos4, sin4, qa_g, wuq, kva_g, wukv, gqn, gqr, gkn, gkr, *, tm=512):
    s = p_rest.shape[0]
    rope_w = MLA_HEADS * MLA_ROPE

    def const(shape):
        return pl.BlockSpec(shape, lambda i: (0,) * len(shape))

    return pl.pallas_call(
        _mla_prep_body,
        grid=(s // tm,),
        in_specs=[
            pl.BlockSpec((tm, MLA_Q_LORA), lambda i: (i, REST_CQ // MLA_Q_LORA)),
            pl.BlockSpec((tm, MLA_KV_LORA), lambda i: (i, REST_CKV // MLA_KV_LORA)),
            pl.BlockSpec((tm, LANE), lambda i: (i, REST_KR // LANE)),
            pl.BlockSpec((tm, rope_w), lambda i: (i, 0)),
            pl.BlockSpec((tm, rope_w), lambda i: (i, 0)),
            const((1, MLA_Q_LORA)), const(wuq.shape), const((1, MLA_KV_LORA)), const(wukv.shape),
            const(gqn.shape), const(gqr.shape), const(gkn.shape), const(gkr.shape),
        ],
        out_specs=[
            pl.BlockSpec((tm, MLA_HEADS * MLA_PAD_QK), lambda i: (i, 0)),
            pl.BlockSpec((tm, MLA_HEADS * MLA_PAD_QK), lambda i: (i, 0)),
            pl.BlockSpec((tm, MLA_HEADS * MLA_V), lambda i: (i, 0)),
        ],
        out_shape=[
            jax.ShapeDtypeStruct((s, MLA_HEADS * MLA_PAD_QK), BF16),
            jax.ShapeDtypeStruct((s, MLA_HEADS * MLA_PAD_QK), BF16),
            jax.ShapeDtypeStruct((s, MLA_HEADS * MLA_V), BF16),
        ],
        compiler_params=_params("parallel"),
        name="mla_prep",
    )(p_rest, p_rest, p_rest, cos4, sin4, qa_g.reshape(1, -1), wuq, kva_g.reshape(1, -1), wukv,
      gqn, gqr, gkn, gkr)


def _mla_attn_body(q_ref, k_ref, v_ref, o_ref, m_ref, l_ref, acc_ref):
    j = pl.program_id(2)

    @pl.when(j == 0)
    def _():
        m_ref[...] = jnp.full(m_ref.shape, NEG_BIG, F32)
        l_ref[...] = jnp.zeros(l_ref.shape, F32)
        acc_ref[...] = jnp.zeros(acc_ref.shape, F32)

    tk = k_ref.shape[0]
    for h in range(MLA_HEADS_PER_STEP):
        qk = slice(h * MLA_PAD_QK, (h + 1) * MLA_PAD_QK)
        vs = slice(h * MLA_V, (h + 1) * MLA_V)
        s = lax.dot_general(q_ref[:, qk], k_ref[:, qk], (((1,), (1,)), ((), ())),
                            preferred_element_type=F32)
        m_prev = m_ref[h]
        m_new = jnp.maximum(m_prev, jnp.max(s, axis=-1, keepdims=True))
        alpha = jnp.exp2(m_prev - m_new)
        p = jnp.exp2(s - jnp.tile(m_new, (1, tk // LANE)))
        l_ref[h] = alpha * l_ref[h] + jnp.sum(p, axis=-1, keepdims=True)
        acc_ref[h] = alpha * acc_ref[h] + jnp.dot(p.astype(BF16), v_ref[:, vs], preferred_element_type=F32)
        m_ref[h] = m_new

    @pl.when(j == pl.num_programs(2) - 1)
    def _():
        for h in range(MLA_HEADS_PER_STEP):
            o_ref[:, h * MLA_V:(h + 1) * MLA_V] = (acc_ref[h] / l_ref[h]).astype(BF16)


def mla_attn(q, k, v, *, tq=1024, tk=1024):
    s = q.shape[0]
    hp = MLA_HEADS_PER_STEP
    stat = pltpu.VMEM((hp, tq, LANE), F32)
    return pl.pallas_call(
        _mla_attn_body,
        grid=(s // tq, MLA_HEADS // hp, s // tk),
        in_specs=[
            pl.BlockSpec((tq, hp * MLA_PAD_QK), lambda i, h, j: (i, h)),
            pl.BlockSpec((tk, hp * MLA_PAD_QK), lambda i, h, j: (j, h)),
            pl.BlockSpec((tk, hp * MLA_V), lambda i, h, j: (j, h)),
        ],
        out_specs=pl.BlockSpec((tq, hp * MLA_V), lambda i, h, j: (i, h)),
        out_shape=jax.ShapeDtypeStruct((s, MLA_HEADS * MLA_V), BF16),
        scratch_shapes=[stat, stat, stat],
        compiler_params=_params("parallel", "parallel", "arbitrary"),
        name="mla_attn",
    )(q, k, v)


def _merge_body(h_ref, za_ref, zb_ref, zc_ref, zd_ref, wg_ref, wb_ref, o_ref, acc_ref):
    n = pl.program_id(2)

    @pl.when(n == 0)
    def _():
        acc_ref[...] = jnp.zeros(acc_ref.shape, F32)

    gate = jax.nn.sigmoid(jnp.dot(h_ref[...], wg_ref[...], preferred_element_type=F32))
    for b, z_ref in enumerate((za_ref, zb_ref, zc_ref, zd_ref)):
        @pl.when(n == b)
        def _(z_ref=z_ref):
            acc_ref[...] += gate * jnp.dot(z_ref[...], wb_ref[...], preferred_element_type=F32)

    @pl.when(n == N_BRANCH - 1)
    def _():
        o_ref[...] = acc_ref[...].astype(BF16)


def merge(h, zs, w_gate, w_branch, layer, *, tm=1024, tn=512):
    s, d = h.shape
    ct = d // tn
    z_spec = pl.BlockSpec((tm, BRANCH_W), lambda i, c, n: (i, 0))
    return pl.pallas_call(
        _merge_body,
        grid=(s // tm, ct, N_BRANCH),
        in_specs=[
            pl.BlockSpec((tm, d), lambda i, c, n: (i, 0)),
            z_spec, z_spec, z_spec, z_spec,
            pl.BlockSpec((None, d, tn), lambda i, c, n: (layer, 0, n * ct + c)),
            pl.BlockSpec((None, None, BRANCH_W, tn), lambda i, c, n: (layer, n, 0, c)),
        ],
        out_specs=pl.BlockSpec((tm, tn), lambda i, c, n: (i, c)),
        out_shape=jax.ShapeDtypeStruct((s, d), BF16),
        scratch_shapes=[pltpu.VMEM((tm, tn), F32)],
        compiler_params=_params("parallel", "arbitrary", "arbitrary"),
        name="merge",
    )(h, *zs, w_gate, w_branch)


def _out_proj_body(x_ref, y_ref, w_ref, o_ref):
    o_ref[...] = x_ref[...] + jnp.dot(y_ref[...], w_ref[...], preferred_element_type=F32)


def out_proj(x, y, w, layer, *, tm=1024, tn=512):
    s, d = x.shape
    return pl.pallas_call(
        _out_proj_body,
        grid=(s // tm, d // tn),
        in_specs=[
            pl.BlockSpec((tm, tn), lambda i, j: (i, j)),
            pl.BlockSpec((tm, d), lambda i, j: (i, 0)),
            pl.BlockSpec((None, d, tn), lambda i, j: (layer, 0, j)),
        ],
        out_specs=pl.BlockSpec((tm, tn), lambda i, j: (i, j)),
        out_shape=jax.ShapeDtypeStruct((s, d), F32),
        compiler_params=_params("parallel", "arbitrary"),
        name="out_proj",
    )(x, y, w)


def _rope_tables(s):
    pos = jnp.arange(s, dtype=F32)
    inv_freq = ROPE_THETA ** (-jnp.arange(0, MLA_ROPE, 2, dtype=F32) / MLA_ROPE)
    ang = pos[:, None] * inv_freq[None, :]
    cos, sin = jnp.cos(ang), jnp.sin(ang)
    cos4 = jnp.tile(jnp.concatenate([cos, cos], axis=-1), (1, MLA_HEADS))
    sin4 = jnp.tile(jnp.concatenate([-sin, sin], axis=-1), (1, MLA_HEADS))
    return cos4, sin4


def _split_heads(w, widths):
    per_head = sum(widths)
    w3 = w.reshape(w.shape[0], MLA_HEADS, per_head)
    parts, off = [], 0
    for wd in widths:
        parts.append(w3[:, :, off:off + wd].reshape(w.shape[0], MLA_HEADS * wd))
        off += wd
    return jnp.concatenate(parts, axis=1)


def kernel(x, ffn1_norm, ffn1_w_gate, ffn1_w_up, ffn1_w_down, mix_norm, w_in, pool_w, pool_scale,
           dil_q_norm, dil_k_norm, sgu_v_norm, sgu_w, sgu_b, mla_q_a_norm, mla_w_uq, mla_kv_a_norm,
           mla_w_ukv, mla_q_norm, mla_k_norm, w_branch, w_out, ffn2_norm, ffn2_w_gate, ffn2_w_up,
           ffn2_w_down):
    b, s, d = x.shape
    cos4, sin4 = _rope_tables(s)
    f1g, f1u, f1d = ffn1_w_gate.astype(BF16), ffn1_w_up.astype(BF16), ffn1_w_down.astype(BF16)
    f2g, f2u, f2d = ffn2_w_gate.astype(BF16), ffn2_w_up.astype(BF16), ffn2_w_down.astype(BF16)
    w_main = w_in[:, :, :W_MAIN_COLS].astype(BF16)
    w_gate = w_in[:, :, OFF_GATE:].astype(BF16)
    w_br, w_o = w_branch.astype(BF16), w_out.astype(BF16)
    pool_wb, sgu_wb = pool_w.astype(BF16), sgu_w.astype(BF16)
    ones = jnp.ones((BRANCH_W,), F32)

    outs = []
    for bi in range(b):
        xb = x[bi]
        for l in range(DEPTH):
            xb = ffn(xb, ffn1_norm[l], f1g, f1u, f1d, l)

            p_rest, h = proj_rest(xb, mix_norm[l], w_main, l)
            dil_gains = jnp.stack([jnp.tile(dil_q_norm[l], DIL_HEADS_PER_GROUP),
                                   jnp.tile(dil_k_norm[l], DIL_HEADS_PER_GROUP), ones]).reshape(3, 1, BRANCH_W)
            qkvs = proj_dil(h, w_main, dil_gains, l)
            z_a = pool_mixer(p_rest, pool_wb, pool_scale[l], l)
            o_l = [dil_attn(qkvs[g], g) for g in range(DIL_GROUPS)]
            z_b = dil_combine([o for o, _ in o_l], [ls for _, ls in o_l])
            b_full = jnp.broadcast_to(sgu_b[l][:, :, None], (SGU_GROUPS, SGU_CHUNK, LANE))
            z_c = sgu_mixer(p_rest, sgu_v_norm[l], sgu_wb, b_full, l)
            q, k, v = mla_prep(
                p_rest, cos4, sin4, mla_q_a_norm[l],
                _split_heads(mla_w_uq[l], (MLA_NOPE, MLA_ROPE)).astype(BF16),
                mla_kv_a_norm[l], _split_heads(mla_w_ukv[l], (MLA_NOPE, MLA_V)).astype(BF16),
                jnp.tile(mla_q_norm[l][:MLA_NOPE], MLA_HEADS).reshape(1, -1),
                jnp.tile(mla_q_norm[l][MLA_NOPE:], MLA_HEADS).reshape(1, -1),
                jnp.tile(mla_k_norm[l][:MLA_NOPE], MLA_HEADS).reshape(1, -1),
                jnp.tile(mla_k_norm[l][MLA_NOPE:], MLA_HEADS).reshape(1, -1))
            z_d = mla_attn(q, k, v)
            merged = merge(h, (z_a, z_b, z_c, z_d), w_gate, w_br, l)
            xb = out_proj(xb, merged, w_o, l)

            xb = ffn(xb, ffn2_norm[l], f2g, f2u, f2d, l)
        outs.append(xb)
    return jnp.stack(outs, axis=0)
```

```python
import functools
import math

import numpy as np
import jax
import jax.numpy as jnp
from jax import lax
from jax.experimental import pallas as pl
from jax.experimental.pallas import tpu as pltpu

F32 = jnp.float32
BF16 = jnp.bfloat16

D_MODEL = 2048
DEPTH = 4
D_FF = 5632
N_BRANCH = 4
BRANCH_W = 512
POOL_WINDOWS = (2, 4, 8, 16)
POOL_HALO = 8
DIL_PATTERNS = ((128, 1), (512, 4), (2048, 16))
DIL_GROUPS = len(DIL_PATTERNS)
DIL_HEADS = 12
DIL_HEADS_PER_GROUP = 4
DIL_HEAD_DIM = 128
DIL_QKV = 3 * DIL_HEADS * DIL_HEAD_DIM
DIL_SIDE = 64
SGU_CHUNK = 128
SGU_GROUPS = 4
MLA_HEADS = 4
MLA_Q_LORA = 384
MLA_KV_LORA = 128
MLA_NOPE = 128
MLA_ROPE = 64
MLA_V = 128
MLA_QK = MLA_NOPE + MLA_ROPE
MLA_PAD_QK = 256
MLA_HEADS_PER_STEP = 2
ROPE_THETA = 10000.0
EPS = 1e-6
NEG_BIG = -1e30
MLA_EXP2_SCALE = (MLA_QK ** -0.5) * math.log2(math.e)

OFF_POOL = 0
OFF_DIL = OFF_POOL + BRANCH_W
OFF_SGU = OFF_DIL + DIL_QKV
OFF_MLA_Q = OFF_SGU + 2 * BRANCH_W
OFF_MLA_KV = OFF_MLA_Q + MLA_Q_LORA
OFF_MLA_KR = OFF_MLA_KV + MLA_KV_LORA
OFF_GATE = OFF_MLA_KR + MLA_ROPE

LANE = 128
REST_POOL = 0
REST_SGU = 512
REST_CQ = 1536
REST_CKV = 1920
REST_KR = 2048
REST_W = REST_KR + LANE

VMEM_LIMIT = 56 * 1024 * 1024


def _params(*sem):
    return pltpu.CompilerParams(dimension_semantics=sem, vmem_limit_bytes=VMEM_LIMIT)


def _rms(xf, g):
    return xf * lax.rsqrt(jnp.mean(xf * xf, axis=-1, keepdims=True) + EPS) * g


def _ffn_body(x_hbm, g_ref, wg_ref, wu_ref, wd_ref, o_ref, h_ref, sem):
    tm = o_ref.shape[0]

    @pl.when(pl.program_id(1) == 0)
    def _():
        cp = pltpu.make_async_copy(x_hbm.at[pl.ds(pl.program_id(0) * tm, tm), :], o_ref, sem)
        cp.start()
        cp.wait()
        h_ref[...] = _rms(o_ref[...], g_ref[...]).astype(BF16)

    h = h_ref[...]
    g = jnp.dot(h, wg_ref[...], preferred_element_type=F32)
    u = jnp.dot(h, wu_ref[...], preferred_element_type=F32)
    a = (0.5 * (jax.nn.silu(g) * u)).astype(BF16)
    o_ref[...] += jnp.dot(a, wd_ref[...], preferred_element_type=F32)


def ffn(x, norm_g, wg, wu, wd, layer, *, tm=1024, tf=512):
    s, d = x.shape
    f = wg.shape[2]
    return pl.pallas_call(
        _ffn_body,
        grid=(s // tm, f // tf),
        in_specs=[
            pl.BlockSpec(memory_space=pl.ANY),
            pl.BlockSpec((1, d), lambda i, j: (0, 0)),
            pl.BlockSpec((None, d, tf), lambda i, j: (layer, 0, j)),
            pl.BlockSpec((None, d, tf), lambda i, j: (layer, 0, j)),
            pl.BlockSpec((None, tf, d), lambda i, j: (layer, j, 0)),
        ],
        out_specs=pl.BlockSpec((tm, d), lambda i, j: (i, 0)),
        out_shape=jax.ShapeDtypeStruct((s, d), F32),
        scratch_shapes=[pltpu.VMEM((tm, d), BF16), pltpu.SemaphoreType.DMA],
        compiler_params=_params("parallel", "arbitrary"),
        name="ffn",
    )(x, norm_g.reshape(1, d), wg, wu, wd)


def _proj_rest_body(x_ref, g_ref, wp_ref, ws_ref, wq_ref, wkv_ref, wkr_ref, p_ref, h_ref):
    h = _rms(x_ref[...], g_ref[...]).astype(BF16)
    h_ref[...] = h
    for w_ref, off in ((wp_ref, REST_POOL), (ws_ref, REST_SGU), (wq_ref, REST_CQ), (wkv_ref, REST_CKV),
                       (wkr_ref, REST_KR)):
        p_ref[:, off:off + w_ref.shape[1]] = jnp.dot(h, w_ref[...], preferred_element_type=F32)


def proj_rest(x, norm_g, w_main, layer, *, tm=512):
    s, d = x.shape

    def wspec(off, width):
        return pl.BlockSpec((None, d, width), lambda i: (layer, 0, off // width))

    return pl.pallas_call(
        _proj_rest_body,
        grid=(s // tm,),
        in_specs=[
            pl.BlockSpec((tm, d), lambda i: (i, 0)),
            pl.BlockSpec((1, d), lambda i: (0, 0)),
            wspec(OFF_POOL, BRANCH_W),
            wspec(OFF_SGU, 2 * BRANCH_W),
            wspec(OFF_MLA_Q, MLA_Q_LORA),
            wspec(OFF_MLA_KV, MLA_KV_LORA),
            wspec(OFF_MLA_KR, LANE),
        ],
        out_specs=[
            pl.BlockSpec((tm, REST_W), lambda i: (i, 0)),
            pl.BlockSpec((tm, d), lambda i: (i, 0)),
        ],
        out_shape=[jax.ShapeDtypeStruct((s, REST_W), F32), jax.ShapeDtypeStruct((s, d), BF16)],
        compiler_params=_params("parallel"),
        name="proj_rest",
    )(x, norm_g.reshape(1, d), w_main, w_main, w_main, w_main, w_main)


def _proj_dil_body(h_ref, w0_ref, w1_ref, w2_ref, g_ref, o0_ref, o1_ref, o2_ref, scr1_ref, scr2_ref):
    normed = pl.program_id(1) < 2
    h = h_ref[...]
    tm = h.shape[0]
    for w_ref, o_ref, scr_ref, (_, d) in zip((w0_ref, w1_ref, w2_ref), (o0_ref, o1_ref, o2_ref),
                                             (None, scr1_ref, scr2_ref), DIL_PATTERNS):
        acc = jnp.dot(h, w_ref[...], preferred_element_type=F32)
        for a in range(DIL_HEADS_PER_GROUP):
            sl = slice(a * DIL_HEAD_DIM, (a + 1) * DIL_HEAD_DIM)
            xh = acc[:, sl]
            r = lax.rsqrt(jnp.mean(xh * xh, axis=-1, keepdims=True) + EPS)
            y = xh * jnp.where(normed, r, 1.0) * g_ref[:, sl]
            if d == 1:
                o_ref[0, :, sl] = y.astype(BF16)
            else:
                scr_ref[a] = y
        if d > 1:
            for r in range(d):
                for a in range(DIL_HEADS_PER_GROUP):
                    sl = slice(a * DIL_HEAD_DIM, (a + 1) * DIL_HEAD_DIM)
                    o_ref[r, :, sl] = scr_ref[a, pl.ds(r, tm // d, stride=d), :].astype(BF16)


def proj_dil(h, w_main, gains, layer, *, tm=1024):
    s, dm = h.shape
    first = OFF_DIL // BRANCH_W

    def wspec(g):
        return pl.BlockSpec((None, dm, BRANCH_W), lambda i, j: (layer, 0, first + j * DIL_GROUPS + g))

    out_specs, out_shapes, scratch = [], [], []
    for _, d in DIL_PATTERNS:
        out_specs.append(pl.BlockSpec((d, tm // d, BRANCH_W), lambda i, j: (0, i, j)))
        out_shapes.append(jax.ShapeDtypeStruct((d, s // d, 3 * BRANCH_W), BF16))
        if d > 1:
            scratch.append(pltpu.VMEM((DIL_HEADS_PER_GROUP, tm, DIL_HEAD_DIM), F32))
    return pl.pallas_call(
        _proj_dil_body,
        grid=(s // tm, 3),
        in_specs=[
            pl.BlockSpec((tm, dm), lambda i, j: (i, 0)),
            wspec(0), wspec(1), wspec(2),
            pl.BlockSpec((None, 1, BRANCH_W), lambda i, j: (j, 0, 0)),
        ],
        out_specs=out_specs,
        out_shape=out_shapes,
        scratch_shapes=scratch,
        compiler_params=_params("parallel", "arbitrary"),
        name="proj_dil",
    )(h, w_main, w_main, w_main, gains)


def _pool_body(prev_ref, cur_ref, next_ref, w_ref, scale_ref, o_ref, ext_ref, *, seq):
    i = pl.program_id(0)
    tm = cur_ref.shape[0]
    cur = cur_ref[...]
    ext_ref[0:POOL_HALO, :] = jnp.where(i > 0, prev_ref[...], 0.0)
    ext_ref[POOL_HALO:POOL_HALO + tm, :] = cur
    ext_ref[POOL_HALO + tm:2 * POOL_HALO + tm, :] = jnp.where(
        i < pl.num_programs(0) - 1, next_ref[...], 0.0)
    t = i * tm + lax.broadcasted_iota(jnp.int32, (tm, 1), 0)
    for g, w in enumerate(POOL_WINDOWS):
        sl = slice(g * LANE, (g + 1) * LANE)
        tot = jnp.zeros((tm, LANE), F32)
        for o in range(-(w // 2), w // 2):
            tot = tot + ext_ref[POOL_HALO + o:POOL_HALO + o + tm, sl]
        cnt = (jnp.minimum(t + w // 2, seq) - jnp.maximum(t - w // 2, 0)).astype(F32)
        dev = (tot / cnt - cur[:, sl]).astype(BF16)
        y = jnp.dot(dev, w_ref[g], preferred_element_type=F32) * scale_ref[:, sl]
        o_ref[:, sl] = y.astype(BF16)


def pool_mixer(p_rest, pool_w, pool_scale, layer, *, tm=512):
    s = p_rest.shape[0]
    hb = tm // POOL_HALO
    last = s // POOL_HALO - 1
    n_win = len(POOL_WINDOWS)
    return pl.pallas_call(
        functools.partial(_pool_body, seq=s),
        grid=(s // tm,),
        in_specs=[
            pl.BlockSpec((POOL_HALO, BRANCH_W), lambda i: (jnp.maximum(i * hb - 1, 0), 0)),
            pl.BlockSpec((tm, BRANCH_W), lambda i: (i, 0)),
            pl.BlockSpec((POOL_HALO, BRANCH_W), lambda i: (jnp.minimum((i + 1) * hb, last), 0)),
            pl.BlockSpec((None, n_win, LANE, LANE), lambda i: (layer, 0, 0, 0)),
            pl.BlockSpec((1, BRANCH_W), lambda i: (0, 0)),
        ],
        out_specs=pl.BlockSpec((tm, BRANCH_W), lambda i: (i, 0)),
        out_shape=jax.ShapeDtypeStruct((s, BRANCH_W), BF16),
        scratch_shapes=[pltpu.VMEM((tm + 2 * POOL_HALO, BRANCH_W), F32)],
        compiler_params=_params("parallel"),
        name="pool_mixer",
    )(p_rest, p_rest, p_rest, pool_w, pool_scale.reshape(1, BRANCH_W))


def _alibi_slopes():
    n = DIL_HEADS
    return np.exp2(np.float32(-8.0) * np.arange(1, n + 1, dtype=np.float32) / np.float32(n))


def _dil_attn_body(q_ref, kp_ref, kc_ref, kn_ref, vp_ref, vc_ref, vn_ref, o_ref, lse_ref,
                   *, sub_len, penalties):
    i = pl.program_id(1)
    t = q_ref.shape[0]
    blk = kp_ref.shape[0]
    nk = blk + 2 * DIL_SIDE
    row = lax.broadcasted_iota(jnp.int32, (blk, nk), 0)
    col = lax.broadcasted_iota(jnp.int32, (blk, nk), 1)
    dist = jnp.abs(col - DIL_SIDE - row)
    in_band = dist <= DIL_SIDE
    distf = dist.astype(F32)
    for h in range(DIL_HEADS_PER_GROUP):
        sl = slice(h * DIL_HEAD_DIM, (h + 1) * DIL_HEAD_DIM)
        k_ext = jnp.concatenate([kp_ref[blk - DIL_SIDE:, sl], kc_ref[:, sl], kn_ref[:DIL_SIDE, sl]], axis=0)
        v_ext = jnp.concatenate([vp_ref[blk - DIL_SIDE:, sl], vc_ref[:, sl], vn_ref[:DIL_SIDE, sl]], axis=0)
        bias = jnp.where(in_band, -penalties[h] * distf, NEG_BIG)
        for b in range(t // blk):
            rows = slice(b * blk, (b + 1) * blk)
            kpos = i * t + b * blk + col - DIL_SIDE
            s = lax.dot_general(q_ref[rows, sl], k_ext[b * blk:b * blk + nk], (((1,), (1,)), ((), ())),
                                preferred_element_type=F32) * (DIL_HEAD_DIM ** -0.5)
            s = jnp.where((kpos >= 0) & (kpos < sub_len), s + bias, NEG_BIG)
            m = jnp.max(s, axis=-1, keepdims=True)
            p = jnp.exp(s - m)
            den = jnp.sum(p, axis=-1, keepdims=True)
            o_ref[rows, sl] = jnp.dot((p / den).astype(BF16), v_ext[b * blk:b * blk + nk],
                                      preferred_element_type=F32)
            lse_ref[rows, sl] = jnp.broadcast_to(m + jnp.log(den), (blk, DIL_HEAD_DIM))


def dil_attn(qkv, group, *, blk=128, max_blocks=8):
    d, sub_len, _ = qkv.shape
    n_blk = sub_len // blk
    qb = min(max_blocks, n_blk)
    t = blk * qb
    slopes = _alibi_slopes()[group * DIL_HEADS_PER_GROUP:(group + 1) * DIL_HEADS_PER_GROUP]
    penalties = tuple(float(np.float32(x) * np.float32(d)) for x in slopes)

    def halo(which, step):
        return pl.BlockSpec((None, blk, BRANCH_W),
                            lambda r, i: (r, jnp.clip(i * qb + step, 0, n_blk - 1), which))

    def cur(which):
        return pl.BlockSpec((None, t, BRANCH_W), lambda r, i: (r, i, which))

    out_spec = pl.BlockSpec((None, t, BRANCH_W), lambda r, i: (r, i, 0))
    return pl.pallas_call(
        functools.partial(_dil_attn_body, sub_len=sub_len, penalties=penalties),
        grid=(d, sub_len // t),
        in_specs=[cur(0), halo(1, -1), cur(1), halo(1, qb), halo(2, -1), cur(2), halo(2, qb)],
        out_specs=[out_spec, out_spec],
        out_shape=[jax.ShapeDtypeStruct((d, sub_len, BRANCH_W), F32)] * 2,
        compiler_params=_params("parallel", "parallel"),
        name=f"dil_attn_g{group}",
    )(qkv, qkv, qkv, qkv, qkv, qkv, qkv)


def _dil_combine_body(o0, o1, o2, l0, l1, l2, z_ref, *scratch):
    tm = z_ref.shape[0]
    nat = {}
    for idx, (src, (_, d)) in enumerate(((o1, DIL_PATTERNS[1]), (o2, DIL_PATTERNS[2]),
                                         (l1, DIL_PATTERNS[1]), (l2, DIL_PATTERNS[2]))):
        scr = scratch[idx]
        for r in range(d):
            for a in range(DIL_HEADS_PER_GROUP):
                sl = slice(a * DIL_HEAD_DIM, (a + 1) * DIL_HEAD_DIM)
                scr[a, pl.ds(r, tm // d, stride=d), :] = src[r, :, sl]
        nat[idx] = scr
    for a in range(DIL_HEADS_PER_GROUP):
        sl = slice(a * DIL_HEAD_DIM, (a + 1) * DIL_HEAD_DIM)
        la, lb, lc = l0[0, :, sl], nat[2][a], nat[3][a]
        m = jnp.maximum(jnp.maximum(la, lb), lc)
        ea, eb, ec = jnp.exp(la - m), jnp.exp(lb - m), jnp.exp(lc - m)
        tot = ea + eb + ec
        z = (ea / tot) * o0[0, :, sl] + (eb / tot) * nat[0][a] + (ec / tot) * nat[1][a]
        z_ref[:, sl] = z.astype(BF16)


def dil_combine(outs, lses, *, tm=512):
    s = outs[0].shape[1]
    specs = [pl.BlockSpec((d, tm // d, BRANCH_W), lambda i: (0, i, 0)) for _, d in DIL_PATTERNS]
    return pl.pallas_call(
        _dil_combine_body,
        grid=(s // tm,),
        in_specs=specs + specs,
        out_specs=pl.BlockSpec((tm, BRANCH_W), lambda i: (i, 0)),
        out_shape=jax.ShapeDtypeStruct((s, BRANCH_W), BF16),
        scratch_shapes=[pltpu.VMEM((DIL_HEADS_PER_GROUP, tm, DIL_HEAD_DIM), F32)] * 4,
        compiler_params=_params("parallel"),
        name="dil_combine",
    )(*outs, *lses)


def _sgu_body(u_ref, v_ref, g_ref, ws_ref, b_ref, o_ref):
    tm = u_ref.shape[0]
    u = jax.nn.gelu(u_ref[...])
    vn = _rms(jax.nn.gelu(v_ref[...]), g_ref[...]).astype(BF16)
    for n in range(tm // SGU_CHUNK):
        rows = slice(n * SGU_CHUNK, (n + 1) * SGU_CHUNK)
        for g in range(SGU_GROUPS):
            cols = slice(g * LANE, (g + 1) * LANE)
            mixed = jnp.dot(ws_ref[g], vn[rows, cols], preferred_element_type=F32) + b_ref[g]
            o_ref[rows, cols] = (u[rows, cols] * mixed).astype(BF16)


def sgu_mixer(p_rest, v_norm_g, ws, b_full, layer, *, tm=512):
    s = p_rest.shape[0]
    return pl.pallas_call(
        _sgu_body,
        grid=(s // tm,),
        in_specs=[
            pl.BlockSpec((tm, BRANCH_W), lambda i: (i, REST_SGU // BRANCH_W)),
            pl.BlockSpec((tm, BRANCH_W), lambda i: (i, REST_SGU // BRANCH_W + 1)),
            pl.BlockSpec((1, BRANCH_W), lambda i: (0, 0)),
            pl.BlockSpec((None, SGU_GROUPS, SGU_CHUNK, SGU_CHUNK), lambda i: (layer, 0, 0, 0)),
            pl.BlockSpec((SGU_GROUPS, SGU_CHUNK, LANE), lambda i: (0, 0, 0)),
        ],
        out_specs=pl.BlockSpec((tm, BRANCH_W), lambda i: (i, 0)),
        out_shape=jax.ShapeDtypeStruct((s, BRANCH_W), BF16),
        compiler_params=_params("parallel"),
        name="sgu_mixer",
    )(p_rest, p_rest, v_norm_g.reshape(1, BRANCH_W), ws, b_full)


def _swap_rope_halves(y):
    width = y.shape[1]
    lane = lax.broadcasted_iota(jnp.int32, y.shape, 1)
    first_half = (lane % MLA_ROPE) < MLA_ROPE // 2
    return jnp.where(first_half, pltpu.roll(y, width - MLA_ROPE // 2, 1),
                     pltpu.roll(y, MLA_ROPE // 2, 1))


def _mla_prep_body(cq_ref, ckv_ref, kr_ref, cos_ref, sin_ref, qa_g, wuq_ref, kva_g, wukv_ref,
                   gqn_ref, gqr_ref, gkn_ref, gkr_ref, q_ref, k_ref, v_ref):
    nope_w = MLA_HEADS * MLA_NOPE
    qall = jnp.dot(_rms(cq_ref[...], qa_g[...]).astype(BF16), wuq_ref[...], preferred_element_type=F32)
    kv = jnp.dot(_rms(ckv_ref[...], kva_g[...]).astype(BF16), wukv_ref[...], preferred_element_type=F32)
    qn, qr = qall[:, :nope_w], qall[:, nope_w:]
    kn = kv[:, :nope_w]
    v_ref[...] = kv[:, nope_w:].astype(BF16)
    kr_blk = kr_ref[...]
    lane128 = lax.broadcasted_iota(jnp.int32, kr_blk.shape, 1)
    kr2 = jnp.where(lane128 < MLA_ROPE, kr_blk, pltpu.roll(kr_blk, MLA_ROPE, 1))
    kr = jnp.concatenate([kr2] * (MLA_HEADS // 2), axis=1)

    rope_w = MLA_HEADS * MLA_ROPE
    head_of_lane = lax.broadcasted_iota(jnp.int32, (1, rope_w), 1) // MLA_ROPE
    qr_sq = qr * qr
    kr_ssq = jnp.sum(jnp.where(head_of_lane == 0, kr * kr, 0.0), axis=-1, keepdims=True)
    rq, rk = [], []
    for h in range(MLA_HEADS):
        sl = slice(h * MLA_NOPE, (h + 1) * MLA_NOPE)
        q_ssq = (jnp.sum(qn[:, sl] * qn[:, sl], axis=-1, keepdims=True)
                 + jnp.sum(jnp.where(head_of_lane == h, qr_sq, 0.0), axis=-1, keepdims=True))
        k_ssq = jnp.sum(kn[:, sl] * kn[:, sl], axis=-1, keepdims=True) + kr_ssq
        rq.append(lax.rsqrt(q_ssq / MLA_QK + EPS))
        rk.append(lax.rsqrt(k_ssq / MLA_QK + EPS))

    def per_lane(rs):
        out = jnp.zeros((kr.shape[0], rope_w), F32)
        for h in range(MLA_HEADS):
            out = jnp.where(head_of_lane == h, rs[h], out)
        return out

    def rotate(x, rs, g_ref):
        y = x * per_lane(rs) * g_ref[...]
        return y * cos_ref[...] + _swap_rope_halves(y) * sin_ref[...]

    q_rope = rotate(qr, rq, gqr_ref) * MLA_EXP2_SCALE
    k_rope = rotate(kr, rk, gkr_ref)
    lane_tile = lax.broadcasted_iota(jnp.int32, (1, LANE), 1) // MLA_ROPE
    for h in range(MLA_HEADS):
        sl = slice(h * MLA_NOPE, (h + 1) * MLA_NOPE)
        base = h * MLA_PAD_QK
        tile = slice((h // 2) * LANE, (h // 2 + 1) * LANE)
        q_ref[:, base:base + MLA_NOPE] = (qn[:, sl] * rq[h] * gqn_ref[:, sl] * MLA_EXP2_SCALE).astype(BF16)
        k_ref[:, base:base + MLA_NOPE] = (kn[:, sl] * rk[h] * gkn_ref[:, sl]).astype(BF16)
        q_ref[:, base + MLA_NOPE:base + MLA_PAD_QK] = jnp.where(
            lane_tile == h % 2, q_rope[:, tile], 0.0).astype(BF16)
        k_ref[:, base + MLA_NOPE:base + MLA_PAD_QK] = k_rope[:, tile].astype(BF16)


def mla_prep(p_rest, cos4, sin4, qa_g, wuq, kva_g, wukv, gqn, gqr, gkn, gkr, *, tm=512):
    s = p_rest.shape[0]
    rope_w = MLA_HEADS * MLA_ROPE

    def const(shape):
        return pl.BlockSpec(shape, lambda i: (0,) * len(shape))

    return pl.pallas_call(
        _mla_prep_body,
        grid=(s // tm,),
        in_specs=[
            pl.BlockSpec((tm, MLA_Q_LORA), lambda i: (i, REST_CQ // MLA_Q_LORA)),
            pl.BlockSpec((tm, MLA_KV_LORA), lambda i: (i, REST_CKV // MLA_KV_LORA)),
            pl.BlockSpec((tm, LANE), lambda i: (i, REST_KR // LANE)),
            pl.BlockSpec((tm, rope_w), lambda i: (i, 0)),
            pl.BlockSpec((tm, rope_w), lambda i: (i, 0)),
            const((1, MLA_Q_LORA)), const(wuq.shape), const((1, MLA_KV_LORA)), const(wukv.shape),
            const(gqn.shape), const(gqr.shape), const(gkn.shape), const(gkr.shape),
        ],
        out_specs=[
            pl.BlockSpec((tm, MLA_HEADS * MLA_PAD_QK), lambda i: (i, 0)),
            pl.BlockSpec((tm, MLA_HEADS * MLA_PAD_QK), lambda i: (i, 0)),
            pl.BlockSpec((tm, MLA_HEADS * MLA_V), lambda i: (i, 0)),
        ],
        out_shape=[
            jax.ShapeDtypeStruct((s, MLA_HEADS * MLA_PAD_QK), BF16),
            jax.ShapeDtypeStruct((s, MLA_HEADS * MLA_PAD_QK), BF16),
            jax.ShapeDtypeStruct((s, MLA_HEADS * MLA_V), BF16),
        ],
        compiler_params=_params("parallel"),
        name="mla_prep",
    )(p_rest, p_rest, p_rest, cos4, sin4, qa_g.reshape(1, -1), wuq, kva_g.reshape(1, -1), wukv,
      gqn, gqr, gkn, gkr)


def _mla_attn_body(q_ref, k_ref, v_ref, o_ref, m_ref, l_ref, acc_ref):
    j = pl.program_id(2)

    @pl.when(j == 0)
    def _():
        m_ref[...] = jnp.full(m_ref.shape, NEG_BIG, F32)
        l_ref[...] = jnp.zeros(l_ref.shape, F32)
        acc_ref[...] = jnp.zeros(acc_ref.shape, F32)

    tk = k_ref.shape[0]
    for h in range(MLA_HEADS_PER_STEP):
        qk = slice(h * MLA_PAD_QK, (h + 1) * MLA_PAD_QK)
        vs = slice(h * MLA_V, (h + 1) * MLA_V)
        s = lax.dot_general(q_ref[:, qk], k_ref[:, qk], (((1,), (1,)), ((), ())),
                            preferred_element_type=F32)
        m_prev = m_ref[h]
        m_new = jnp.maximum(m_prev, jnp.max(s, axis=-1, keepdims=True))
        alpha = jnp.exp2(m_prev - m_new)
        p = jnp.exp2(s - jnp.tile(m_new, (1, tk // LANE)))
        l_ref[h] = alpha * l_ref[h] + jnp.sum(p, axis=-1, keepdims=True)
        acc_ref[h] = alpha * acc_ref[h] + jnp.dot(p.astype(BF16), v_ref[:, vs], preferred_element_type=F32)
        m_ref[h] = m_new

    @pl.when(j == pl.num_programs(2) - 1)
    def _():
        for h in range(MLA_HEADS_PER_STEP):
            o_ref[:, h * MLA_V:(h + 1) * MLA_V] = (acc_ref[h] / l_ref[h]).astype(BF16)


def mla_attn(q, k, v, *, tq=1024, tk=2048):
    s = q.shape[0]
    hp = MLA_HEADS_PER_STEP
    stat = pltpu.VMEM((hp, tq, LANE), F32)
    return pl.pallas_call(
        _mla_attn_body,
        grid=(s // tq, MLA_HEADS // hp, s // tk),
        in_specs=[
            pl.BlockSpec((tq, hp * MLA_PAD_QK), lambda i, h, j: (i, h)),
            pl.BlockSpec((tk, hp * MLA_PAD_QK), lambda i, h, j: (j, h)),
            pl.BlockSpec((tk, hp * MLA_V), lambda i, h, j: (j, h)),
        ],
        out_specs=pl.BlockSpec((tq, hp * MLA_V), lambda i, h, j: (i, h)),
        out_shape=jax.ShapeDtypeStruct((s, MLA_HEADS * MLA_V), BF16),
        scratch_shapes=[stat, stat, stat],
        compiler_params=_params("parallel", "parallel", "arbitrary"),
        name="mla_attn",
    )(q, k, v)


def _merge_body(h_ref, *refs):
    z_refs, wg_refs, wb_refs = refs[:N_BRANCH], refs[N_BRANCH:2 * N_BRANCH], refs[2 * N_BRANCH:3 * N_BRANCH]
    o_ref = refs[3 * N_BRANCH]
    h = h_ref[...]
    acc = None
    for z_ref, wg_ref, wb_ref in zip(z_refs, wg_refs, wb_refs):
        gate = jax.nn.sigmoid(jnp.dot(h, wg_ref[...], preferred_element_type=F32))
        term = gate * jnp.dot(z_ref[...], wb_ref[...], preferred_element_type=F32)
        acc = term if acc is None else acc + term
    o_ref[...] = acc.astype(BF16)


def merge(h, zs, w_gate, w_branch, layer, *, tm=1024, tn=512):
    s, d = h.shape
    ct = d // tn
    z_spec = pl.BlockSpec((tm, BRANCH_W), lambda i, c: (i, 0))
    g_specs = [pl.BlockSpec((None, d, tn), lambda i, c, n=n: (layer, 0, n * ct + c)) for n in range(N_BRANCH)]
    b_specs = [pl.BlockSpec((None, None, BRANCH_W, tn), lambda i, c, n=n: (layer, n, 0, c))
               for n in range(N_BRANCH)]
    return pl.pallas_call(
        _merge_body,
        grid=(s // tm, ct),
        in_specs=[pl.BlockSpec((tm, d), lambda i, c: (i, 0))] + [z_spec] * N_BRANCH + g_specs + b_specs,
        out_specs=pl.BlockSpec((tm, tn), lambda i, c: (i, c)),
        out_shape=jax.ShapeDtypeStruct((s, d), BF16),
        compiler_params=_params("parallel", "arbitrary"),
        name="merge",
    )(h, *zs, *([w_gate] * N_BRANCH), *([w_branch] * N_BRANCH))


def _out_proj_body(x_ref, y_ref, w_ref, o_ref):
    o_ref[...] = x_ref[...] + jnp.dot(y_ref[...], w_ref[...], preferred_element_type=F32)


def out_proj(x, y, w, layer, *, tm=1024, tn=1024):
    s, d = x.shape
    return pl.pallas_call(
        _out_proj_body,
        grid=(s // tm, d // tn),
        in_specs=[
            pl.BlockSpec((tm, tn), lambda i, j: (i, j)),
            pl.BlockSpec((tm, d), lambda i, j: (i, 0)),
            pl.BlockSpec((None, d, tn), lambda i, j: (layer, 0, j)),
        ],
        out_specs=pl.BlockSpec((tm, tn), lambda i, j: (i, j)),
        out_shape=jax.ShapeDtypeStruct((s, d), F32),
        compiler_params=_params("parallel", "arbitrary"),
        name="out_proj",
    )(x, y, w)


def _rope_tables(s):
    pos = jnp.arange(s, dtype=F32)
    inv_freq = ROPE_THETA ** (-jnp.arange(0, MLA_ROPE, 2, dtype=F32) / MLA_ROPE)
    ang = pos[:, None] * inv_freq[None, :]
    cos, sin = jnp.cos(ang), jnp.sin(ang)
    cos4 = jnp.tile(jnp.concatenate([cos, cos], axis=-1), (1, MLA_HEADS))
    sin4 = jnp.tile(jnp.concatenate([-sin, sin], axis=-1), (1, MLA_HEADS))
    return cos4, sin4


def _split_heads(w, widths):
    per_head = sum(widths)
    w3 = w.reshape(w.shape[0], MLA_HEADS, per_head)
    parts, off = [], 0
    for wd in widths:
        parts.append(w3[:, :, off:off + wd].reshape(w.shape[0], MLA_HEADS * wd))
        off += wd
    return jnp.concatenate(parts, axis=1)


def kernel(x, ffn1_norm, ffn1_w_gate, ffn1_w_up, ffn1_w_down, mix_norm, w_in, pool_w, pool_scale,
           dil_q_norm, dil_k_norm, sgu_v_norm, sgu_w, sgu_b, mla_q_a_norm, mla_w_uq, mla_kv_a_norm,
           mla_w_ukv, mla_q_norm, mla_k_norm, w_branch, w_out, ffn2_norm, ffn2_w_gate, ffn2_w_up,
           ffn2_w_down):
    b, s, d = x.shape
    cos4, sin4 = _rope_tables(s)
    f1g, f1u, f1d = ffn1_w_gate.astype(BF16), ffn1_w_up.astype(BF16), ffn1_w_down.astype(BF16)
    f2g, f2u, f2d = ffn2_w_gate.astype(BF16), ffn2_w_up.astype(BF16), ffn2_w_down.astype(BF16)
    w_main = w_in.astype(BF16)
    w_gate = w_main[:, :, OFF_GATE:]
    w_br, w_o = w_branch.astype(BF16), w_out.astype(BF16)
    pool_wb, sgu_wb = pool_w.astype(BF16), sgu_w.astype(BF16)
    ones = jnp.ones((BRANCH_W,), F32)

    outs = []
    for bi in range(b):
        xb = x[bi]
        for l in range(DEPTH):
            xb = ffn(xb, ffn1_norm[l], f1g, f1u, f1d, l)

            p_rest, h = proj_rest(xb, mix_norm[l], w_main, l)
            dil_gains = jnp.stack([jnp.tile(dil_q_norm[l], DIL_HEADS_PER_GROUP),
                                   jnp.tile(dil_k_norm[l], DIL_HEADS_PER_GROUP), ones]).reshape(3, 1, BRANCH_W)
            qkvs = proj_dil(h, w_main, dil_gains, l)
            z_a = pool_mixer(p_rest, pool_wb, pool_scale[l], l)
            o_l = [dil_attn(qkvs[g], g) for g in range(DIL_GROUPS)]
            z_b = dil_combine([o for o, _ in o_l], [ls for _, ls in o_l])
            b_full = jnp.broadcast_to(sgu_b[l][:, :, None], (SGU_GROUPS, SGU_CHUNK, LANE))
            z_c = sgu_mixer(p_rest, sgu_v_norm[l], sgu_wb, b_full, l)
            q, k, v = mla_prep(
                p_rest, cos4, sin4, mla_q_a_norm[l],
                _split_heads(mla_w_uq[l], (MLA_NOPE, MLA_ROPE)).astype(BF16),
                mla_kv_a_norm[l], _split_heads(mla_w_ukv[l], (MLA_NOPE, MLA_V)).astype(BF16),
                jnp.tile(mla_q_norm[l][:MLA_NOPE], MLA_HEADS).reshape(1, -1),
                jnp.tile(mla_q_norm[l][MLA_NOPE:], MLA_HEADS).reshape(1, -1),
                jnp.tile(mla_k_norm[l][:MLA_NOPE], MLA_HEADS).reshape(1, -1),
                jnp.tile(mla_k_norm[l][MLA_NOPE:], MLA_HEADS).reshape(1, -1))
            z_d = mla_attn(q, k, v)
            merged = merge(h, (z_a, z_b, z_c, z_d), w_gate, w_br, l)
            xb = out_proj(xb, merged, w_o, l)

            xb = ffn(xb, ffn2_norm[l], f2g, f2u, f2d, l)
        outs.append(xb)
    return jnp.stack(outs, axis=0)
```

```python
import functools
import math

import numpy as np
import jax
import jax.numpy as jnp
from jax import lax
from jax.experimental import pallas as pl
from jax.experimental.pallas import tpu as pltpu

F32 = jnp.float32
BF16 = jnp.bfloat16

D_MODEL = 2048
DEPTH = 4
D_FF = 5632
N_BRANCH = 4
BRANCH_W = 512
POOL_WINDOWS = (2, 4, 8, 16)
POOL_HALO = 8
DIL_PATTERNS = ((128, 1), (512, 4), (2048, 16))
DIL_GROUPS = len(DIL_PATTERNS)
DIL_HEADS = 12
DIL_HEADS_PER_GROUP = 4
DIL_HEAD_DIM = 128
DIL_QKV = 3 * DIL_HEADS * DIL_HEAD_DIM
DIL_SIDE = 64
SGU_CHUNK = 128
SGU_GROUPS = 4
MLA_HEADS = 4
MLA_Q_LORA = 384
MLA_KV_LORA = 128
MLA_NOPE = 128
MLA_ROPE = 64
MLA_V = 128
MLA_QK = MLA_NOPE + MLA_ROPE
MLA_PAD_QK = 256
MLA_HEADS_PER_STEP = 4
ROPE_THETA = 10000.0
EPS = 1e-6
NEG_BIG = -1e30
MLA_EXP2_SCALE = (MLA_QK ** -0.5) * math.log2(math.e)

OFF_POOL = 0
OFF_DIL = OFF_POOL + BRANCH_W
OFF_SGU = OFF_DIL + DIL_QKV
OFF_MLA_Q = OFF_SGU + 2 * BRANCH_W
OFF_MLA_KV = OFF_MLA_Q + MLA_Q_LORA
OFF_MLA_KR = OFF_MLA_KV + MLA_KV_LORA
OFF_GATE = OFF_MLA_KR + MLA_ROPE

LANE = 128
REST_POOL = 0
REST_SGU = 512
REST_CQ = 1536
REST_CKV = 1920
REST_KR = 2048
REST_W = REST_KR + LANE

VMEM_LIMIT = 56 * 1024 * 1024


def _params(*sem):
    return pltpu.CompilerParams(dimension_semantics=sem, vmem_limit_bytes=VMEM_LIMIT)


def _rms(xf, g):
    return xf * lax.rsqrt(jnp.mean(xf * xf, axis=-1, keepdims=True) + EPS) * g


def _ffn_body(x_hbm, g_ref, wg_ref, wu_ref, wd_ref, o_ref, h_ref, sem):
    tm = o_ref.shape[0]

    @pl.when(pl.program_id(1) == 0)
    def _():
        cp = pltpu.make_async_copy(x_hbm.at[pl.ds(pl.program_id(0) * tm, tm), :], o_ref, sem)
        cp.start()
        cp.wait()
        h_ref[...] = _rms(o_ref[...], g_ref[...]).astype(BF16)

    h = h_ref[...]
    g = jnp.dot(h, wg_ref[...].astype(BF16), preferred_element_type=F32)
    u = jnp.dot(h, wu_ref[...].astype(BF16), preferred_element_type=F32)
    a = (0.5 * (jax.nn.silu(g) * u)).astype(BF16)
    o_ref[...] += jnp.dot(a, wd_ref[...].astype(BF16), preferred_element_type=F32)


def ffn(x, norm_g, wg, wu, wd, layer, *, tm=1024, tf=512):
    s, d = x.shape
    f = wg.shape[2]
    return pl.pallas_call(
        _ffn_body,
        grid=(s // tm, f // tf),
        in_specs=[
            pl.BlockSpec(memory_space=pl.ANY),
            pl.BlockSpec((1, d), lambda i, j: (0, 0)),
            pl.BlockSpec((None, d, tf), lambda i, j: (layer, 0, j)),
            pl.BlockSpec((None, d, tf), lambda i, j: (layer, 0, j)),
            pl.BlockSpec((None, tf, d), lambda i, j: (layer, j, 0)),
        ],
        out_specs=pl.BlockSpec((tm, d), lambda i, j: (i, 0)),
        out_shape=jax.ShapeDtypeStruct((s, d), F32),
        scratch_shapes=[pltpu.VMEM((tm, d), BF16), pltpu.SemaphoreType.DMA],
        compiler_params=_params("parallel", "arbitrary"),
        name="ffn",
    )(x, norm_g.reshape(1, d), wg, wu, wd)


def _proj_rest_body(x_ref, g_ref, wp_ref, ws_ref, wq_ref, wkv_ref, wkr_ref, p_ref, h_ref):
    h = _rms(x_ref[...], g_ref[...]).astype(BF16)
    h_ref[...] = h
    for w_ref, off in ((wp_ref, REST_POOL), (ws_ref, REST_SGU), (wq_ref, REST_CQ), (wkv_ref, REST_CKV),
                       (wkr_ref, REST_KR)):
        p_ref[:, off:off + w_ref.shape[1]] = jnp.dot(h, w_ref[...], preferred_element_type=F32)


def proj_rest(x, norm_g, w_main, layer, *, tm=512):
    s, d = x.shape

    def wspec(off, width):
        return pl.BlockSpec((None, d, width), lambda i: (layer, 0, off // width))

    return pl.pallas_call(
        _proj_rest_body,
        grid=(s // tm,),
        in_specs=[
            pl.BlockSpec((tm, d), lambda i: (i, 0)),
            pl.BlockSpec((1, d), lambda i: (0, 0)),
            wspec(OFF_POOL, BRANCH_W),
            wspec(OFF_SGU, 2 * BRANCH_W),
            wspec(OFF_MLA_Q, MLA_Q_LORA),
            wspec(OFF_MLA_KV, MLA_KV_LORA),
            wspec(OFF_MLA_KR, LANE),
        ],
        out_specs=[
            pl.BlockSpec((tm, REST_W), lambda i: (i, 0)),
            pl.BlockSpec((tm, d), lambda i: (i, 0)),
        ],
        out_shape=[jax.ShapeDtypeStruct((s, REST_W), F32), jax.ShapeDtypeStruct((s, d), BF16)],
        compiler_params=_params("parallel"),
        name="proj_rest",
    )(x, norm_g.reshape(1, d), w_main, w_main, w_main, w_main, w_main)


def _proj_dil_body(h_ref, w0_ref, w1_ref, w2_ref, g_ref, o0_ref, o1_ref, o2_ref, scr1_ref, scr2_ref):
    normed = pl.program_id(1) < 2
    h = h_ref[...]
    tm = h.shape[0]
    for w_ref, o_ref, scr_ref, (_, d) in reversed(list(zip((w0_ref, w1_ref, w2_ref), (o0_ref, o1_ref, o2_ref),
                                                           (None, scr1_ref, scr2_ref), DIL_PATTERNS))):
        acc = jnp.dot(h, w_ref[...], preferred_element_type=F32)
        for a in range(DIL_HEADS_PER_GROUP):
            sl = slice(a * DIL_HEAD_DIM, (a + 1) * DIL_HEAD_DIM)
            xh = acc[:, sl]
            r = lax.rsqrt(jnp.mean(xh * xh, axis=-1, keepdims=True) + EPS)
            y = xh * jnp.where(normed, r, 1.0) * g_ref[:, sl]
            if d == 1:
                o_ref[0, :, sl] = y.astype(BF16)
            else:
                scr_ref[a] = y
        if d > 1:
            for r in range(d):
                for a in range(DIL_HEADS_PER_GROUP):
                    sl = slice(a * DIL_HEAD_DIM, (a + 1) * DIL_HEAD_DIM)
                    o_ref[r, :, sl] = scr_ref[a, pl.ds(r, tm // d, stride=d), :].astype(BF16)


def proj_dil(h, w_main, gains, layer, *, tm=1024):
    s, dm = h.shape
    first = OFF_DIL // BRANCH_W

    def wspec(g):
        return pl.BlockSpec((None, dm, BRANCH_W), lambda i, j: (layer, 0, first + j * DIL_GROUPS + g))

    out_specs, out_shapes, scratch = [], [], []
    for _, d in DIL_PATTERNS:
        out_specs.append(pl.BlockSpec((d, tm // d, BRANCH_W), lambda i, j: (0, i, j)))
        out_shapes.append(jax.ShapeDtypeStruct((d, s // d, 3 * BRANCH_W), BF16))
        if d > 1:
            scratch.append(pltpu.VMEM((DIL_HEADS_PER_GROUP, tm, DIL_HEAD_DIM), F32))
    return pl.pallas_call(
        _proj_dil_body,
        grid=(s // tm, 3),
        in_specs=[
            pl.BlockSpec((tm, dm), lambda i, j: (i, 0)),
            wspec(0), wspec(1), wspec(2),
            pl.BlockSpec((None, 1, BRANCH_W), lambda i, j: (j, 0, 0)),
        ],
        out_specs=out_specs,
        out_shape=out_shapes,
        scratch_shapes=scratch,
        compiler_params=_params("parallel", "arbitrary"),
        name="proj_dil",
    )(h, w_main, w_main, w_main, gains)


def _pool_body(prev_ref, cur_ref, next_ref, w_ref, scale_ref, o_ref, ext_ref, *, seq):
    i = pl.program_id(0)
    tm = cur_ref.shape[0]
    cur = cur_ref[...]
    ext_ref[0:POOL_HALO, :] = jnp.where(i > 0, prev_ref[...], 0.0)
    ext_ref[POOL_HALO:POOL_HALO + tm, :] = cur
    ext_ref[POOL_HALO + tm:2 * POOL_HALO + tm, :] = jnp.where(
        i < pl.num_programs(0) - 1, next_ref[...], 0.0)
    t = i * tm + lax.broadcasted_iota(jnp.int32, (tm, 1), 0)
    for g, w in enumerate(POOL_WINDOWS):
        sl = slice(g * LANE, (g + 1) * LANE)
        tot = jnp.zeros((tm, LANE), F32)
        for o in range(-(w // 2), w // 2):
            tot = tot + ext_ref[POOL_HALO + o:POOL_HALO + o + tm, sl]
        cnt = (jnp.minimum(t + w // 2, seq) - jnp.maximum(t - w // 2, 0)).astype(F32)
        dev = (tot / cnt - cur[:, sl]).astype(BF16)
        y = jnp.dot(dev, w_ref[g], preferred_element_type=F32) * scale_ref[:, sl]
        o_ref[:, sl] = y.astype(BF16)


def pool_mixer(p_rest, pool_w, pool_scale, layer, *, tm=512):
    s = p_rest.shape[0]
    hb = tm // POOL_HALO
    last = s // POOL_HALO - 1
    n_win = len(POOL_WINDOWS)
    return pl.pallas_call(
        functools.partial(_pool_body, seq=s),
        grid=(s // tm,),
        in_specs=[
            pl.BlockSpec((POOL_HALO, BRANCH_W), lambda i: (jnp.maximum(i * hb - 1, 0), 0)),
            pl.BlockSpec((tm, BRANCH_W), lambda i: (i, 0)),
            pl.BlockSpec((POOL_HALO, BRANCH_W), lambda i: (jnp.minimum((i + 1) * hb, last), 0)),
            pl.BlockSpec((None, n_win, LANE, LANE), lambda i: (layer, 0, 0, 0)),
            pl.BlockSpec((1, BRANCH_W), lambda i: (0, 0)),
        ],
        out_specs=pl.BlockSpec((tm, BRANCH_W), lambda i: (i, 0)),
        out_shape=jax.ShapeDtypeStruct((s, BRANCH_W), BF16),
        scratch_shapes=[pltpu.VMEM((tm + 2 * POOL_HALO, BRANCH_W), F32)],
        compiler_params=_params("parallel"),
        name="pool_mixer",
    )(p_rest, p_rest, p_rest, pool_w, pool_scale.reshape(1, BRANCH_W))


def _alibi_slopes():
    n = DIL_HEADS
    return np.exp2(np.float32(-8.0) * np.arange(1, n + 1, dtype=np.float32) / np.float32(n))


def _dil_attn_body(q_ref, kp_ref, kc_ref, kn_ref, vp_ref, vc_ref, vn_ref, o_ref, lse_ref,
                   *, sub_len, penalties):
    i = pl.program_id(1)
    t = q_ref.shape[0]
    blk = kp_ref.shape[0]
    nk = blk + 2 * DIL_SIDE
    row = lax.broadcasted_iota(jnp.int32, (blk, nk), 0)
    col = lax.broadcasted_iota(jnp.int32, (blk, nk), 1)
    dist = jnp.abs(col - DIL_SIDE - row)
    in_band = dist <= DIL_SIDE
    distf = dist.astype(F32)
    for h in range(DIL_HEADS_PER_GROUP):
        sl = slice(h * DIL_HEAD_DIM, (h + 1) * DIL_HEAD_DIM)
        k_ext = jnp.concatenate([kp_ref[blk - DIL_SIDE:, sl], kc_ref[:, sl], kn_ref[:DIL_SIDE, sl]], axis=0)
        v_ext = jnp.concatenate([vp_ref[blk - DIL_SIDE:, sl], vc_ref[:, sl], vn_ref[:DIL_SIDE, sl]], axis=0)
        bias = jnp.where(in_band, -penalties[h] * distf, NEG_BIG)
        for b in range(t // blk):
            rows = slice(b * blk, (b + 1) * blk)
            kpos = i * t + b * blk + col - DIL_SIDE
            s = lax.dot_general(q_ref[rows, sl], k_ext[b * blk:b * blk + nk], (((1,), (1,)), ((), ())),
                                preferred_element_type=F32) * (DIL_HEAD_DIM ** -0.5)
            s = jnp.where((kpos >= 0) & (kpos < sub_len), s + bias, NEG_BIG)
            m = jnp.max(s, axis=-1, keepdims=True)
            p = jnp.exp(s - m)
            den = jnp.sum(p, axis=-1, keepdims=True)
            o_ref[rows, sl] = jnp.dot((p / den).astype(BF16), v_ext[b * blk:b * blk + nk],
                                      preferred_element_type=F32)
            lse_ref[rows, sl] = jnp.broadcast_to(m + jnp.log(den), (blk, DIL_HEAD_DIM))


def dil_attn(qkv, group, *, blk=128, max_blocks=8):
    d, sub_len, _ = qkv.shape
    n_blk = sub_len // blk
    qb = min(max_blocks, n_blk)
    t = blk * qb
    slopes = _alibi_slopes()[group * DIL_HEADS_PER_GROUP:(group + 1) * DIL_HEADS_PER_GROUP]
    penalties = tuple(float(np.float32(x) * np.float32(d)) for x in slopes)

    def halo(which, step):
        return pl.BlockSpec((None, blk, BRANCH_W),
                            lambda r, i: (r, jnp.clip(i * qb + step, 0, n_blk - 1), which))

    def cur(which):
        return pl.BlockSpec((None, t, BRANCH_W), lambda r, i: (r, i, which))

    out_spec = pl.BlockSpec((None, t, BRANCH_W), lambda r, i: (r, i, 0))
    return pl.pallas_call(
        functools.partial(_dil_attn_body, sub_len=sub_len, penalties=penalties),
        grid=(d, sub_len // t),
        in_specs=[cur(0), halo(1, -1), cur(1), halo(1, qb), halo(2, -1), cur(2), halo(2, qb)],
        out_specs=[out_spec, out_spec],
        out_shape=[jax.ShapeDtypeStruct((d, sub_len, BRANCH_W), F32)] * 2,
        compiler_params=_params("parallel", "parallel"),
        name=f"dil_attn_g{group}",
    )(qkv, qkv, qkv, qkv, qkv, qkv, qkv)


def _dil_combine_body(o0, o1, o2, l0, l1, l2, z_ref, *scratch):
    tm = z_ref.shape[0]
    nat = {}
    for idx, (src, (_, d)) in enumerate(((o1, DIL_PATTERNS[1]), (o2, DIL_PATTERNS[2]),
                                         (l1, DIL_PATTERNS[1]), (l2, DIL_PATTERNS[2]))):
        scr = scratch[idx]
        for r in range(d):
            for a in range(DIL_HEADS_PER_GROUP):
                sl = slice(a * DIL_HEAD_DIM, (a + 1) * DIL_HEAD_DIM)
                scr[a, pl.ds(r, tm // d, stride=d), :] = src[r, :, sl]
        nat[idx] = scr
    for a in range(DIL_HEADS_PER_GROUP):
        sl = slice(a * DIL_HEAD_DIM, (a + 1) * DIL_HEAD_DIM)
        la, lb, lc = l0[0, :, sl], nat[2][a], nat[3][a]
        m = jnp.maximum(jnp.maximum(la, lb), lc)
        ea, eb, ec = jnp.exp(la - m), jnp.exp(lb - m), jnp.exp(lc - m)
        tot = ea + eb + ec
        z = (ea / tot) * o0[0, :, sl] + (eb / tot) * nat[0][a] + (ec / tot) * nat[1][a]
        z_ref[:, sl] = z.astype(BF16)


def dil_combine(outs, lses, *, tm=512):
    s = outs[0].shape[1]
    specs = [pl.BlockSpec((d, tm // d, BRANCH_W), lambda i: (0, i, 0)) for _, d in DIL_PATTERNS]
    return pl.pallas_call(
        _dil_combine_body,
        grid=(s // tm,),
        in_specs=specs + specs,
        out_specs=pl.BlockSpec((tm, BRANCH_W), lambda i: (i, 0)),
        out_shape=jax.ShapeDtypeStruct((s, BRANCH_W), BF16),
        scratch_shapes=[pltpu.VMEM((DIL_HEADS_PER_GROUP, tm, DIL_HEAD_DIM), F32)] * 4,
        compiler_params=_params("parallel"),
        name="dil_combine",
    )(*outs, *lses)


def _sgu_body(u_ref, v_ref, g_ref, ws_ref, b_ref, o_ref):
    tm = u_ref.shape[0]
    u = jax.nn.gelu(u_ref[...])
    vn = _rms(jax.nn.gelu(v_ref[...]), g_ref[...]).astype(BF16)
    for n in range(tm // SGU_CHUNK):
        rows = slice(n * SGU_CHUNK, (n + 1) * SGU_CHUNK)
        for g in range(SGU_GROUPS):
            cols = slice(g * LANE, (g + 1) * LANE)
            mixed = jnp.dot(ws_ref[g], vn[rows, cols], preferred_element_type=F32) + b_ref[g]
            o_ref[rows, cols] = (u[rows, cols] * mixed).astype(BF16)


def sgu_mixer(p_rest, v_norm_g, ws, b_full, layer, *, tm=512):
    s = p_rest.shape[0]
    return pl.pallas_call(
        _sgu_body,
        grid=(s // tm,),
        in_specs=[
            pl.BlockSpec((tm, BRANCH_W), lambda i: (i, REST_SGU // BRANCH_W)),
            pl.BlockSpec((tm, BRANCH_W), lambda i: (i, REST_SGU // BRANCH_W + 1)),
            pl.BlockSpec((1, BRANCH_W), lambda i: (0, 0)),
            pl.BlockSpec((None, SGU_GROUPS, SGU_CHUNK, SGU_CHUNK), lambda i: (layer, 0, 0, 0)),
            pl.BlockSpec((SGU_GROUPS, SGU_CHUNK, LANE), lambda i: (0, 0, 0)),
        ],
        out_specs=pl.BlockSpec((tm, BRANCH_W), lambda i: (i, 0)),
        out_shape=jax.ShapeDtypeStruct((s, BRANCH_W), BF16),
        compiler_params=_params("parallel"),
        name="sgu_mixer",
    )(p_rest, p_rest, v_norm_g.reshape(1, BRANCH_W), ws, b_full)


def _swap_rope_halves(y):
    width = y.shape[1]
    lane = lax.broadcasted_iota(jnp.int32, y.shape, 1)
    first_half = (lane % MLA_ROPE) < MLA_ROPE // 2
    return jnp.where(first_half, pltpu.roll(y, width - MLA_ROPE // 2, 1),
                     pltpu.roll(y, MLA_ROPE // 2, 1))


def _mla_prep_body(cq_ref, ckv_ref, kr_ref, cos_ref, sin_ref, qa_g, wuq_ref, kva_g, wukv_ref,
                   gqn_ref, gqr_ref, gkn_ref, gkr_ref, q_ref, k_ref, v_ref):
    nope_w = MLA_HEADS * MLA_NOPE
    qall = jnp.dot(_rms(cq_ref[...], qa_g[...]).astype(BF16), wuq_ref[...], preferred_element_type=F32)
    kv = jnp.dot(_rms(ckv_ref[...], kva_g[...]).astype(BF16), wukv_ref[...], preferred_element_type=F32)
    qn, qr = qall[:, :nope_w], qall[:, nope_w:]
    kn = kv[:, :nope_w]
    v_ref[...] = kv[:, nope_w:].astype(BF16)
    kr_blk = kr_ref[...]
    lane128 = lax.broadcasted_iota(jnp.int32, kr_blk.shape, 1)
    kr2 = jnp.where(lane128 < MLA_ROPE, kr_blk, pltpu.roll(kr_blk, MLA_ROPE, 1))
    kr = jnp.concatenate([kr2] * (MLA_HEADS // 2), axis=1)

    rope_w = MLA_HEADS * MLA_ROPE
    head_of_lane = lax.broadcasted_iota(jnp.int32, (1, rope_w), 1) // MLA_ROPE
    qr_sq = qr * qr
    kr_ssq = jnp.sum(jnp.where(head_of_lane == 0, kr * kr, 0.0), axis=-1, keepdims=True)
    rq, rk = [], []
    for h in range(MLA_HEADS):
        sl = slice(h * MLA_NOPE, (h + 1) * MLA_NOPE)
        q_ssq = (jnp.sum(qn[:, sl] * qn[:, sl], axis=-1, keepdims=True)
                 + jnp.sum(jnp.where(head_of_lane == h, qr_sq, 0.0), axis=-1, keepdims=True))
        k_ssq = jnp.sum(kn[:, sl] * kn[:, sl], axis=-1, keepdims=True) + kr_ssq
        rq.append(lax.rsqrt(q_ssq / MLA_QK + EPS))
        rk.append(lax.rsqrt(k_ssq / MLA_QK + EPS))

    def per_lane(rs):
        out = jnp.zeros((kr.shape[0], rope_w), F32)
        for h in range(MLA_HEADS):
            out = jnp.where(head_of_lane == h, rs[h], out)
        return out

    def rotate(x, rs, g_ref):
        y = x * per_lane(rs) * g_ref[...]
        return y * cos_ref[...] + _swap_rope_halves(y) * sin_ref[...]

    q_rope = rotate(qr, rq, gqr_ref) * MLA_EXP2_SCALE
    k_rope = rotate(kr, rk, gkr_ref)
    lane_tile = lax.broadcasted_iota(jnp.int32, (1, LANE), 1) // MLA_ROPE
    for h in range(MLA_HEADS):
        sl = slice(h * MLA_NOPE, (h + 1) * MLA_NOPE)
        base = h * MLA_PAD_QK
        tile = slice((h // 2) * LANE, (h // 2 + 1) * LANE)
        q_ref[:, base:base + MLA_NOPE] = (qn[:, sl] * rq[h] * gqn_ref[:, sl] * MLA_EXP2_SCALE).astype(BF16)
        k_ref[:, base:base + MLA_NOPE] = (kn[:, sl] * rk[h] * gkn_ref[:, sl]).astype(BF16)
        q_ref[:, base + MLA_NOPE:base + MLA_PAD_QK] = jnp.where(
            lane_tile == h % 2, q_rope[:, tile], 0.0).astype(BF16)
        k_ref[:, base + MLA_NOPE:base + MLA_PAD_QK] = k_rope[:, tile].astype(BF16)


def mla_prep(p_rest, cos4, sin4, qa_g, wuq, kva_g, wukv, gqn, gqr, gkn, gkr, *, tm=512):
    s = p_rest.shape[0]
    rope_w = MLA_HEADS * MLA_ROPE

    def const(shape):
        return pl.BlockSpec(shape, lambda i: (0,) * len(shape))

    return pl.pallas_call(
        _mla_prep_body,
        grid=(s // tm,),
        in_specs=[
            pl.BlockSpec((tm, MLA_Q_LORA), lambda i: (i, REST_CQ // MLA_Q_LORA)),
            pl.BlockSpec((tm, MLA_KV_LORA), lambda i: (i, REST_CKV // MLA_KV_LORA)),
            pl.BlockSpec((tm, LANE), lambda i: (i, REST_KR // LANE)),
            pl.BlockSpec((tm, rope_w), lambda i: (i, 0)),
            pl.BlockSpec((tm, rope_w), lambda i: (i, 0)),
            const((1, MLA_Q_LORA)), const(wuq.shape), const((1, MLA_KV_LORA)), const(wukv.shape),
            const(gqn.shape), const(gqr.shape), const(gkn.shape), const(gkr.shape),
        ],
        out_specs=[
            pl.BlockSpec((tm, MLA_HEADS * MLA_PAD_QK), lambda i: (i, 0)),
            pl.BlockSpec((tm, MLA_HEADS * MLA_PAD_QK), lambda i: (i, 0)),
            pl.BlockSpec((tm, MLA_HEADS * MLA_V), lambda i: (i, 0)),
        ],
        out_shape=[
            jax.ShapeDtypeStruct((s, MLA_HEADS * MLA_PAD_QK), BF16),
            jax.ShapeDtypeStruct((s, MLA_HEADS * MLA_PAD_QK), BF16),
            jax.ShapeDtypeStruct((s, MLA_HEADS * MLA_V), BF16),
        ],
        compiler_params=_params("parallel"),
        name="mla_prep",
    )(p_rest, p_rest, p_rest, cos4, sin4, qa_g.reshape(1, -1), wuq, kva_g.reshape(1, -1), wukv,
      gqn, gqr, gkn, gkr)


def _mla_attn_body(q_ref, k_ref, v_ref, o_ref, m_ref, l_ref, acc_ref):
    j = pl.program_id(2)

    @pl.when(j == 0)
    def _():
        m_ref[...] = jnp.full(m_ref.shape, NEG_BIG, F32)
        l_ref[...] = jnp.zeros(l_ref.shape, F32)
        acc_ref[...] = jnp.zeros(acc_ref.shape, F32)

    tk = k_ref.shape[0]
    for h in range(MLA_HEADS_PER_STEP):
        qk = slice(h * MLA_PAD_QK, (h + 1) * MLA_PAD_QK)
        vs = slice(h * MLA_V, (h + 1) * MLA_V)
        s = lax.dot_general(q_ref[:, qk], k_ref[:, qk], (((1,), (1,)), ((), ())),
                            preferred_element_type=F32)
        m_prev = m_ref[h]
        m_new = jnp.maximum(m_prev, jnp.max(s, axis=-1, keepdims=True))
        alpha = jnp.exp2(m_prev - m_new)
        p = jnp.exp2(s - jnp.tile(m_new, (1, tk // LANE)))
        l_ref[h] = alpha * l_ref[h] + jnp.sum(p, axis=-1, keepdims=True)
        acc_ref[h] = alpha * acc_ref[h] + jnp.dot(p.astype(BF16), v_ref[:, vs], preferred_element_type=F32)
        m_ref[h] = m_new

    @pl.when(j == pl.num_programs(2) - 1)
    def _():
        for h in range(MLA_HEADS_PER_STEP):
            o_ref[:, h * MLA_V:(h + 1) * MLA_V] = (acc_ref[h] / l_ref[h]).astype(BF16)


def mla_attn(q, k, v, *, tq=1024, tk=2048):
    s = q.shape[0]
    hp = MLA_HEADS_PER_STEP
    stat = pltpu.VMEM((hp, tq, LANE), F32)
    return pl.pallas_call(
        _mla_attn_body,
        grid=(s // tq, MLA_HEADS // hp, s // tk),
        in_specs=[
            pl.BlockSpec((tq, hp * MLA_PAD_QK), lambda i, h, j: (i, h)),
            pl.BlockSpec((tk, hp * MLA_PAD_QK), lambda i, h, j: (j, h)),
            pl.BlockSpec((tk, hp * MLA_V), lambda i, h, j: (j, h)),
        ],
        out_specs=pl.BlockSpec((tq, hp * MLA_V), lambda i, h, j: (i, h)),
        out_shape=jax.ShapeDtypeStruct((s, MLA_HEADS * MLA_V), BF16),
        scratch_shapes=[stat, stat, stat],
        compiler_params=_params("parallel", "parallel", "arbitrary"),
        name="mla_attn",
    )(q, k, v)


def _merge_body(h_ref, *refs):
    z_refs, wg_refs, wb_refs = refs[:N_BRANCH], refs[N_BRANCH:2 * N_BRANCH], refs[2 * N_BRANCH:3 * N_BRANCH]
    o_ref = refs[3 * N_BRANCH]
    h = h_ref[...]
    acc = None
    for z_ref, wg_ref, wb_ref in zip(z_refs, wg_refs, wb_refs):
        gate = jax.nn.sigmoid(jnp.dot(h, wg_ref[...], preferred_element_type=F32))
        term = gate * jnp.dot(z_ref[...], wb_ref[...], preferred_element_type=F32)
        acc = term if acc is None else acc + term
    o_ref[...] = acc.astype(BF16)


def merge(h, zs, w_gate, w_branch, layer, *, tm=1024, tn=512):
    s, d = h.shape
    ct = d // tn
    z_spec = pl.BlockSpec((tm, BRANCH_W), lambda i, c: (i, 0))
    g_specs = [pl.BlockSpec((None, d, tn), lambda i, c, n=n: (layer, 0, n * ct + c)) for n in range(N_BRANCH)]
    b_specs = [pl.BlockSpec((None, None, BRANCH_W, tn), lambda i, c, n=n: (layer, n, 0, c))
               for n in range(N_BRANCH)]
    return pl.pallas_call(
        _merge_body,
        grid=(s // tm, ct),
        in_specs=[pl.BlockSpec((tm, d), lambda i, c: (i, 0))] + [z_spec] * N_BRANCH + g_specs + b_specs,
        out_specs=pl.BlockSpec((tm, tn), lambda i, c: (i, c)),
        out_shape=jax.ShapeDtypeStruct((s, d), BF16),
        compiler_params=_params("parallel", "arbitrary"),
        name="merge",
    )(h, *zs, *([w_gate] * N_BRANCH), *([w_branch] * N_BRANCH))


def _out_proj_body(x_ref, y_ref, w_ref, o_ref):
    o_ref[...] = x_ref[...] + jnp.dot(y_ref[...], w_ref[...], preferred_element_type=F32)


def out_proj(x, y, w, layer, *, tm=1024, tn=1024):
    s, d = x.shape
    return pl.pallas_call(
        _out_proj_body,
        grid=(s // tm, d // tn),
        in_specs=[
            pl.BlockSpec((tm, tn), lambda i, j: (i, j)),
            pl.BlockSpec((tm, d), lambda i, j: (i, 0)),
            pl.BlockSpec((None, d, tn), lambda i, j: (layer, 0, j)),
        ],
        out_specs=pl.BlockSpec((tm, tn), lambda i, j: (i, j)),
        out_shape=jax.ShapeDtypeStruct((s, d), F32),
        compiler_params=_params("parallel", "arbitrary"),
        name="out_proj",
    )(x, y, w)


def _rope_tables(s):
    pos = jnp.arange(s, dtype=F32)
    inv_freq = ROPE_THETA ** (-jnp.arange(0, MLA_ROPE, 2, dtype=F32) / MLA_ROPE)
    ang = pos[:, None] * inv_freq[None, :]
    cos, sin = jnp.cos(ang), jnp.sin(ang)
    cos4 = jnp.tile(jnp.concatenate([cos, cos], axis=-1), (1, MLA_HEADS))
    sin4 = jnp.tile(jnp.concatenate([-sin, sin], axis=-1), (1, MLA_HEADS))
    return cos4, sin4


def _split_heads(w, widths):
    per_head = sum(widths)
    w3 = w.reshape(w.shape[0], MLA_HEADS, per_head)
    parts, off = [], 0
    for wd in widths:
        parts.append(w3[:, :, off:off + wd].reshape(w.shape[0], MLA_HEADS * wd))
        off += wd
    return jnp.concatenate(parts, axis=1)


def kernel(x, ffn1_norm, ffn1_w_gate, ffn1_w_up, ffn1_w_down, mix_norm, w_in, pool_w, pool_scale,
           dil_q_norm, dil_k_norm, sgu_v_norm, sgu_w, sgu_b, mla_q_a_norm, mla_w_uq, mla_kv_a_norm,
           mla_w_ukv, mla_q_norm, mla_k_norm, w_branch, w_out, ffn2_norm, ffn2_w_gate, ffn2_w_up,
           ffn2_w_down):
    b, s, d = x.shape
    cos4, sin4 = _rope_tables(s)
    w_main = w_in.astype(BF16)
    w_gate = w_main[:, :, OFF_GATE:]
    w_br, w_o = w_branch.astype(BF16), w_out.astype(BF16)
    pool_wb, sgu_wb = pool_w.astype(BF16), sgu_w.astype(BF16)
    ones = jnp.ones((BRANCH_W,), F32)

    outs = []
    for bi in range(b):
        xb = x[bi]
        for l in range(DEPTH):
            xb = ffn(xb, ffn1_norm[l], ffn1_w_gate, ffn1_w_up, ffn1_w_down, l)

            p_rest, h = proj_rest(xb, mix_norm[l], w_main, l)
            dil_gains = jnp.stack([jnp.tile(dil_q_norm[l], DIL_HEADS_PER_GROUP),
                                   jnp.tile(dil_k_norm[l], DIL_HEADS_PER_GROUP), ones]).reshape(3, 1, BRANCH_W)
            qkvs = proj_dil(h, w_main, dil_gains, l)
            z_a = pool_mixer(p_rest, pool_wb, pool_scale[l], l)
            o_l = [dil_attn(qkvs[g], g) for g in range(DIL_GROUPS)]
            z_b = dil_combine([o for o, _ in o_l], [ls for _, ls in o_l])
            b_full = jnp.broadcast_to(sgu_b[l][:, :, None], (SGU_GROUPS, SGU_CHUNK, LANE))
            z_c = sgu_mixer(p_rest, sgu_v_norm[l], sgu_wb, b_full, l)
            q, k, v = mla_prep(
                p_rest, cos4, sin4, mla_q_a_norm[l],
                _split_heads(mla_w_uq[l], (MLA_NOPE, MLA_ROPE)).astype(BF16),
                mla_kv_a_norm[l], _split_heads(mla_w_ukv[l], (MLA_NOPE, MLA_V)).astype(BF16),
                jnp.tile(mla_q_norm[l][:MLA_NOPE], MLA_HEADS).reshape(1, -1),
                jnp.tile(mla_q_norm[l][MLA_NOPE:], MLA_HEADS).reshape(1, -1),
                jnp.tile(mla_k_norm[l][:MLA_NOPE], MLA_HEADS).reshape(1, -1),
                jnp.tile(mla_k_norm[l][MLA_NOPE:], MLA_HEADS).reshape(1, -1))
            z_d = mla_attn(q, k, v)
            merged = merge(h, (z_a, z_b, z_c, z_d), w_gate, w_br, l)
            xb = out_proj(xb, merged, w_o, l)

            xb = ffn(xb, ffn2_norm[l], ffn2_w_gate, ffn2_w_up, ffn2_w_down, l)
        outs.append(xb)
    return jnp.stack(outs, axis=0)
```

```python
import functools
import math

import numpy as np
import jax
import jax.numpy as jnp
from jax import lax
from jax.experimental import pallas as pl
from jax.experimental.pallas import tpu as pltpu

F32 = jnp.float32
BF16 = jnp.bfloat16

D_MODEL = 2048
DEPTH = 4
D_FF = 5632
N_BRANCH = 4
BRANCH_W = 512
POOL_WINDOWS = (2, 4, 8, 16)
POOL_HALO = 8
DIL_PATTERNS = ((128, 1), (512, 4), (2048, 16))
DIL_GROUPS = len(DIL_PATTERNS)
DIL_HEADS = 12
DIL_HEADS_PER_GROUP = 4
DIL_HEAD_DIM = 128
DIL_QKV = 3 * DIL_HEADS * DIL_HEAD_DIM
DIL_SIDE = 64
SGU_CHUNK = 128
SGU_GROUPS = 4
MLA_HEADS = 4
MLA_Q_LORA = 384
MLA_KV_LORA = 128
MLA_NOPE = 128
MLA_ROPE = 64
MLA_V = 128
MLA_QK = MLA_NOPE + MLA_ROPE
MLA_PAD_QK = 256
MLA_HEADS_PER_STEP = 4
ROPE_THETA = 10000.0
EPS = 1e-6
NEG_BIG = -1e30
MLA_EXP2_SCALE = (MLA_QK ** -0.5) * math.log2(math.e)

OFF_POOL = 0
OFF_DIL = OFF_POOL + BRANCH_W
OFF_SGU = OFF_DIL + DIL_QKV
OFF_MLA_Q = OFF_SGU + 2 * BRANCH_W
OFF_MLA_KV = OFF_MLA_Q + MLA_Q_LORA
OFF_MLA_KR = OFF_MLA_KV + MLA_KV_LORA
OFF_GATE = OFF_MLA_KR + MLA_ROPE

LANE = 128
REST_POOL = 0
REST_SGU = 512
REST_CQ = 1536
REST_CKV = 1920
REST_KR = 2048
REST_W = REST_KR + LANE

VMEM_LIMIT = 56 * 1024 * 1024


def _params(*sem):
    return pltpu.CompilerParams(dimension_semantics=sem, vmem_limit_bytes=VMEM_LIMIT)


def _rms(xf, g):
    return xf * lax.rsqrt(jnp.mean(xf * xf, axis=-1, keepdims=True) + EPS) * g


FFN_PREFETCH_STEP = 1


def _ffn_body(x_hbm, g_ref, wg_ref, wu_ref, wd_ref, o_hbm, acc_ref, h_ref, in_sem, out_sem):
    i, f = pl.program_id(0), pl.program_id(1)
    ni, nf = pl.num_programs(0), pl.num_programs(1)
    tm = acc_ref.shape[1]
    slot = lax.rem(i, 2)
    other = 1 - slot

    def x_copy(tile, buf):
        return pltpu.make_async_copy(x_hbm.at[pl.ds(tile * tm, tm), :], acc_ref.at[buf], in_sem.at[buf])

    def out_copy(tile, buf):
        return pltpu.make_async_copy(acc_ref.at[buf], o_hbm.at[pl.ds(tile * tm, tm), :], out_sem.at[buf])

    @pl.when(f == 0)
    def _():
        @pl.when(i == 0)
        def _():
            x_copy(i, slot).start()
        x_copy(i, slot).wait()
        h_ref[...] = _rms(acc_ref[slot], g_ref[...]).astype(BF16)

    @pl.when((f == FFN_PREFETCH_STEP) & (i + 1 < ni))
    def _():
        @pl.when(i > 0)
        def _():
            out_copy(i - 1, other).wait()
        x_copy(i + 1, other).start()

    h = h_ref[...]
    g = jnp.dot(h, wg_ref[...].astype(BF16), preferred_element_type=F32)
    u = jnp.dot(h, wu_ref[...].astype(BF16), preferred_element_type=F32)
    a = (0.5 * (jax.nn.silu(g) * u)).astype(BF16)
    acc_ref[slot] += jnp.dot(a, wd_ref[...].astype(BF16), preferred_element_type=F32)

    @pl.when(f == nf - 1)
    def _():
        out_copy(i, slot).start()

        @pl.when(i == ni - 1)
        def _():
            @pl.when(ni > 1)
            def _():
                out_copy(i - 1, other).wait()
            out_copy(i, slot).wait()


def ffn(x, norm_g, wg, wu, wd, layer, *, tm=1024, tf=512):
    s, d = x.shape
    f = wg.shape[2]
    assert f // tf > FFN_PREFETCH_STEP
    return pl.pallas_call(
        _ffn_body,
        grid=(s // tm, f // tf),
        in_specs=[
            pl.BlockSpec(memory_space=pl.ANY),
            pl.BlockSpec((1, d), lambda i, j: (0, 0)),
            pl.BlockSpec((None, d, tf), lambda i, j: (layer, 0, j)),
            pl.BlockSpec((None, d, tf), lambda i, j: (layer, 0, j)),
            pl.BlockSpec((None, tf, d), lambda i, j: (layer, j, 0)),
        ],
        out_specs=pl.BlockSpec(memory_space=pl.ANY),
        out_shape=jax.ShapeDtypeStruct((s, d), F32),
        scratch_shapes=[pltpu.VMEM((2, tm, d), F32), pltpu.VMEM((tm, d), BF16),
                        pltpu.SemaphoreType.DMA((2,)), pltpu.SemaphoreType.DMA((2,))],
        compiler_params=_params("arbitrary", "arbitrary"),
        name="ffn",
    )(x, norm_g.reshape(1, d), wg, wu, wd)


def _proj_rest_body(x_ref, g_ref, wp_ref, ws_ref, wq_ref, wkv_ref, wkr_ref, p_ref, h_ref):
    h = _rms(x_ref[...], g_ref[...]).astype(BF16)
    h_ref[...] = h
    for w_ref, off in ((wp_ref, REST_POOL), (ws_ref, REST_SGU), (wq_ref, REST_CQ), (wkv_ref, REST_CKV),
                       (wkr_ref, REST_KR)):
        p_ref[:, off:off + w_ref.shape[1]] = jnp.dot(h, w_ref[...], preferred_element_type=F32)


def proj_rest(x, norm_g, w_main, layer, *, tm=512):
    s, d = x.shape

    def wspec(off, width):
        return pl.BlockSpec((None, d, width), lambda i: (layer, 0, off // width))

    return pl.pallas_call(
        _proj_rest_body,
        grid=(s // tm,),
        in_specs=[
            pl.BlockSpec((tm, d), lambda i: (i, 0)),
            pl.BlockSpec((1, d), lambda i: (0, 0)),
            wspec(OFF_POOL, BRANCH_W),
            wspec(OFF_SGU, 2 * BRANCH_W),
            wspec(OFF_MLA_Q, MLA_Q_LORA),
            wspec(OFF_MLA_KV, MLA_KV_LORA),
            wspec(OFF_MLA_KR, LANE),
        ],
        out_specs=[
            pl.BlockSpec((tm, REST_W), lambda i: (i, 0)),
            pl.BlockSpec((tm, d), lambda i: (i, 0)),
        ],
        out_shape=[jax.ShapeDtypeStruct((s, REST_W), F32), jax.ShapeDtypeStruct((s, d), BF16)],
        compiler_params=_params("parallel"),
        name="proj_rest",
    )(x, norm_g.reshape(1, d), w_main, w_main, w_main, w_main, w_main)


def _proj_dil_body(h_ref, w0_ref, w1_ref, w2_ref, g_ref, o0_ref, o1_ref, o2_ref, scr1_ref, scr2_ref):
    normed = pl.program_id(1) < 2
    h = h_ref[...]
    tm = h.shape[0]
    for w_ref, o_ref, scr_ref, (_, d) in reversed(list(zip((w0_ref, w1_ref, w2_ref), (o0_ref, o1_ref, o2_ref),
                                                           (None, scr1_ref, scr2_ref), DIL_PATTERNS))):
        acc = jnp.dot(h, w_ref[...], preferred_element_type=F32)
        for a in range(DIL_HEADS_PER_GROUP):
            sl = slice(a * DIL_HEAD_DIM, (a + 1) * DIL_HEAD_DIM)
            xh = acc[:, sl]
            r = lax.rsqrt(jnp.mean(xh * xh, axis=-1, keepdims=True) + EPS)
            y = xh * jnp.where(normed, r, 1.0) * g_ref[:, sl]
            if d == 1:
                o_ref[0, :, sl] = y.astype(BF16)
            else:
                scr_ref[a] = y
        if d > 1:
            for r in range(d):
                for a in range(DIL_HEADS_PER_GROUP):
                    sl = slice(a * DIL_HEAD_DIM, (a + 1) * DIL_HEAD_DIM)
                    o_ref[r, :, sl] = scr_ref[a, pl.ds(r, tm // d, stride=d), :].astype(BF16)


def proj_dil(h, w_main, gains, layer, *, tm=1024):
    s, dm = h.shape
    first = OFF_DIL // BRANCH_W

    def wspec(g):
        return pl.BlockSpec((None, dm, BRANCH_W), lambda i, j: (layer, 0, first + j * DIL_GROUPS + g))

    out_specs, out_shapes, scratch = [], [], []
    for _, d in DIL_PATTERNS:
        out_specs.append(pl.BlockSpec((d, tm // d, BRANCH_W), lambda i, j: (0, i, j)))
        out_shapes.append(jax.ShapeDtypeStruct((d, s // d, 3 * BRANCH_W), BF16))
        if d > 1:
            scratch.append(pltpu.VMEM((DIL_HEADS_PER_GROUP, tm, DIL_HEAD_DIM), F32))
    return pl.pallas_call(
        _proj_dil_body,
        grid=(s // tm, 3),
        in_specs=[
            pl.BlockSpec((tm, dm), lambda i, j: (i, 0)),
            wspec(0), wspec(1), wspec(2),
            pl.BlockSpec((None, 1, BRANCH_W), lambda i, j: (j, 0, 0)),
        ],
        out_specs=out_specs,
        out_shape=out_shapes,
        scratch_shapes=scratch,
        compiler_params=_params("parallel", "arbitrary"),
        name="proj_dil",
    )(h, w_main, w_main, w_main, gains)


def _pool_body(prev_ref, cur_ref, next_ref, w_ref, scale_ref, o_ref, ext_ref, *, seq):
    i = pl.program_id(0)
    tm = cur_ref.shape[0]
    cur = cur_ref[...]
    ext_ref[0:POOL_HALO, :] = jnp.where(i > 0, prev_ref[...], 0.0)
    ext_ref[POOL_HALO:POOL_HALO + tm, :] = cur
    ext_ref[POOL_HALO + tm:2 * POOL_HALO + tm, :] = jnp.where(
        i < pl.num_programs(0) - 1, next_ref[...], 0.0)
    t = i * tm + lax.broadcasted_iota(jnp.int32, (tm, 1), 0)
    for g, w in enumerate(POOL_WINDOWS):
        sl = slice(g * LANE, (g + 1) * LANE)
        tot = jnp.zeros((tm, LANE), F32)
        for o in range(-(w // 2), w // 2):
            tot = tot + ext_ref[POOL_HALO + o:POOL_HALO + o + tm, sl]
        cnt = (jnp.minimum(t + w // 2, seq) - jnp.maximum(t - w // 2, 0)).astype(F32)
        dev = (tot / cnt - cur[:, sl]).astype(BF16)
        y = jnp.dot(dev, w_ref[g], preferred_element_type=F32) * scale_ref[:, sl]
        o_ref[:, sl] = y.astype(BF16)


def pool_mixer(p_rest, pool_w, pool_scale, layer, *, tm=512):
    s = p_rest.shape[0]
    hb = tm // POOL_HALO
    last = s // POOL_HALO - 1
    n_win = len(POOL_WINDOWS)
    return pl.pallas_call(
        functools.partial(_pool_body, seq=s),
        grid=(s // tm,),
        in_specs=[
            pl.BlockSpec((POOL_HALO, BRANCH_W), lambda i: (jnp.maximum(i * hb - 1, 0), 0)),
            pl.BlockSpec((tm, BRANCH_W), lambda i: (i, 0)),
            pl.BlockSpec((POOL_HALO, BRANCH_W), lambda i: (jnp.minimum((i + 1) * hb, last), 0)),
            pl.BlockSpec((None, n_win, LANE, LANE), lambda i: (layer, 0, 0, 0)),
            pl.BlockSpec((1, BRANCH_W), lambda i: (0, 0)),
        ],
        out_specs=pl.BlockSpec((tm, BRANCH_W), lambda i: (i, 0)),
        out_shape=jax.ShapeDtypeStruct((s, BRANCH_W), BF16),
        scratch_shapes=[pltpu.VMEM((tm + 2 * POOL_HALO, BRANCH_W), F32)],
        compiler_params=_params("parallel"),
        name="pool_mixer",
    )(p_rest, p_rest, p_rest, pool_w, pool_scale.reshape(1, BRANCH_W))


def _alibi_slopes():
    n = DIL_HEADS
    return np.exp2(np.float32(-8.0) * np.arange(1, n + 1, dtype=np.float32) / np.float32(n))


def _dil_attn_body(q_ref, kp_ref, kc_ref, kn_ref, vp_ref, vc_ref, vn_ref, o_ref, lse_ref,
                   *, sub_len, penalties):
    i = pl.program_id(1)
    t = q_ref.shape[0]
    blk = kp_ref.shape[0]
    nk = blk + 2 * DIL_SIDE
    row = lax.broadcasted_iota(jnp.int32, (blk, nk), 0)
    col = lax.broadcasted_iota(jnp.int32, (blk, nk), 1)
    dist = jnp.abs(col - DIL_SIDE - row)
    in_band = dist <= DIL_SIDE
    distf = dist.astype(F32)
    for h in range(DIL_HEADS_PER_GROUP):
        sl = slice(h * DIL_HEAD_DIM, (h + 1) * DIL_HEAD_DIM)
        k_ext = jnp.concatenate([kp_ref[blk - DIL_SIDE:, sl], kc_ref[:, sl], kn_ref[:DIL_SIDE, sl]], axis=0)
        v_ext = jnp.concatenate([vp_ref[blk - DIL_SIDE:, sl], vc_ref[:, sl], vn_ref[:DIL_SIDE, sl]], axis=0)
        bias = jnp.where(in_band, -penalties[h] * distf, NEG_BIG)
        for b in range(t // blk):
            rows = slice(b * blk, (b + 1) * blk)
            kpos = i * t + b * blk + col - DIL_SIDE
            s = lax.dot_general(q_ref[rows, sl], k_ext[b * blk:b * blk + nk], (((1,), (1,)), ((), ())),
                                preferred_element_type=F32) * (DIL_HEAD_DIM ** -0.5)
            s = jnp.where((kpos >= 0) & (kpos < sub_len), s + bias, NEG_BIG)
            m = jnp.max(s, axis=-1, keepdims=True)
            p = jnp.exp(s - m)
            den = jnp.sum(p, axis=-1, keepdims=True)
            o_ref[rows, sl] = jnp.dot((p / den).astype(BF16), v_ext[b * blk:b * blk + nk],
                                      preferred_element_type=F32)
            lse_ref[rows, sl] = jnp.broadcast_to(m + jnp.log(den), (blk, DIL_HEAD_DIM))


def dil_attn(qkv, group, *, blk=128, max_blocks=8):
    d, sub_len, _ = qkv.shape
    n_blk = sub_len // blk
    qb = min(max_blocks, n_blk)
    t = blk * qb
    slopes = _alibi_slopes()[group * DIL_HEADS_PER_GROUP:(group + 1) * DIL_HEADS_PER_GROUP]
    penalties = tuple(float(np.float32(x) * np.float32(d)) for x in slopes)

    def halo(which, step):
        return pl.BlockSpec((None, blk, BRANCH_W),
                            lambda r, i: (r, jnp.clip(i * qb + step, 0, n_blk - 1), which))

    def cur(which):
        return pl.BlockSpec((None, t, BRANCH_W), lambda r, i: (r, i, which))

    out_spec = pl.BlockSpec((None, t, BRANCH_W), lambda r, i: (r, i, 0))
    return pl.pallas_call(
        functools.partial(_dil_attn_body, sub_len=sub_len, penalties=penalties),
        grid=(d, sub_len // t),
        in_specs=[cur(0), halo(1, -1), cur(1), halo(1, qb), halo(2, -1), cur(2), halo(2, qb)],
        out_specs=[out_spec, out_spec],
        out_shape=[jax.ShapeDtypeStruct((d, sub_len, BRANCH_W), F32)] * 2,
        compiler_params=_params("parallel", "parallel"),
        name=f"dil_attn_g{group}",
    )(qkv, qkv, qkv, qkv, qkv, qkv, qkv)


def _dil_combine_body(o0, o1, o2, l0, l1, l2, z_ref, *scratch):
    tm = z_ref.shape[0]
    nat = {}
    for idx, (src, (_, d)) in enumerate(((o1, DIL_PATTERNS[1]), (o2, DIL_PATTERNS[2]),
                                         (l1, DIL_PATTERNS[1]), (l2, DIL_PATTERNS[2]))):
        scr = scratch[idx]
        for r in range(d):
            for a in range(DIL_HEADS_PER_GROUP):
                sl = slice(a * DIL_HEAD_DIM, (a + 1) * DIL_HEAD_DIM)
                scr[a, pl.ds(r, tm // d, stride=d), :] = src[r, :, sl]
        nat[idx] = scr
    for a in range(DIL_HEADS_PER_GROUP):
        sl = slice(a * DIL_HEAD_DIM, (a + 1) * DIL_HEAD_DIM)
        la, lb, lc = l0[0, :, sl], nat[2][a], nat[3][a]
        m = jnp.maximum(jnp.maximum(la, lb), lc)
        ea, eb, ec = jnp.exp(la - m), jnp.exp(lb - m), jnp.exp(lc - m)
        tot = ea + eb + ec
        z = (ea / tot) * o0[0, :, sl] + (eb / tot) * nat[0][a] + (ec / tot) * nat[1][a]
        z_ref[:, sl] = z.astype(BF16)


def dil_combine(outs, lses, *, tm=512):
    s = outs[0].shape[1]
    specs = [pl.BlockSpec((d, tm // d, BRANCH_W), lambda i: (0, i, 0)) for _, d in DIL_PATTERNS]
    return pl.pallas_call(
        _dil_combine_body,
        grid=(s // tm,),
        in_specs=specs + specs,
        out_specs=pl.BlockSpec((tm, BRANCH_W), lambda i: (i, 0)),
        out_shape=jax.ShapeDtypeStruct((s, BRANCH_W), BF16),
        scratch_shapes=[pltpu.VMEM((DIL_HEADS_PER_GROUP, tm, DIL_HEAD_DIM), F32)] * 4,
        compiler_params=_params("parallel"),
        name="dil_combine",
    )(*outs, *lses)


def _sgu_body(u_ref, v_ref, g_ref, ws_ref, b_ref, o_ref):
    tm = u_ref.shape[0]
    u = jax.nn.gelu(u_ref[...])
    vn = _rms(jax.nn.gelu(v_ref[...]), g_ref[...]).astype(BF16)
    for n in range(tm // SGU_CHUNK):
        rows = slice(n * SGU_CHUNK, (n + 1) * SGU_CHUNK)
        for g in range(SGU_GROUPS):
            cols = slice(g * LANE, (g + 1) * LANE)
            mixed = jnp.dot(ws_ref[g], vn[rows, cols], preferred_element_type=F32) + b_ref[g]
            o_ref[rows, cols] = (u[rows, cols] * mixed).astype(BF16)


def sgu_mixer(p_rest, v_norm_g, ws, b_full, layer, *, tm=512):
    s = p_rest.shape[0]
    return pl.pallas_call(
        _sgu_body,
        grid=(s // tm,),
        in_specs=[
            pl.BlockSpec((tm, BRANCH_W), lambda i: (i, REST_SGU // BRANCH_W)),
            pl.BlockSpec((tm, BRANCH_W), lambda i: (i, REST_SGU // BRANCH_W + 1)),
            pl.BlockSpec((1, BRANCH_W), lambda i: (0, 0)),
            pl.BlockSpec((None, SGU_GROUPS, SGU_CHUNK, SGU_CHUNK), lambda i: (layer, 0, 0, 0)),
            pl.BlockSpec((SGU_GROUPS, SGU_CHUNK, LANE), lambda i: (0, 0, 0)),
        ],
        out_specs=pl.BlockSpec((tm, BRANCH_W), lambda i: (i, 0)),
        out_shape=jax.ShapeDtypeStruct((s, BRANCH_W), BF16),
        compiler_params=_params("parallel"),
        name="sgu_mixer",
    )(p_rest, p_rest, v_norm_g.reshape(1, BRANCH_W), ws, b_full)


def _swap_rope_halves(y):
    width = y.shape[1]
    lane = lax.broadcasted_iota(jnp.int32, y.shape, 1)
    first_half = (lane % MLA_ROPE) < MLA_ROPE // 2
    return jnp.where(first_half, pltpu.roll(y, width - MLA_ROPE // 2, 1),
                     pltpu.roll(y, MLA_ROPE // 2, 1))


def _mla_prep_body(cq_ref, ckv_ref, kr_ref, cos_ref, sin_ref, qa_g, wuq_ref, kva_g, wukv_ref,
                   gqn_ref, gqr_ref, gkn_ref, gkr_ref, q_ref, k_ref, v_ref):
    nope_w = MLA_HEADS * MLA_NOPE
    qall = jnp.dot(_rms(cq_ref[...], qa_g[...]).astype(BF16), wuq_ref[...], preferred_element_type=F32)
    kv = jnp.dot(_rms(ckv_ref[...], kva_g[...]).astype(BF16), wukv_ref[...], preferred_element_type=F32)
    qn, qr = qall[:, :nope_w], qall[:, nope_w:]
    kn = kv[:, :nope_w]
    v_ref[...] = kv[:, nope_w:].astype(BF16)
    kr_blk = kr_ref[...]
    lane128 = lax.broadcasted_iota(jnp.int32, kr_blk.shape, 1)
    kr2 = jnp.where(lane128 < MLA_ROPE, kr_blk, pltpu.roll(kr_blk, MLA_ROPE, 1))
    kr = jnp.concatenate([kr2] * (MLA_HEADS // 2), axis=1)

    rope_w = MLA_HEADS * MLA_ROPE
    head_of_lane = lax.broadcasted_iota(jnp.int32, (1, rope_w), 1) // MLA_ROPE
    qr_sq = qr * qr
    kr_ssq = jnp.sum(jnp.where(head_of_lane == 0, kr * kr, 0.0), axis=-1, keepdims=True)
    rq, rk = [], []
    for h in range(MLA_HEADS):
        sl = slice(h * MLA_NOPE, (h + 1) * MLA_NOPE)
        q_ssq = (jnp.sum(qn[:, sl] * qn[:, sl], axis=-1, keepdims=True)
                 + jnp.sum(jnp.where(head_of_lane == h, qr_sq, 0.0), axis=-1, keepdims=True))
        k_ssq = jnp.sum(kn[:, sl] * kn[:, sl], axis=-1, keepdims=True) + kr_ssq
        rq.append(lax.rsqrt(q_ssq / MLA_QK + EPS))
        rk.append(lax.rsqrt(k_ssq / MLA_QK + EPS))

    def per_lane(rs):
        out = jnp.zeros((kr.shape[0], rope_w), F32)
        for h in range(MLA_HEADS):
            out = jnp.where(head_of_lane == h, rs[h], out)
        return out

    def rotate(x, rs, g_ref):
        y = x * per_lane(rs) * g_ref[...]
        return y * cos_ref[...] + _swap_rope_halves(y) * sin_ref[...]

    q_rope = rotate(qr, rq, gqr_ref) * MLA_EXP2_SCALE
    k_rope = rotate(kr, rk, gkr_ref)
    lane_tile = lax.broadcasted_iota(jnp.int32, (1, LANE), 1) // MLA_ROPE
    for h in range(MLA_HEADS):
        sl = slice(h * MLA_NOPE, (h + 1) * MLA_NOPE)
        base = h * MLA_PAD_QK
        tile = slice((h // 2) * LANE, (h // 2 + 1) * LANE)
        q_ref[:, base:base + MLA_NOPE] = (qn[:, sl] * rq[h] * gqn_ref[:, sl] * MLA_EXP2_SCALE).astype(BF16)
        k_ref[:, base:base + MLA_NOPE] = (kn[:, sl] * rk[h] * gkn_ref[:, sl]).astype(BF16)
        q_ref[:, base + MLA_NOPE:base + MLA_PAD_QK] = jnp.where(
            lane_tile == h % 2, q_rope[:, tile], 0.0).astype(BF16)
        k_ref[:, base + MLA_NOPE:base + MLA_PAD_QK] = k_rope[:, tile].astype(BF16)


def mla_prep(p_rest, cos4, sin4, qa_g, wuq, kva_g, wukv, gqn, gqr, gkn, gkr, *, tm=512):
    s = p_rest.shape[0]
    rope_w = MLA_HEADS * MLA_ROPE

    def const(shape):
        return pl.BlockSpec(shape, lambda i: (0,) * len(shape))

    return pl.pallas_call(
        _mla_prep_body,
        grid=(s // tm,),
        in_specs=[
            pl.BlockSpec((tm, MLA_Q_LORA), lambda i: (i, REST_CQ // MLA_Q_LORA)),
            pl.BlockSpec((tm, MLA_KV_LORA), lambda i: (i, REST_CKV // MLA_KV_LORA)),
            pl.BlockSpec((tm, LANE), lambda i: (i, REST_KR // LANE)),
            pl.BlockSpec((tm, rope_w), lambda i: (i, 0)),
            pl.BlockSpec((tm, rope_w), lambda i: (i, 0)),
            const((1, MLA_Q_LORA)), const(wuq.shape), const((1, MLA_KV_LORA)), const(wukv.shape),
            const(gqn.shape), const(gqr.shape), const(gkn.shape), const(gkr.shape),
        ],
        out_specs=[
            pl.BlockSpec((tm, MLA_HEADS * MLA_PAD_QK), lambda i: (i, 0)),
            pl.BlockSpec((tm, MLA_HEADS * MLA_PAD_QK), lambda i: (i, 0)),
            pl.BlockSpec((tm, MLA_HEADS * MLA_V), lambda i: (i, 0)),
        ],
        out_shape=[
            jax.ShapeDtypeStruct((s, MLA_HEADS * MLA_PAD_QK), BF16),
            jax.ShapeDtypeStruct((s, MLA_HEADS * MLA_PAD_QK), BF16),
            jax.ShapeDtypeStruct((s, MLA_HEADS * MLA_V), BF16),
        ],
        compiler_params=_params("parallel"),
        name="mla_prep",
    )(p_rest, p_rest, p_rest, cos4, sin4, qa_g.reshape(1, -1), wuq, kva_g.reshape(1, -1), wukv,
      gqn, gqr, gkn, gkr)


def _mla_attn_body(q_ref, k_ref, v_ref, o_ref, m_ref, l_ref, acc_ref):
    j = pl.program_id(2)

    @pl.when(j == 0)
    def _():
        m_ref[...] = jnp.full(m_ref.shape, NEG_BIG, F32)
        l_ref[...] = jnp.zeros(l_ref.shape, F32)
        acc_ref[...] = jnp.zeros(acc_ref.shape, F32)

    tk = k_ref.shape[0]
    for h in range(MLA_HEADS_PER_STEP):
        qk = slice(h * MLA_PAD_QK, (h + 1) * MLA_PAD_QK)
        vs = slice(h * MLA_V, (h + 1) * MLA_V)
        s = lax.dot_general(q_ref[:, qk], k_ref[:, qk], (((1,), (1,)), ((), ())),
                            preferred_element_type=F32)
        m_prev = m_ref[h]
        m_new = jnp.maximum(m_prev, jnp.max(s, axis=-1, keepdims=True))
        alpha = jnp.exp2(m_prev - m_new)
        p = jnp.exp2(s - jnp.tile(m_new, (1, tk // LANE)))
        l_ref[h] = alpha * l_ref[h] + jnp.sum(p, axis=-1, keepdims=True)
        acc_ref[h] = alpha * acc_ref[h] + jnp.dot(p.astype(BF16), v_ref[:, vs], preferred_element_type=F32)
        m_ref[h] = m_new

    @pl.when(j == pl.num_programs(2) - 1)
    def _():
        for h in range(MLA_HEADS_PER_STEP):
            o_ref[:, h * MLA_V:(h + 1) * MLA_V] = (acc_ref[h] / l_ref[h]).astype(BF16)


def mla_attn(q, k, v, *, tq=1024, tk=2048):
    s = q.shape[0]
    hp = MLA_HEADS_PER_STEP
    stat = pltpu.VMEM((hp, tq, LANE), F32)
    return pl.pallas_call(
        _mla_attn_body,
        grid=(s // tq, MLA_HEADS // hp, s // tk),
        in_specs=[
            pl.BlockSpec((tq, hp * MLA_PAD_QK), lambda i, h, j: (i, h)),
            pl.BlockSpec((tk, hp * MLA_PAD_QK), lambda i, h, j: (j, h)),
            pl.BlockSpec((tk, hp * MLA_V), lambda i, h, j: (j, h)),
        ],
        out_specs=pl.BlockSpec((tq, hp * MLA_V), lambda i, h, j: (i, h)),
        out_shape=jax.ShapeDtypeStruct((s, MLA_HEADS * MLA_V), BF16),
        scratch_shapes=[stat, stat, stat],
        compiler_params=_params("parallel", "parallel", "arbitrary"),
        name="mla_attn",
    )(q, k, v)


def _merge_body(h_ref, *refs):
    z_refs, wg_refs, wb_refs = refs[:N_BRANCH], refs[N_BRANCH:2 * N_BRANCH], refs[2 * N_BRANCH:3 * N_BRANCH]
    o_ref = refs[3 * N_BRANCH]
    h = h_ref[...]
    acc = None
    for z_ref, wg_ref, wb_ref in zip(z_refs, wg_refs, wb_refs):
        gate = jax.nn.sigmoid(jnp.dot(h, wg_ref[...], preferred_element_type=F32))
        term = gate * jnp.dot(z_ref[...], wb_ref[...], preferred_element_type=F32)
        acc = term if acc is None else acc + term
    o_ref[...] = acc.astype(BF16)


def merge(h, zs, w_gate, w_branch, layer, *, tm=1024, tn=512):
    s, d = h.shape
    ct = d // tn
    z_spec = pl.BlockSpec((tm, BRANCH_W), lambda i, c: (i, 0))
    g_specs = [pl.BlockSpec((None, d, tn), lambda i, c, n=n: (layer, 0, n * ct + c)) for n in range(N_BRANCH)]
    b_specs = [pl.BlockSpec((None, None, BRANCH_W, tn), lambda i, c, n=n: (layer, n, 0, c))
               for n in range(N_BRANCH)]
    return pl.pallas_call(
        _merge_body,
        grid=(s // tm, ct),
        in_specs=[pl.BlockSpec((tm, d), lambda i, c: (i, 0))] + [z_spec] * N_BRANCH + g_specs + b_specs,
        out_specs=pl.BlockSpec((tm, tn), lambda i, c: (i, c)),
        out_shape=jax.ShapeDtypeStruct((s, d), BF16),
        compiler_params=_params("parallel", "arbitrary"),
        name="merge",
    )(h, *zs, *([w_gate] * N_BRANCH), *([w_branch] * N_BRANCH))


def _out_proj_body(x_ref, y_ref, w_ref, o_ref):
    o_ref[...] = x_ref[...] + jnp.dot(y_ref[...], w_ref[...], preferred_element_type=F32)


def out_proj(x, y, w, layer, *, tm=1024, tn=1024):
    s, d = x.shape
    return pl.pallas_call(
        _out_proj_body,
        grid=(s // tm, d // tn),
        in_specs=[
            pl.BlockSpec((tm, tn), lambda i, j: (i, j)),
            pl.BlockSpec((tm, d), lambda i, j: (i, 0)),
            pl.BlockSpec((None, d, tn), lambda i, j: (layer, 0, j)),
        ],
        out_specs=pl.BlockSpec((tm, tn), lambda i, j: (i, j)),
        out_shape=jax.ShapeDtypeStruct((s, d), F32),
        compiler_params=_params("parallel", "arbitrary"),
        name="out_proj",
    )(x, y, w)


W_MAIN_COLS = OFF_MLA_KR + LANE
GATE_SHIFT = OFF_GATE - OFF_MLA_KR


def _cast_body(x_ref, o_ref):
    o_ref[...] = x_ref[...].astype(BF16)


def _cast_gate_body(a_ref, b_ref, shift_ref, o_ref):
    last = pl.program_id(2) == pl.num_programs(2) - 1
    lane = lax.broadcasted_iota(jnp.int32, b_ref.shape, 1)
    b = jnp.where(last & (lane >= GATE_SHIFT), 0.0, b_ref[...])
    x = jnp.concatenate([a_ref[...], b], axis=1).astype(BF16)
    o_ref[...] = jnp.dot(x, shift_ref[...], preferred_element_type=F32).astype(BF16)


def cast_w_in(w_in, *, tr_main=256, tr_gate=2048, tn=512):
    nl, d, _ = w_in.shape
    main = pl.pallas_call(
        _cast_body,
        grid=(nl, d // tr_main),
        in_specs=[pl.BlockSpec((None, tr_main, W_MAIN_COLS), lambda l, r: (l, r, 0))],
        out_specs=pl.BlockSpec((None, tr_main, W_MAIN_COLS), lambda l, r: (l, r, 0)),
        out_shape=jax.ShapeDtypeStruct((nl, d, W_MAIN_COLS), BF16),
        compiler_params=_params("parallel", "parallel"),
        name="cast_w_main",
    )(w_in)
    gate_w = N_BRANCH * D_MODEL
    rows = np.arange(tn + LANE)[:, None]
    shift = jnp.asarray(rows == np.arange(tn)[None, :] + GATE_SHIFT, BF16)
    gate = pl.pallas_call(
        _cast_gate_body,
        grid=(nl, d // tr_gate, gate_w // tn),
        in_specs=[
            pl.BlockSpec((None, tr_gate, tn), lambda l, r, c: (l, r, OFF_MLA_KR // tn + c)),
            pl.BlockSpec((None, tr_gate, LANE), lambda l, r, c: (l, r, (OFF_MLA_KR + (c + 1) * tn) // LANE)),
            pl.BlockSpec((tn + LANE, tn), lambda l, r, c: (0, 0)),
        ],
        out_specs=pl.BlockSpec((None, tr_gate, tn), lambda l, r, c: (l, r, c)),
        out_shape=jax.ShapeDtypeStruct((nl, d, gate_w), BF16),
        compiler_params=_params("parallel", "parallel", "arbitrary"),
        name="cast_w_gate",
    )(w_in, w_in, shift)
    return main, gate


def _rope_tables(s):
    pos = jnp.arange(s, dtype=F32)
    inv_freq = ROPE_THETA ** (-jnp.arange(0, MLA_ROPE, 2, dtype=F32) / MLA_ROPE)
    ang = pos[:, None] * inv_freq[None, :]
    cos, sin = jnp.cos(ang), jnp.sin(ang)
    cos4 = jnp.tile(jnp.concatenate([cos, cos], axis=-1), (1, MLA_HEADS))
    sin4 = jnp.tile(jnp.concatenate([-sin, sin], axis=-1), (1, MLA_HEADS))
    return cos4, sin4


def _split_heads(w, widths):
    per_head = sum(widths)
    w3 = w.reshape(w.shape[0], MLA_HEADS, per_head)
    parts, off = [], 0
    for wd in widths:
        parts.append(w3[:, :, off:off + wd].reshape(w.shape[0], MLA_HEADS * wd))
        off += wd
    return jnp.concatenate(parts, axis=1)


def kernel(x, ffn1_norm, ffn1_w_gate, ffn1_w_up, ffn1_w_down, mix_norm, w_in, pool_w, pool_scale,
           dil_q_norm, dil_k_norm, sgu_v_norm, sgu_w, sgu_b, mla_q_a_norm, mla_w_uq, mla_kv_a_norm,
           mla_w_ukv, mla_q_norm, mla_k_norm, w_branch, w_out, ffn2_norm, ffn2_w_gate, ffn2_w_up,
           ffn2_w_down):
    b, s, d = x.shape
    cos4, sin4 = _rope_tables(s)
    w_main, w_gate = cast_w_in(w_in)
    w_br, w_o = w_branch.astype(BF16), w_out.astype(BF16)
    pool_wb, sgu_wb = pool_w.astype(BF16), sgu_w.astype(BF16)
    ones = jnp.ones((BRANCH_W,), F32)

    outs = []
    for bi in range(b):
        xb = x[bi]
        for l in range(DEPTH):
            xb = ffn(xb, ffn1_norm[l], ffn1_w_gate, ffn1_w_up, ffn1_w_down, l)

            p_rest, h = proj_rest(xb, mix_norm[l], w_main, l)
            dil_gains = jnp.stack([jnp.tile(dil_q_norm[l], DIL_HEADS_PER_GROUP),
                                   jnp.tile(dil_k_norm[l], DIL_HEADS_PER_GROUP), ones]).reshape(3, 1, BRANCH_W)
            qkvs = proj_dil(h, w_main, dil_gains, l)
            z_a = pool_mixer(p_rest, pool_wb, pool_scale[l], l)
            o_l = [dil_attn(qkvs[g], g) for g in range(DIL_GROUPS)]
            z_b = dil_combine([o for o, _ in o_l], [ls for _, ls in o_l])
            b_full = jnp.broadcast_to(sgu_b[l][:, :, None], (SGU_GROUPS, SGU_CHUNK, LANE))
            z_c = sgu_mixer(p_rest, sgu_v_norm[l], sgu_wb, b_full, l)
            q, k, v = mla_prep(
                p_rest, cos4, sin4, mla_q_a_norm[l],
                _split_heads(mla_w_uq[l], (MLA_NOPE, MLA_ROPE)).astype(BF16),
                mla_kv_a_norm[l], _split_heads(mla_w_ukv[l], (MLA_NOPE, MLA_V)).astype(BF16),
                jnp.tile(mla_q_norm[l][:MLA_NOPE], MLA_HEADS).reshape(1, -1),
                jnp.tile(mla_q_norm[l][MLA_NOPE:], MLA_HEADS).reshape(1, -1),
                jnp.tile(mla_k_norm[l][:MLA_NOPE], MLA_HEADS).reshape(1, -1),
                jnp.tile(mla_k_norm[l][MLA_NOPE:], MLA_HEADS).reshape(1, -1))
            z_d = mla_attn(q, k, v)
            merged = merge(h, (z_a, z_b, z_c, z_d), w_gate, w_br, l)
            xb = out_proj(xb, merged, w_o, l)

            xb = ffn(xb, ffn2_norm[l], ffn2_w_gate, ffn2_w_up, ffn2_w_down, l)
        outs.append(xb)
    return jnp.stack(outs, axis=0)
```

```python
import functools
import math

import numpy as np
import jax
import jax.numpy as jnp
from jax import lax
from jax.experimental import pallas as pl
from jax.experimental.pallas import tpu as pltpu

F32 = jnp.float32
BF16 = jnp.bfloat16

D_MODEL = 2048
DEPTH = 4
D_FF = 5632
N_BRANCH = 4
BRANCH_W = 512
POOL_WINDOWS = (2, 4, 8, 16)
POOL_HALO = 8
DIL_PATTERNS = ((128, 1), (512, 4), (2048, 16))
DIL_GROUPS = len(DIL_PATTERNS)
DIL_HEADS = 12
DIL_HEADS_PER_GROUP = 4
DIL_HEAD_DIM = 128
DIL_QKV = 3 * DIL_HEADS * DIL_HEAD_DIM
DIL_SIDE = 64
SGU_CHUNK = 128
SGU_GROUPS = 4
MLA_HEADS = 4
MLA_Q_LORA = 384
MLA_KV_LORA = 128
MLA_NOPE = 128
MLA_ROPE = 64
MLA_V = 128
MLA_QK = MLA_NOPE + MLA_ROPE
MLA_PAD_QK = 256
MLA_HEADS_PER_STEP = 4
ROPE_THETA = 10000.0
EPS = 1e-6
NEG_BIG = -1e30
MLA_EXP2_SCALE = (MLA_QK ** -0.5) * math.log2(math.e)

OFF_POOL = 0
OFF_DIL = OFF_POOL + BRANCH_W
OFF_SGU = OFF_DIL + DIL_QKV
OFF_MLA_Q = OFF_SGU + 2 * BRANCH_W
OFF_MLA_KV = OFF_MLA_Q + MLA_Q_LORA
OFF_MLA_KR = OFF_MLA_KV + MLA_KV_LORA
OFF_GATE = OFF_MLA_KR + MLA_ROPE

LANE = 128
REST_POOL = 0
REST_SGU = 512
REST_CQ = 1536
REST_CKV = 1920
REST_KR = 2048
REST_W = REST_KR + LANE

VMEM_LIMIT = 56 * 1024 * 1024


def _params(*sem):
    return pltpu.CompilerParams(dimension_semantics=sem, vmem_limit_bytes=VMEM_LIMIT)


def _rms(xf, g):
    return xf * lax.rsqrt(jnp.mean(xf * xf, axis=-1, keepdims=True) + EPS) * g


def _dot_nt(a, b_t):
    return lax.dot_general(a, b_t, (((1,), (1,)), ((), ())), preferred_element_type=F32)


FFN_PREFETCH_STEP = 1


def _ffn_body(x_hbm, g_ref, wg_ref, wu_ref, wd_ref, o_hbm, acc_ref, h_ref, in_sem, out_sem):
    i, f = pl.program_id(0), pl.program_id(1)
    ni, nf = pl.num_programs(0), pl.num_programs(1)
    tm = acc_ref.shape[1]
    slot = lax.rem(i, 2)
    other = 1 - slot

    def x_copy(tile, buf):
        return pltpu.make_async_copy(x_hbm.at[pl.ds(tile * tm, tm), :], acc_ref.at[buf], in_sem.at[buf])

    def out_copy(tile, buf):
        return pltpu.make_async_copy(acc_ref.at[buf], o_hbm.at[pl.ds(tile * tm, tm), :], out_sem.at[buf])

    @pl.when(f == 0)
    def _():
        @pl.when(i == 0)
        def _():
            x_copy(i, slot).start()
        x_copy(i, slot).wait()
        h_ref[...] = _rms(acc_ref[slot], g_ref[...]).astype(BF16)

    @pl.when((f == FFN_PREFETCH_STEP) & (i + 1 < ni))
    def _():
        @pl.when(i > 0)
        def _():
            out_copy(i - 1, other).wait()
        x_copy(i + 1, other).start()

    h = h_ref[...]
    g = jnp.dot(h, wg_ref[...].astype(BF16), preferred_element_type=F32)
    u = jnp.dot(h, wu_ref[...].astype(BF16), preferred_element_type=F32)
    a = (0.5 * (jax.nn.silu(g) * u)).astype(BF16)
    acc_ref[slot] += jnp.dot(a, wd_ref[...].astype(BF16), preferred_element_type=F32)

    @pl.when(f == nf - 1)
    def _():
        out_copy(i, slot).start()

        @pl.when(i == ni - 1)
        def _():
            @pl.when(ni > 1)
            def _():
                out_copy(i - 1, other).wait()
            out_copy(i, slot).wait()


def ffn(x, norm_g, wg, wu, wd, layer, *, tm=1024, tf=512):
    s, d = x.shape
    f = wg.shape[2]
    assert f // tf > FFN_PREFETCH_STEP
    return pl.pallas_call(
        _ffn_body,
        grid=(s // tm, f // tf),
        in_specs=[
            pl.BlockSpec(memory_space=pl.ANY),
            pl.BlockSpec((1, d), lambda i, j: (0, 0)),
            pl.BlockSpec((None, d, tf), lambda i, j: (layer, 0, j)),
            pl.BlockSpec((None, d, tf), lambda i, j: (layer, 0, j)),
            pl.BlockSpec((None, tf, d), lambda i, j: (layer, j, 0)),
        ],
        out_specs=pl.BlockSpec(memory_space=pl.ANY),
        out_shape=jax.ShapeDtypeStruct((s, d), F32),
        scratch_shapes=[pltpu.VMEM((2, tm, d), F32), pltpu.VMEM((tm, d), BF16),
                        pltpu.SemaphoreType.DMA((2,)), pltpu.SemaphoreType.DMA((2,))],
        compiler_params=_params("arbitrary", "arbitrary"),
        name="ffn",
    )(x, norm_g.reshape(1, d), wg, wu, wd)


def _proj_rest_body(x_ref, g_ref, wp_ref, ws_ref, wq_ref, wkv_ref, wkr_ref, p_ref, h_ref):
    h = _rms(x_ref[...], g_ref[...]).astype(BF16)
    h_ref[...] = h
    for w_ref, off in ((wp_ref, REST_POOL), (ws_ref, REST_SGU), (wq_ref, REST_CQ), (wkv_ref, REST_CKV),
                       (wkr_ref, REST_KR)):
        p_ref[:, off:off + w_ref.shape[0]] = _dot_nt(h, w_ref[...])


def proj_rest(x, norm_g, w_main, layer, *, tm=512):
    s, d = x.shape

    def wspec(off, width):
        return pl.BlockSpec((None, width, d), lambda i: (layer, off // width, 0))

    return pl.pallas_call(
        _proj_rest_body,
        grid=(s // tm,),
        in_specs=[
            pl.BlockSpec((tm, d), lambda i: (i, 0)),
            pl.BlockSpec((1, d), lambda i: (0, 0)),
            wspec(OFF_POOL, BRANCH_W),
            wspec(OFF_SGU, 2 * BRANCH_W),
            wspec(OFF_MLA_Q, MLA_Q_LORA),
            wspec(OFF_MLA_KV, MLA_KV_LORA),
            wspec(OFF_MLA_KR, LANE),
        ],
        out_specs=[
            pl.BlockSpec((tm, REST_W), lambda i: (i, 0)),
            pl.BlockSpec((tm, d), lambda i: (i, 0)),
        ],
        out_shape=[jax.ShapeDtypeStruct((s, REST_W), F32), jax.ShapeDtypeStruct((s, d), BF16)],
        compiler_params=_params("parallel"),
        name="proj_rest",
    )(x, norm_g.reshape(1, d), w_main, w_main, w_main, w_main, w_main)


def _proj_dil_body(h_ref, w0_ref, w1_ref, w2_ref, g_ref, o0_ref, o1_ref, o2_ref, scr1_ref, scr2_ref):
    normed = pl.program_id(1) < 2
    h = h_ref[...]
    tm = h.shape[0]
    for w_ref, o_ref, scr_ref, (_, d) in reversed(list(zip((w0_ref, w1_ref, w2_ref), (o0_ref, o1_ref, o2_ref),
                                                           (None, scr1_ref, scr2_ref), DIL_PATTERNS))):
        acc = _dot_nt(h, w_ref[...])
        for a in range(DIL_HEADS_PER_GROUP):
            sl = slice(a * DIL_HEAD_DIM, (a + 1) * DIL_HEAD_DIM)
            xh = acc[:, sl]
            r = lax.rsqrt(jnp.mean(xh * xh, axis=-1, keepdims=True) + EPS)
            y = xh * jnp.where(normed, r, 1.0) * g_ref[:, sl]
            if d == 1:
                o_ref[0, :, sl] = y.astype(BF16)
            else:
                scr_ref[a] = y
        if d > 1:
            for r in range(d):
                for a in range(DIL_HEADS_PER_GROUP):
                    sl = slice(a * DIL_HEAD_DIM, (a + 1) * DIL_HEAD_DIM)
                    o_ref[r, :, sl] = scr_ref[a, pl.ds(r, tm // d, stride=d), :].astype(BF16)


def proj_dil(h, w_main, gains, layer, *, tm=1024):
    s, dm = h.shape
    first = OFF_DIL // BRANCH_W

    def wspec(g):
        return pl.BlockSpec((None, BRANCH_W, dm), lambda i, j: (layer, first + j * DIL_GROUPS + g, 0))

    out_specs, out_shapes, scratch = [], [], []
    for _, d in DIL_PATTERNS:
        out_specs.append(pl.BlockSpec((d, tm // d, BRANCH_W), lambda i, j: (0, i, j)))
        out_shapes.append(jax.ShapeDtypeStruct((d, s // d, 3 * BRANCH_W), BF16))
        if d > 1:
            scratch.append(pltpu.VMEM((DIL_HEADS_PER_GROUP, tm, DIL_HEAD_DIM), F32))
    return pl.pallas_call(
        _proj_dil_body,
        grid=(s // tm, 3),
        in_specs=[
            pl.BlockSpec((tm, dm), lambda i, j: (i, 0)),
            wspec(0), wspec(1), wspec(2),
            pl.BlockSpec((None, 1, BRANCH_W), lambda i, j: (j, 0, 0)),
        ],
        out_specs=out_specs,
        out_shape=out_shapes,
        scratch_shapes=scratch,
        compiler_params=_params("parallel", "arbitrary"),
        name="proj_dil",
    )(h, w_main, w_main, w_main, gains)


def _pool_body(prev_ref, cur_ref, next_ref, w_ref, scale_ref, o_ref, ext_ref, *, seq):
    i = pl.program_id(0)
    tm = cur_ref.shape[0]
    cur = cur_ref[...]
    ext_ref[0:POOL_HALO, :] = jnp.where(i > 0, prev_ref[...], 0.0)
    ext_ref[POOL_HALO:POOL_HALO + tm, :] = cur
    ext_ref[POOL_HALO + tm:2 * POOL_HALO + tm, :] = jnp.where(
        i < pl.num_programs(0) - 1, next_ref[...], 0.0)
    t = i * tm + lax.broadcasted_iota(jnp.int32, (tm, 1), 0)
    for g, w in enumerate(POOL_WINDOWS):
        sl = slice(g * LANE, (g + 1) * LANE)
        tot = jnp.zeros((tm, LANE), F32)
        for o in range(-(w // 2), w // 2):
            tot = tot + ext_ref[POOL_HALO + o:POOL_HALO + o + tm, sl]
        cnt = (jnp.minimum(t + w // 2, seq) - jnp.maximum(t - w // 2, 0)).astype(F32)
        dev = (tot / cnt - cur[:, sl]).astype(BF16)
        y = jnp.dot(dev, w_ref[g], preferred_element_type=F32) * scale_ref[:, sl]
        o_ref[:, sl] = y.astype(BF16)


def pool_mixer(p_rest, pool_w, pool_scale, layer, *, tm=512):
    s = p_rest.shape[0]
    hb = tm // POOL_HALO
    last = s // POOL_HALO - 1
    n_win = len(POOL_WINDOWS)
    return pl.pallas_call(
        functools.partial(_pool_body, seq=s),
        grid=(s // tm,),
        in_specs=[
            pl.BlockSpec((POOL_HALO, BRANCH_W), lambda i: (jnp.maximum(i * hb - 1, 0), 0)),
            pl.BlockSpec((tm, BRANCH_W), lambda i: (i, 0)),
            pl.BlockSpec((POOL_HALO, BRANCH_W), lambda i: (jnp.minimum((i + 1) * hb, last), 0)),
            pl.BlockSpec((None, n_win, LANE, LANE), lambda i: (layer, 0, 0, 0)),
            pl.BlockSpec((1, BRANCH_W), lambda i: (0, 0)),
        ],
        out_specs=pl.BlockSpec((tm, BRANCH_W), lambda i: (i, 0)),
        out_shape=jax.ShapeDtypeStruct((s, BRANCH_W), BF16),
        scratch_shapes=[pltpu.VMEM((tm + 2 * POOL_HALO, BRANCH_W), F32)],
        compiler_params=_params("parallel"),
        name="pool_mixer",
    )(p_rest, p_rest, p_rest, pool_w, pool_scale.reshape(1, BRANCH_W))


def _alibi_slopes():
    n = DIL_HEADS
    return np.exp2(np.float32(-8.0) * np.arange(1, n + 1, dtype=np.float32) / np.float32(n))


def _dil_attn_body(q_ref, kp_ref, kc_ref, kn_ref, vp_ref, vc_ref, vn_ref, o_ref, lse_ref,
                   *, sub_len, penalties):
    i = pl.program_id(1)
    t = q_ref.shape[0]
    blk = kp_ref.shape[0]
    nk = blk + 2 * DIL_SIDE
    row = lax.broadcasted_iota(jnp.int32, (blk, nk), 0)
    col = lax.broadcasted_iota(jnp.int32, (blk, nk), 1)
    dist = jnp.abs(col - DIL_SIDE - row)
    in_band = dist <= DIL_SIDE
    distf = dist.astype(F32)
    for h in range(DIL_HEADS_PER_GROUP):
        sl = slice(h * DIL_HEAD_DIM, (h + 1) * DIL_HEAD_DIM)
        k_ext = jnp.concatenate([kp_ref[blk - DIL_SIDE:, sl], kc_ref[:, sl], kn_ref[:DIL_SIDE, sl]], axis=0)
        v_ext = jnp.concatenate([vp_ref[blk - DIL_SIDE:, sl], vc_ref[:, sl], vn_ref[:DIL_SIDE, sl]], axis=0)
        bias = jnp.where(in_band, -penalties[h] * distf, NEG_BIG)
        for b in range(t // blk):
            rows = slice(b * blk, (b + 1) * blk)
            kpos = i * t + b * blk + col - DIL_SIDE
            s = lax.dot_general(q_ref[rows, sl], k_ext[b * blk:b * blk + nk], (((1,), (1,)), ((), ())),
                                preferred_element_type=F32) * (DIL_HEAD_DIM ** -0.5)
            s = jnp.where((kpos >= 0) & (kpos < sub_len), s + bias, NEG_BIG)
            m = jnp.max(s, axis=-1, keepdims=True)
            p = jnp.exp(s - m)
            den = jnp.sum(p, axis=-1, keepdims=True)
            o_ref[rows, sl] = jnp.dot((p / den).astype(BF16), v_ext[b * blk:b * blk + nk],
                                      preferred_element_type=F32)
            lse_ref[rows, sl] = jnp.broadcast_to(m + jnp.log(den), (blk, DIL_HEAD_DIM))


def dil_attn(qkv, group, *, blk=128, max_blocks=8):
    d, sub_len, _ = qkv.shape
    n_blk = sub_len // blk
    qb = min(max_blocks, n_blk)
    t = blk * qb
    slopes = _alibi_slopes()[group * DIL_HEADS_PER_GROUP:(group + 1) * DIL_HEADS_PER_GROUP]
    penalties = tuple(float(np.float32(x) * np.float32(d)) for x in slopes)

    def halo(which, step):
        return pl.BlockSpec((None, blk, BRANCH_W),
                            lambda r, i: (r, jnp.clip(i * qb + step, 0, n_blk - 1), which))

    def cur(which):
        return pl.BlockSpec((None, t, BRANCH_W), lambda r, i: (r, i, which))

    out_spec = pl.BlockSpec((None, t, BRANCH_W), lambda r, i: (r, i, 0))
    return pl.pallas_call(
        functools.partial(_dil_attn_body, sub_len=sub_len, penalties=penalties),
        grid=(d, sub_len // t),
        in_specs=[cur(0), halo(1, -1), cur(1), halo(1, qb), halo(2, -1), cur(2), halo(2, qb)],
        out_specs=[out_spec, out_spec],
        out_shape=[jax.ShapeDtypeStruct((d, sub_len, BRANCH_W), F32)] * 2,
        compiler_params=_params("parallel", "parallel"),
        name=f"dil_attn_g{group}",
    )(qkv, qkv, qkv, qkv, qkv, qkv, qkv)


def _dil_combine_body(o0, o1, o2, l0, l1, l2, z_ref, *scratch):
    tm = z_ref.shape[0]
    nat = {}
    for idx, (src, (_, d)) in enumerate(((o1, DIL_PATTERNS[1]), (o2, DIL_PATTERNS[2]),
                                         (l1, DIL_PATTERNS[1]), (l2, DIL_PATTERNS[2]))):
        scr = scratch[idx]
        for r in range(d):
            for a in range(DIL_HEADS_PER_GROUP):
                sl = slice(a * DIL_HEAD_DIM, (a + 1) * DIL_HEAD_DIM)
                scr[a, pl.ds(r, tm // d, stride=d), :] = src[r, :, sl]
        nat[idx] = scr
    for a in range(DIL_HEADS_PER_GROUP):
        sl = slice(a * DIL_HEAD_DIM, (a + 1) * DIL_HEAD_DIM)
        la, lb, lc = l0[0, :, sl], nat[2][a], nat[3][a]
        m = jnp.maximum(jnp.maximum(la, lb), lc)
        ea, eb, ec = jnp.exp(la - m), jnp.exp(lb - m), jnp.exp(lc - m)
        tot = ea + eb + ec
        z = (ea / tot) * o0[0, :, sl] + (eb / tot) * nat[0][a] + (ec / tot) * nat[1][a]
        z_ref[:, sl] = z.astype(BF16)


def dil_combine(outs, lses, *, tm=512):
    s = outs[0].shape[1]
    specs = [pl.BlockSpec((d, tm // d, BRANCH_W), lambda i: (0, i, 0)) for _, d in DIL_PATTERNS]
    return pl.pallas_call(
        _dil_combine_body,
        grid=(s // tm,),
        in_specs=specs + specs,
        out_specs=pl.BlockSpec((tm, BRANCH_W), lambda i: (i, 0)),
        out_shape=jax.ShapeDtypeStruct((s, BRANCH_W), BF16),
        scratch_shapes=[pltpu.VMEM((DIL_HEADS_PER_GROUP, tm, DIL_HEAD_DIM), F32)] * 4,
        compiler_params=_params("parallel"),
        name="dil_combine",
    )(*outs, *lses)


def _sgu_body(u_ref, v_ref, g_ref, ws_ref, b_ref, o_ref):
    tm = u_ref.shape[0]
    u = jax.nn.gelu(u_ref[...])
    vn = _rms(jax.nn.gelu(v_ref[...]), g_ref[...]).astype(BF16)
    for n in range(tm // SGU_CHUNK):
        rows = slice(n * SGU_CHUNK, (n + 1) * SGU_CHUNK)
        for g in range(SGU_GROUPS):
            cols = slice(g * LANE, (g + 1) * LANE)
            mixed = jnp.dot(ws_ref[g], vn[rows, cols], preferred_element_type=F32) + b_ref[g]
            o_ref[rows, cols] = (u[rows, cols] * mixed).astype(BF16)


def sgu_mixer(p_rest, v_norm_g, ws, b_full, layer, *, tm=512):
    s = p_rest.shape[0]
    return pl.pallas_call(
        _sgu_body,
        grid=(s // tm,),
        in_specs=[
            pl.BlockSpec((tm, BRANCH_W), lambda i: (i, REST_SGU // BRANCH_W)),
            pl.BlockSpec((tm, BRANCH_W), lambda i: (i, REST_SGU // BRANCH_W + 1)),
            pl.BlockSpec((1, BRANCH_W), lambda i: (0, 0)),
            pl.BlockSpec((None, SGU_GROUPS, SGU_CHUNK, SGU_CHUNK), lambda i: (layer, 0, 0, 0)),
            pl.BlockSpec((SGU_GROUPS, SGU_CHUNK, LANE), lambda i: (0, 0, 0)),
        ],
        out_specs=pl.BlockSpec((tm, BRANCH_W), lambda i: (i, 0)),
        out_shape=jax.ShapeDtypeStruct((s, BRANCH_W), BF16),
        compiler_params=_params("parallel"),
        name="sgu_mixer",
    )(p_rest, p_rest, v_norm_g.reshape(1, BRANCH_W), ws, b_full)


def _swap_rope_halves(y):
    width = y.shape[1]
    lane = lax.broadcasted_iota(jnp.int32, y.shape, 1)
    first_half = (lane % MLA_ROPE) < MLA_ROPE // 2
    return jnp.where(first_half, pltpu.roll(y, width - MLA_ROPE // 2, 1),
                     pltpu.roll(y, MLA_ROPE // 2, 1))


def _mla_prep_body(cq_ref, ckv_ref, kr_ref, cos_ref, sin_ref, qa_g, wuq_ref, kva_g, wukv_ref,
                   gqn_ref, gqr_ref, gkn_ref, gkr_ref, q_ref, k_ref, v_ref):
    nope_w = MLA_HEADS * MLA_NOPE
    qall = jnp.dot(_rms(cq_ref[...], qa_g[...]).astype(BF16), wuq_ref[...], preferred_element_type=F32)
    kv = jnp.dot(_rms(ckv_ref[...], kva_g[...]).astype(BF16), wukv_ref[...], preferred_element_type=F32)
    qn, qr = qall[:, :nope_w], qall[:, nope_w:]
    kn = kv[:, :nope_w]
    v_ref[...] = kv[:, nope_w:].astype(BF16)
    kr_blk = kr_ref[...]
    lane128 = lax.broadcasted_iota(jnp.int32, kr_blk.shape, 1)
    kr2 = jnp.where(lane128 < MLA_ROPE, kr_blk, pltpu.roll(kr_blk, MLA_ROPE, 1))
    kr = jnp.concatenate([kr2] * (MLA_HEADS // 2), axis=1)

    rope_w = MLA_HEADS * MLA_ROPE
    head_of_lane = lax.broadcasted_iota(jnp.int32, (1, rope_w), 1) // MLA_ROPE
    qr_sq = qr * qr
    kr_ssq = jnp.sum(jnp.where(head_of_lane == 0, kr * kr, 0.0), axis=-1, keepdims=True)
    rq, rk = [], []
    for h in range(MLA_HEADS):
        sl = slice(h * MLA_NOPE, (h + 1) * MLA_NOPE)
        q_ssq = (jnp.sum(qn[:, sl] * qn[:, sl], axis=-1, keepdims=True)
                 + jnp.sum(jnp.where(head_of_lane == h, qr_sq, 0.0), axis=-1, keepdims=True))
        k_ssq = jnp.sum(kn[:, sl] * kn[:, sl], axis=-1, keepdims=True) + kr_ssq
        rq.append(lax.rsqrt(q_ssq / MLA_QK + EPS))
        rk.append(lax.rsqrt(k_ssq / MLA_QK + EPS))

    def per_lane(rs):
        out = jnp.zeros((kr.shape[0], rope_w), F32)
        for h in range(MLA_HEADS):
            out = jnp.where(head_of_lane == h, rs[h], out)
        return out

    def rotate(x, rs, g_ref):
        y = x * per_lane(rs) * g_ref[...]
        return y * cos_ref[...] + _swap_rope_halves(y) * sin_ref[...]

    q_rope = rotate(qr, rq, gqr_ref) * MLA_EXP2_SCALE
    k_rope = rotate(kr, rk, gkr_ref)
    lane_tile = lax.broadcasted_iota(jnp.int32, (1, LANE), 1) // MLA_ROPE
    for h in range(MLA_HEADS):
        sl = slice(h * MLA_NOPE, (h + 1) * MLA_NOPE)
        base = h * MLA_PAD_QK
        tile = slice((h // 2) * LANE, (h // 2 + 1) * LANE)
        q_ref[:, base:base + MLA_NOPE] = (qn[:, sl] * rq[h] * gqn_ref[:, sl] * MLA_EXP2_SCALE).astype(BF16)
        k_ref[:, base:base + MLA_NOPE] = (kn[:, sl] * rk[h] * gkn_ref[:, sl]).astype(BF16)
        q_ref[:, base + MLA_NOPE:base + MLA_PAD_QK] = jnp.where(
            lane_tile == h % 2, q_rope[:, tile], 0.0).astype(BF16)
        k_ref[:, base + MLA_NOPE:base + MLA_PAD_QK] = k_rope[:, tile].astype(BF16)


def mla_prep(p_rest, cos4, sin4, qa_g, wuq, kva_g, wukv, gqn, gqr, gkn, gkr, *, tm=512):
    s = p_rest.shape[0]
    rope_w = MLA_HEADS * MLA_ROPE

    def const(shape):
        return pl.BlockSpec(shape, lambda i: (0,) * len(shape))

    return pl.pallas_call(
        _mla_prep_body,
        grid=(s // tm,),
        in_specs=[
            pl.BlockSpec((tm, MLA_Q_LORA), lambda i: (i, REST_CQ // MLA_Q_LORA)),
            pl.BlockSpec((tm, MLA_KV_LORA), lambda i: (i, REST_CKV // MLA_KV_LORA)),
            pl.BlockSpec((tm, LANE), lambda i: (i, REST_KR // LANE)),
            pl.BlockSpec((tm, rope_w), lambda i: (i, 0)),
            pl.BlockSpec((tm, rope_w), lambda i: (i, 0)),
            const((1, MLA_Q_LORA)), const(wuq.shape), const((1, MLA_KV_LORA)), const(wukv.shape),
            const(gqn.shape), const(gqr.shape), const(gkn.shape), const(gkr.shape),
        ],
        out_specs=[
            pl.BlockSpec((tm, MLA_HEADS * MLA_PAD_QK), lambda i: (i, 0)),
            pl.BlockSpec((tm, MLA_HEADS * MLA_PAD_QK), lambda i: (i, 0)),
            pl.BlockSpec((tm, MLA_HEADS * MLA_V), lambda i: (i, 0)),
        ],
        out_shape=[
            jax.ShapeDtypeStruct((s, MLA_HEADS * MLA_PAD_QK), BF16),
            jax.ShapeDtypeStruct((s, MLA_HEADS * MLA_PAD_QK), BF16),
            jax.ShapeDtypeStruct((s, MLA_HEADS * MLA_V), BF16),
        ],
        compiler_params=_params("parallel"),
        name="mla_prep",
    )(p_rest, p_rest, p_rest, cos4, sin4, qa_g.reshape(1, -1), wuq, kva_g.reshape(1, -1), wukv,
      gqn, gqr, gkn, gkr)


def _mla_attn_body(q_ref, k_ref, v_ref, o_ref, m_ref, l_ref, acc_ref):
    j = pl.program_id(2)

    @pl.when(j == 0)
    def _():
        m_ref[...] = jnp.full(m_ref.shape, NEG_BIG, F32)
        l_ref[...] = jnp.zeros(l_ref.shape, F32)
        acc_ref[...] = jnp.zeros(acc_ref.shape, F32)

    tk = k_ref.shape[0]
    for h in range(MLA_HEADS_PER_STEP):
        qk = slice(h * MLA_PAD_QK, (h + 1) * MLA_PAD_QK)
        vs = slice(h * MLA_V, (h + 1) * MLA_V)
        s = lax.dot_general(q_ref[:, qk], k_ref[:, qk], (((1,), (1,)), ((), ())),
                            preferred_element_type=F32)
        m_prev = m_ref[h]
        m_new = jnp.maximum(m_prev, jnp.max(s, axis=-1, keepdims=True))
        alpha = jnp.exp2(m_prev - m_new)
        p = jnp.exp2(s - jnp.tile(m_new, (1, tk // LANE)))
        l_ref[h] = alpha * l_ref[h] + jnp.sum(p, axis=-1, keepdims=True)
        acc_ref[h] = alpha * acc_ref[h] + jnp.dot(p.astype(BF16), v_ref[:, vs], preferred_element_type=F32)
        m_ref[h] = m_new

    @pl.when(j == pl.num_programs(2) - 1)
    def _():
        for h in range(MLA_HEADS_PER_STEP):
            o_ref[:, h * MLA_V:(h + 1) * MLA_V] = (acc_ref[h] / l_ref[h]).astype(BF16)


def mla_attn(q, k, v, *, tq=1024, tk=2048):
    s = q.shape[0]
    hp = MLA_HEADS_PER_STEP
    stat = pltpu.VMEM((hp, tq, LANE), F32)
    return pl.pallas_call(
        _mla_attn_body,
        grid=(s // tq, MLA_HEADS // hp, s // tk),
        in_specs=[
            pl.BlockSpec((tq, hp * MLA_PAD_QK), lambda i, h, j: (i, h)),
            pl.BlockSpec((tk, hp * MLA_PAD_QK), lambda i, h, j: (j, h)),
            pl.BlockSpec((tk, hp * MLA_V), lambda i, h, j: (j, h)),
        ],
        out_specs=pl.BlockSpec((tq, hp * MLA_V), lambda i, h, j: (i, h)),
        out_shape=jax.ShapeDtypeStruct((s, MLA_HEADS * MLA_V), BF16),
        scratch_shapes=[stat, stat, stat],
        compiler_params=_params("parallel", "parallel", "arbitrary"),
        name="mla_attn",
    )(q, k, v)


def _merge_body(h_ref, *refs):
    z_refs, wg_refs, wb_refs = refs[:N_BRANCH], refs[N_BRANCH:2 * N_BRANCH], refs[2 * N_BRANCH:3 * N_BRANCH]
    o_ref = refs[3 * N_BRANCH]
    h = h_ref[...]
    acc = None
    for z_ref, wg_ref, wb_ref in zip(z_refs, wg_refs, wb_refs):
        gate = jax.nn.sigmoid(_dot_nt(h, wg_ref[0]))
        term = gate * jnp.dot(z_ref[...], wb_ref[...], preferred_element_type=F32)
        acc = term if acc is None else acc + term
    o_ref[...] = acc.astype(BF16)


GATE_ROW_ALIGN = 64


def merge(h, zs, wt, w_branch, layer, *, tm=1024, tn=512):
    s, d = h.shape
    ct = d // tn
    assert OFF_GATE % GATE_ROW_ALIGN == 0 and d % GATE_ROW_ALIGN == 0 and tn % GATE_ROW_ALIGN == 0
    z_spec = pl.BlockSpec((tm, BRANCH_W), lambda i, c: (i, 0))
    g_specs = [pl.BlockSpec((pl.Element(1), pl.Element(tn), pl.Element(d)),
                            lambda i, c, n=n: (layer, pl.multiple_of(OFF_GATE + n * d + c * tn, GATE_ROW_ALIGN), 0))
               for n in range(N_BRANCH)]
    b_specs = [pl.BlockSpec((None, None, BRANCH_W, tn), lambda i, c, n=n: (layer, n, 0, c))
               for n in range(N_BRANCH)]
    return pl.pallas_call(
        _merge_body,
        grid=(s // tm, ct),
        in_specs=[pl.BlockSpec((tm, d), lambda i, c: (i, 0))] + [z_spec] * N_BRANCH + g_specs + b_specs,
        out_specs=pl.BlockSpec((tm, tn), lambda i, c: (i, c)),
        out_shape=jax.ShapeDtypeStruct((s, d), BF16),
        compiler_params=_params("parallel", "arbitrary"),
        name="merge",
    )(h, *zs, *([wt] * N_BRANCH), *([w_branch] * N_BRANCH))


def _out_proj_body(x_ref, y_ref, w_ref, o_ref):
    o_ref[...] = x_ref[...] + jnp.dot(y_ref[...], w_ref[...], preferred_element_type=F32)


def out_proj(x, y, w, layer, *, tm=1024, tn=1024):
    s, d = x.shape
    return pl.pallas_call(
        _out_proj_body,
        grid=(s // tm, d // tn),
        in_specs=[
            pl.BlockSpec((tm, tn), lambda i, j: (i, j)),
            pl.BlockSpec((tm, d), lambda i, j: (i, 0)),
            pl.BlockSpec((None, d, tn), lambda i, j: (layer, 0, j)),
        ],
        out_specs=pl.BlockSpec((tm, tn), lambda i, j: (i, j)),
        out_shape=jax.ShapeDtypeStruct((s, d), F32),
        compiler_params=_params("parallel", "arbitrary"),
        name="out_proj",
    )(x, y, w)


def _cast_body(x_ref, o_ref):
    o_ref[...] = x_ref[...].astype(BF16)


def cast_rows(wt, *, tr=1024):
    nl, n, d = wt.shape
    return pl.pallas_call(
        _cast_body,
        grid=(nl, pl.cdiv(n, tr)),
        in_specs=[pl.BlockSpec((None, tr, d), lambda l, r: (l, r, 0))],
        out_specs=pl.BlockSpec((None, tr, d), lambda l, r: (l, r, 0)),
        out_shape=jax.ShapeDtypeStruct((nl, n, d), BF16),
        compiler_params=_params("parallel", "parallel"),
        name="cast_rows",
    )(wt)


def _rope_tables(s):
    pos = jnp.arange(s, dtype=F32)
    inv_freq = ROPE_THETA ** (-jnp.arange(0, MLA_ROPE, 2, dtype=F32) / MLA_ROPE)
    ang = pos[:, None] * inv_freq[None, :]
    cos, sin = jnp.cos(ang), jnp.sin(ang)
    cos4 = jnp.tile(jnp.concatenate([cos, cos], axis=-1), (1, MLA_HEADS))
    sin4 = jnp.tile(jnp.concatenate([-sin, sin], axis=-1), (1, MLA_HEADS))
    return cos4, sin4


def _split_heads(w, widths):
    per_head = sum(widths)
    w3 = w.reshape(w.shape[0], MLA_HEADS, per_head)
    parts, off = [], 0
    for wd in widths:
        parts.append(w3[:, :, off:off + wd].reshape(w.shape[0], MLA_HEADS * wd))
        off += wd
    return jnp.concatenate(parts, axis=1)


def kernel(x, ffn1_norm, ffn1_w_gate, ffn1_w_up, ffn1_w_down, mix_norm, w_in, pool_w, pool_scale,
           dil_q_norm, dil_k_norm, sgu_v_norm, sgu_w, sgu_b, mla_q_a_norm, mla_w_uq, mla_kv_a_norm,
           mla_w_ukv, mla_q_norm, mla_k_norm, w_branch, w_out, ffn2_norm, ffn2_w_gate, ffn2_w_up,
           ffn2_w_down):
    b, s, d = x.shape
    cos4, sin4 = _rope_tables(s)
    w_main = cast_rows(jnp.swapaxes(w_in, 1, 2))
    w_br, w_o = w_branch.astype(BF16), w_out.astype(BF16)
    pool_wb, sgu_wb = pool_w.astype(BF16), sgu_w.astype(BF16)
    ones = jnp.ones((BRANCH_W,), F32)

    outs = []
    for bi in range(b):
        xb = x[bi]
        for l in range(DEPTH):
            xb = ffn(xb, ffn1_norm[l], ffn1_w_gate, ffn1_w_up, ffn1_w_down, l)

            p_rest, h = proj_rest(xb, mix_norm[l], w_main, l)
            dil_gains = jnp.stack([jnp.tile(dil_q_norm[l], DIL_HEADS_PER_GROUP),
                                   jnp.tile(dil_k_norm[l], DIL_HEADS_PER_GROUP), ones]).reshape(3, 1, BRANCH_W)
            qkvs = proj_dil(h, w_main, dil_gains, l)
            z_a = pool_mixer(p_rest, pool_wb, pool_scale[l], l)
            o_l = [dil_attn(qkvs[g], g) for g in range(DIL_GROUPS)]
            z_b = dil_combine([o for o, _ in o_l], [ls for _, ls in o_l])
            b_full = jnp.broadcast_to(sgu_b[l][:, :, None], (SGU_GROUPS, SGU_CHUNK, LANE))
            z_c = sgu_mixer(p_rest, sgu_v_norm[l], sgu_wb, b_full, l)
            q, k, v = mla_prep(
                p_rest, cos4, sin4, mla_q_a_norm[l],
                _split_heads(mla_w_uq[l], (MLA_NOPE, MLA_ROPE)).astype(BF16),
                mla_kv_a_norm[l], _split_heads(mla_w_ukv[l], (MLA_NOPE, MLA_V)).astype(BF16),
                jnp.tile(mla_q_norm[l][:MLA_NOPE], MLA_HEADS).reshape(1, -1),
                jnp.tile(mla_q_norm[l][MLA_NOPE:], MLA_HEADS).reshape(1, -1),
                jnp.tile(mla_k_norm[l][:MLA_NOPE], MLA_HEADS).reshape(1, -1),
                jnp.tile(mla_k_norm[l][MLA_NOPE:], MLA_HEADS).reshape(1, -1))
            z_d = mla_attn(q, k, v)
            merged = merge(h, (z_a, z_b, z_c, z_d), w_main, w_br, l)
            xb = out_proj(xb, merged, w_o, l)

            xb = ffn(xb, ffn2_norm[l], ffn2_w_gate, ffn2_w_up, ffn2_w_down, l)
        outs.append(xb)
    return jnp.stack(outs, axis=0)
```

```python
import functools
import math

import numpy as np
import jax
import jax.numpy as jnp
from jax import lax
from jax.experimental import pallas as pl
from jax.experimental.pallas import tpu as pltpu

F32 = jnp.float32
BF16 = jnp.bfloat16

D_MODEL = 2048
DEPTH = 4
D_FF = 5632
N_BRANCH = 4
BRANCH_W = 512
POOL_WINDOWS = (2, 4, 8, 16)
POOL_HALO = 8
DIL_PATTERNS = ((128, 1), (512, 4), (2048, 16))
DIL_GROUPS = len(DIL_PATTERNS)
DIL_HEADS = 12
DIL_HEADS_PER_GROUP = 4
DIL_HEAD_DIM = 128
DIL_QKV = 3 * DIL_HEADS * DIL_HEAD_DIM
DIL_SIDE = 64
SGU_CHUNK = 128
SGU_GROUPS = 4
MLA_HEADS = 4
MLA_Q_LORA = 384
MLA_KV_LORA = 128
MLA_NOPE = 128
MLA_ROPE = 64
MLA_V = 128
MLA_QK = MLA_NOPE + MLA_ROPE
MLA_PAD_QK = 256
MLA_HEADS_PER_STEP = 4
ROPE_THETA = 10000.0
EPS = 1e-6
NEG_BIG = -1e30
MLA_EXP2_SCALE = (MLA_QK ** -0.5) * math.log2(math.e)

OFF_POOL = 0
OFF_DIL = OFF_POOL + BRANCH_W
OFF_SGU = OFF_DIL + DIL_QKV
OFF_MLA_Q = OFF_SGU + 2 * BRANCH_W
OFF_MLA_KV = OFF_MLA_Q + MLA_Q_LORA
OFF_MLA_KR = OFF_MLA_KV + MLA_KV_LORA
OFF_GATE = OFF_MLA_KR + MLA_ROPE

LANE = 128
REST_POOL = 0
REST_SGU = 512
REST_CQ = 1536
REST_CKV = 1920
REST_KR = 2048
REST_W = REST_KR + LANE

VMEM_LIMIT = 56 * 1024 * 1024


def _params(*sem):
    return pltpu.CompilerParams(dimension_semantics=sem, vmem_limit_bytes=VMEM_LIMIT)


def _rms(xf, g):
    return xf * lax.rsqrt(jnp.mean(xf * xf, axis=-1, keepdims=True) + EPS) * g


def _dot_nt(a, b_t):
    return lax.dot_general(a, b_t, (((1,), (1,)), ((), ())), preferred_element_type=F32)


FFN_PREFETCH_STEP = 1


def _ffn_body(x_hbm, g_ref, wg_ref, wu_ref, wd_ref, o_hbm, acc_ref, h_ref, in_sem, out_sem):
    i, f = pl.program_id(0), pl.program_id(1)
    ni, nf = pl.num_programs(0), pl.num_programs(1)
    tm = acc_ref.shape[1]
    slot = lax.rem(i, 2)
    other = 1 - slot

    def x_copy(tile, buf):
        return pltpu.make_async_copy(x_hbm.at[pl.ds(tile * tm, tm), :], acc_ref.at[buf], in_sem.at[buf])

    def out_copy(tile, buf):
        return pltpu.make_async_copy(acc_ref.at[buf], o_hbm.at[pl.ds(tile * tm, tm), :], out_sem.at[buf])

    @pl.when(f == 0)
    def _():
        @pl.when(i == 0)
        def _():
            x_copy(i, slot).start()
        x_copy(i, slot).wait()
        h_ref[...] = _rms(acc_ref[slot], g_ref[...]).astype(BF16)

    @pl.when((f == FFN_PREFETCH_STEP) & (i + 1 < ni))
    def _():
        @pl.when(i > 0)
        def _():
            out_copy(i - 1, other).wait()
        x_copy(i + 1, other).start()

    h = h_ref[...]
    g = jnp.dot(h, wg_ref[...].astype(BF16), preferred_element_type=F32)
    u = jnp.dot(h, wu_ref[...].astype(BF16), preferred_element_type=F32)
    a = (0.5 * (jax.nn.silu(g) * u)).astype(BF16)
    acc_ref[slot] += jnp.dot(a, wd_ref[...].astype(BF16), preferred_element_type=F32)

    @pl.when(f == nf - 1)
    def _():
        out_copy(i, slot).start()

        @pl.when(i == ni - 1)
        def _():
            @pl.when(ni > 1)
            def _():
                out_copy(i - 1, other).wait()
            out_copy(i, slot).wait()


def ffn(x, norm_g, wg, wu, wd, layer, *, tm=1024, tf=512):
    s, d = x.shape
    f = wg.shape[2]
    assert f // tf > FFN_PREFETCH_STEP
    return pl.pallas_call(
        _ffn_body,
        grid=(s // tm, f // tf),
        in_specs=[
            pl.BlockSpec(memory_space=pl.ANY),
            pl.BlockSpec((1, d), lambda i, j: (0, 0)),
            pl.BlockSpec((None, d, tf), lambda i, j: (layer, 0, j)),
            pl.BlockSpec((None, d, tf), lambda i, j: (layer, 0, j)),
            pl.BlockSpec((None, tf, d), lambda i, j: (layer, j, 0)),
        ],
        out_specs=pl.BlockSpec(memory_space=pl.ANY),
        out_shape=jax.ShapeDtypeStruct((s, d), F32),
        scratch_shapes=[pltpu.VMEM((2, tm, d), F32), pltpu.VMEM((tm, d), BF16),
                        pltpu.SemaphoreType.DMA((2,)), pltpu.SemaphoreType.DMA((2,))],
        compiler_params=_params("arbitrary", "arbitrary"),
        name="ffn",
    )(x, norm_g.reshape(1, d), wg, wu, wd)


def _proj_rest_body(x0_ref, xn_ref, g_ref, wp_ref, ws_ref, wq_ref, wkv_ref, wkr_ref, p_ref, h_ref, hn_ref):
    i = pl.program_id(0)
    slot = lax.rem(i, 2)

    @pl.when(i == 0)
    def _():
        hn_ref[0] = _rms(x0_ref[...], g_ref[...]).astype(BF16)

    h = hn_ref[slot]
    h_ref[...] = h
    for w_ref, off in ((wp_ref, REST_POOL), (ws_ref, REST_SGU), (wq_ref, REST_CQ), (wkv_ref, REST_CKV),
                       (wkr_ref, REST_KR)):
        p_ref[:, off:off + w_ref.shape[0]] = _dot_nt(h, w_ref[...])
    hn_ref[1 - slot] = _rms(xn_ref[...], g_ref[...]).astype(BF16)


def proj_rest(x, norm_g, w_main, layer, *, tm=512):
    s, d = x.shape
    last = s // tm - 1

    def wspec(off, width):
        return pl.BlockSpec((None, width, d), lambda i: (layer, off // width, 0))

    return pl.pallas_call(
        _proj_rest_body,
        grid=(s // tm,),
        in_specs=[
            pl.BlockSpec((tm, d), lambda i: (0, 0)),
            pl.BlockSpec((tm, d), lambda i: (jnp.minimum(i + 1, last), 0)),
            pl.BlockSpec((1, d), lambda i: (0, 0)),
            wspec(OFF_POOL, BRANCH_W),
            wspec(OFF_SGU, 2 * BRANCH_W),
            wspec(OFF_MLA_Q, MLA_Q_LORA),
            wspec(OFF_MLA_KV, MLA_KV_LORA),
            wspec(OFF_MLA_KR, LANE),
        ],
        out_specs=[
            pl.BlockSpec((tm, REST_W), lambda i: (i, 0)),
            pl.BlockSpec((tm, d), lambda i: (i, 0)),
        ],
        out_shape=[jax.ShapeDtypeStruct((s, REST_W), F32), jax.ShapeDtypeStruct((s, d), BF16)],
        scratch_shapes=[pltpu.VMEM((2, tm, d), BF16)],
        compiler_params=_params("arbitrary"),
        name="proj_rest",
    )(x, x, norm_g.reshape(1, d), w_main, w_main, w_main, w_main, w_main)


def _proj_dil_body(h_ref, w0_ref, w1_ref, w2_ref, g_ref, o0_ref, o1_ref, o2_ref, scr1_ref, scr2_ref):
    normed = pl.program_id(1) < 2
    h = h_ref[...]
    tm = h.shape[0]
    for w_ref, o_ref, scr_ref, (_, d) in reversed(list(zip((w0_ref, w1_ref, w2_ref), (o0_ref, o1_ref, o2_ref),
                                                           (None, scr1_ref, scr2_ref), DIL_PATTERNS))):
        acc = _dot_nt(h, w_ref[...])
        for a in range(DIL_HEADS_PER_GROUP):
            sl = slice(a * DIL_HEAD_DIM, (a + 1) * DIL_HEAD_DIM)
            xh = acc[:, sl]
            r = lax.rsqrt(jnp.mean(xh * xh, axis=-1, keepdims=True) + EPS)
            y = xh * jnp.where(normed, r, 1.0) * g_ref[:, sl]
            if d == 1:
                o_ref[0, :, sl] = y.astype(BF16)
            else:
                scr_ref[a] = y
        if d > 1:
            for r in range(d):
                for a in range(DIL_HEADS_PER_GROUP):
                    sl = slice(a * DIL_HEAD_DIM, (a + 1) * DIL_HEAD_DIM)
                    o_ref[r, :, sl] = scr_ref[a, pl.ds(r, tm // d, stride=d), :].astype(BF16)


def proj_dil(h, w_main, gains, layer, *, tm=1024):
    s, dm = h.shape
    first = OFF_DIL // BRANCH_W

    def wspec(g):
        return pl.BlockSpec((None, BRANCH_W, dm), lambda i, j: (layer, first + j * DIL_GROUPS + g, 0))

    out_specs, out_shapes, scratch = [], [], []
    for _, d in DIL_PATTERNS:
        out_specs.append(pl.BlockSpec((d, tm // d, BRANCH_W), lambda i, j: (0, i, j)))
        out_shapes.append(jax.ShapeDtypeStruct((d, s // d, 3 * BRANCH_W), BF16))
        if d > 1:
            scratch.append(pltpu.VMEM((DIL_HEADS_PER_GROUP, tm, DIL_HEAD_DIM), F32))
    return pl.pallas_call(
        _proj_dil_body,
        grid=(s // tm, 3),
        in_specs=[
            pl.BlockSpec((tm, dm), lambda i, j: (i, 0)),
            wspec(0), wspec(1), wspec(2),
            pl.BlockSpec((None, 1, BRANCH_W), lambda i, j: (j, 0, 0)),
        ],
        out_specs=out_specs,
        out_shape=out_shapes,
        scratch_shapes=scratch,
        compiler_params=_params("parallel", "arbitrary"),
        name="proj_dil",
    )(h, w_main, w_main, w_main, gains)


def _pool_body(prev_ref, cur_ref, next_ref, w_ref, scale_ref, o_ref, ext_ref, *, seq):
    i = pl.program_id(0)
    tm = cur_ref.shape[0]
    cur = cur_ref[...]
    ext_ref[0:POOL_HALO, :] = jnp.where(i > 0, prev_ref[...], 0.0)
    ext_ref[POOL_HALO:POOL_HALO + tm, :] = cur
    ext_ref[POOL_HALO + tm:2 * POOL_HALO + tm, :] = jnp.where(
        i < pl.num_programs(0) - 1, next_ref[...], 0.0)
    t = i * tm + lax.broadcasted_iota(jnp.int32, (tm, 1), 0)
    for g, w in enumerate(POOL_WINDOWS):
        sl = slice(g * LANE, (g + 1) * LANE)
        tot = jnp.zeros((tm, LANE), F32)
        for o in range(-(w // 2), w // 2):
            tot = tot + ext_ref[POOL_HALO + o:POOL_HALO + o + tm, sl]
        cnt = (jnp.minimum(t + w // 2, seq) - jnp.maximum(t - w // 2, 0)).astype(F32)
        dev = (tot / cnt - cur[:, sl]).astype(BF16)
        y = jnp.dot(dev, w_ref[g], preferred_element_type=F32) * scale_ref[:, sl]
        o_ref[:, sl] = y.astype(BF16)


def pool_mixer(p_rest, pool_w, pool_scale, layer, *, tm=512):
    s = p_rest.shape[0]
    hb = tm // POOL_HALO
    last = s // POOL_HALO - 1
    n_win = len(POOL_WINDOWS)
    return pl.pallas_call(
        functools.partial(_pool_body, seq=s),
        grid=(s // tm,),
        in_specs=[
            pl.BlockSpec((POOL_HALO, BRANCH_W), lambda i: (jnp.maximum(i * hb - 1, 0), 0)),
            pl.BlockSpec((tm, BRANCH_W), lambda i: (i, 0)),
            pl.BlockSpec((POOL_HALO, BRANCH_W), lambda i: (jnp.minimum((i + 1) * hb, last), 0)),
            pl.BlockSpec((None, n_win, LANE, LANE), lambda i: (layer, 0, 0, 0)),
            pl.BlockSpec((1, BRANCH_W), lambda i: (0, 0)),
        ],
        out_specs=pl.BlockSpec((tm, BRANCH_W), lambda i: (i, 0)),
        out_shape=jax.ShapeDtypeStruct((s, BRANCH_W), BF16),
        scratch_shapes=[pltpu.VMEM((tm + 2 * POOL_HALO, BRANCH_W), F32)],
        compiler_params=_params("parallel"),
        name="pool_mixer",
    )(p_rest, p_rest, p_rest, pool_w, pool_scale.reshape(1, BRANCH_W))


def _alibi_slopes():
    n = DIL_HEADS
    return np.exp2(np.float32(-8.0) * np.arange(1, n + 1, dtype=np.float32) / np.float32(n))


LSE_LANES = LANE // DIL_HEADS_PER_GROUP


def _dil_attn_body(cur_ref, prev_ref, next_ref, o_ref, lse_ref, *, sub_len, penalties):
    i = pl.program_id(1)
    t = cur_ref.shape[0]
    blk = prev_ref.shape[0]
    nk = blk + 2 * DIL_SIDE
    row = lax.broadcasted_iota(jnp.int32, (blk, nk), 0)
    col = lax.broadcasted_iota(jnp.int32, (blk, nk), 1)
    dist = jnp.abs(col - DIL_SIDE - row)
    in_band = dist <= DIL_SIDE
    distf = dist.astype(F32)
    lane_head = lax.broadcasted_iota(jnp.int32, (1, LANE), 1) // LSE_LANES
    lse_tiles = [jnp.zeros((blk, LANE), F32) for _ in range(t // blk)]
    for h in range(DIL_HEADS_PER_GROUP):
        ql = slice(h * DIL_HEAD_DIM, (h + 1) * DIL_HEAD_DIM)
        kl = slice(BRANCH_W + h * DIL_HEAD_DIM, BRANCH_W + (h + 1) * DIL_HEAD_DIM)
        vl = slice(2 * BRANCH_W + h * DIL_HEAD_DIM, 2 * BRANCH_W + (h + 1) * DIL_HEAD_DIM)
        k_ext = jnp.concatenate([prev_ref[blk - DIL_SIDE:, kl], cur_ref[:, kl], next_ref[:DIL_SIDE, kl]], axis=0)
        v_ext = jnp.concatenate([prev_ref[blk - DIL_SIDE:, vl], cur_ref[:, vl], next_ref[:DIL_SIDE, vl]], axis=0)
        bias = jnp.where(in_band, -penalties[h] * distf, NEG_BIG)
        for b in range(t // blk):
            rows = slice(b * blk, (b + 1) * blk)
            kpos = i * t + b * blk + col - DIL_SIDE
            s = lax.dot_general(cur_ref[rows, ql], k_ext[b * blk:b * blk + nk], (((1,), (1,)), ((), ())),
                                preferred_element_type=F32) * (DIL_HEAD_DIM ** -0.5)
            s = jnp.where((kpos >= 0) & (kpos < sub_len), s + bias, NEG_BIG)
            m = jnp.max(s, axis=-1, keepdims=True)
            p = jnp.exp(s - m)
            den = jnp.sum(p, axis=-1, keepdims=True)
            o_ref[rows, ql] = jnp.dot((p / den).astype(BF16), v_ext[b * blk:b * blk + nk],
                                      preferred_element_type=F32).astype(BF16)
            lse_tiles[b] = jnp.where(lane_head == h, m + jnp.log(den), lse_tiles[b])
    for b, tile in enumerate(lse_tiles):
        lse_ref[b * blk:(b + 1) * blk, :] = tile


def dil_attn(qkv, group, *, blk=128, max_blocks=8):
    d, sub_len, width = qkv.shape
    n_blk = sub_len // blk
    qb = min(max_blocks, n_blk)
    t = blk * qb
    slopes = _alibi_slopes()[group * DIL_HEADS_PER_GROUP:(group + 1) * DIL_HEADS_PER_GROUP]
    penalties = tuple(float(np.float32(x) * np.float32(d)) for x in slopes)

    def halo(step):
        return pl.BlockSpec((None, blk, width), lambda r, i: (r, jnp.clip(i * qb + step, 0, n_blk - 1), 0))

    return pl.pallas_call(
        functools.partial(_dil_attn_body, sub_len=sub_len, penalties=penalties),
        grid=(d, sub_len // t),
        in_specs=[pl.BlockSpec((None, t, width), lambda r, i: (r, i, 0)), halo(-1), halo(qb)],
        out_specs=[pl.BlockSpec((None, t, BRANCH_W), lambda r, i: (r, i, 0)),
                   pl.BlockSpec((None, t, LANE), lambda r, i: (r, i, 0))],
        out_shape=[jax.ShapeDtypeStruct((d, sub_len, BRANCH_W), BF16),
                   jax.ShapeDtypeStruct((d, sub_len, LANE), F32)],
        compiler_params=_params("parallel", "parallel"),
        name=f"dil_attn_g{group}",
    )(qkv, qkv, qkv)


def _dil_combine_body(o0, o1, o2, l0, l1, l2, z_ref, so1, so2, sl1, sl2):
    tm = z_ref.shape[0]
    for src, scr, (_, d) in ((o1, so1, DIL_PATTERNS[1]), (o2, so2, DIL_PATTERNS[2])):
        for r in range(d):
            for a in range(DIL_HEADS_PER_GROUP):
                sl = slice(a * DIL_HEAD_DIM, (a + 1) * DIL_HEAD_DIM)
                scr[a, pl.ds(r, tm // d, stride=d), :] = src[r, :, sl].astype(F32)
    for src, scr, (_, d) in ((l1, sl1, DIL_PATTERNS[1]), (l2, sl2, DIL_PATTERNS[2])):
        for r in range(d):
            scr[pl.ds(r, tm // d, stride=d), :] = src[r]
    la, lb, lc = l0[0], sl1[...], sl2[...]
    m = jnp.maximum(jnp.maximum(la, lb), lc)
    ea, eb, ec = jnp.exp(la - m), jnp.exp(lb - m), jnp.exp(lc - m)
    tot = ea + eb + ec
    wa, wb, wc = ea / tot, eb / tot, ec / tot
    for a in range(DIL_HEADS_PER_GROUP):
        sl = slice(a * DIL_HEAD_DIM, (a + 1) * DIL_HEAD_DIM)
        one = slice(a * LSE_LANES, a * LSE_LANES + 1)
        shape = (tm, DIL_HEAD_DIM)
        z = (jnp.broadcast_to(wa[:, one], shape) * o0[0, :, sl].astype(F32)
             + jnp.broadcast_to(wb[:, one], shape) * so1[a]
             + jnp.broadcast_to(wc[:, one], shape) * so2[a])
        z_ref[:, sl] = z.astype(BF16)


def dil_combine(outs, lses, *, tm=512):
    s = outs[0].shape[1]
    o_specs = [pl.BlockSpec((d, tm // d, BRANCH_W), lambda i: (0, i, 0)) for _, d in DIL_PATTERNS]
    l_specs = [pl.BlockSpec((d, tm // d, LANE), lambda i: (0, i, 0)) for _, d in DIL_PATTERNS]
    slab = pltpu.VMEM((DIL_HEADS_PER_GROUP, tm, DIL_HEAD_DIM), F32)
    return pl.pallas_call(
        _dil_combine_body,
        grid=(s // tm,),
        in_specs=o_specs + l_specs,
        out_specs=pl.BlockSpec((tm, BRANCH_W), lambda i: (i, 0)),
        out_shape=jax.ShapeDtypeStruct((s, BRANCH_W), BF16),
        scratch_shapes=[slab, slab, pltpu.VMEM((tm, LANE), F32), pltpu.VMEM((tm, LANE), F32)],
        compiler_params=_params("parallel"),
        name="dil_combine",
    )(*outs, *lses)


def _sgu_body(u_ref, v_ref, g_ref, ws_ref, b_ref, o_ref):
    tm = u_ref.shape[0]
    u = jax.nn.gelu(u_ref[...])
    vn = _rms(jax.nn.gelu(v_ref[...]), g_ref[...]).astype(BF16)
    for n in range(tm // SGU_CHUNK):
        rows = slice(n * SGU_CHUNK, (n + 1) * SGU_CHUNK)
        for g in range(SGU_GROUPS):
            cols = slice(g * LANE, (g + 1) * LANE)
            mixed = jnp.dot(ws_ref[g], vn[rows, cols], preferred_element_type=F32) + b_ref[g]
            o_ref[rows, cols] = (u[rows, cols] * mixed).astype(BF16)


def sgu_mixer(p_rest, v_norm_g, ws, b_full, layer, *, tm=512):
    s = p_rest.shape[0]
    return pl.pallas_call(
        _sgu_body,
        grid=(s // tm,),
        in_specs=[
            pl.BlockSpec((tm, BRANCH_W), lambda i: (i, REST_SGU // BRANCH_W)),
            pl.BlockSpec((tm, BRANCH_W), lambda i: (i, REST_SGU // BRANCH_W + 1)),
            pl.BlockSpec((1, BRANCH_W), lambda i: (0, 0)),
            pl.BlockSpec((None, SGU_GROUPS, SGU_CHUNK, SGU_CHUNK), lambda i: (layer, 0, 0, 0)),
            pl.BlockSpec((SGU_GROUPS, SGU_CHUNK, LANE), lambda i: (0, 0, 0)),
        ],
        out_specs=pl.BlockSpec((tm, BRANCH_W), lambda i: (i, 0)),
        out_shape=jax.ShapeDtypeStruct((s, BRANCH_W), BF16),
        compiler_params=_params("parallel"),
        name="sgu_mixer",
    )(p_rest, p_rest, v_norm_g.reshape(1, BRANCH_W), ws, b_full)


def _swap_rope_halves(y):
    width = y.shape[1]
    lane = lax.broadcasted_iota(jnp.int32, y.shape, 1)
    first_half = (lane % MLA_ROPE) < MLA_ROPE // 2
    return jnp.where(first_half, pltpu.roll(y, width - MLA_ROPE // 2, 1),
                     pltpu.roll(y, MLA_ROPE // 2, 1))


def _mla_prep_body(cq_ref, ckv_ref, kr_ref, cos_ref, sin_ref, qa_g, wuq_ref, kva_g, wukv_ref,
                   gqn_ref, gqr_ref, gkn_ref, gkr_ref, q_ref, k_ref, v_ref):
    nope_w = MLA_HEADS * MLA_NOPE
    qall = jnp.dot(_rms(cq_ref[...], qa_g[...]).astype(BF16), wuq_ref[...], preferred_element_type=F32)
    kv = jnp.dot(_rms(ckv_ref[...], kva_g[...]).astype(BF16), wukv_ref[...], preferred_element_type=F32)
    qn, qr = qall[:, :nope_w], qall[:, nope_w:]
    kn = kv[:, :nope_w]
    v_ref[...] = kv[:, nope_w:].astype(BF16)
    kr_blk = kr_ref[...]
    lane128 = lax.broadcasted_iota(jnp.int32, kr_blk.shape, 1)
    kr2 = jnp.where(lane128 < MLA_ROPE, kr_blk, pltpu.roll(kr_blk, MLA_ROPE, 1))
    kr = jnp.concatenate([kr2] * (MLA_HEADS // 2), axis=1)

    rope_w = MLA_HEADS * MLA_ROPE
    head_of_lane = lax.broadcasted_iota(jnp.int32, (1, rope_w), 1) // MLA_ROPE
    qr_sq = qr * qr
    kr_ssq = jnp.sum(jnp.where(head_of_lane == 0, kr * kr, 0.0), axis=-1, keepdims=True)
    rq, rk = [], []
    for h in range(MLA_HEADS):
        sl = slice(h * MLA_NOPE, (h + 1) * MLA_NOPE)
        q_ssq = (jnp.sum(qn[:, sl] * qn[:, sl], axis=-1, keepdims=True)
                 + jnp.sum(jnp.where(head_of_lane == h, qr_sq, 0.0), axis=-1, keepdims=True))
        k_ssq = jnp.sum(kn[:, sl] * kn[:, sl], axis=-1, keepdims=True) + kr_ssq
        rq.append(lax.rsqrt(q_ssq / MLA_QK + EPS))
        rk.append(lax.rsqrt(k_ssq / MLA_QK + EPS))

    def per_lane(rs):
        out = jnp.zeros((kr.shape[0], rope_w), F32)
        for h in range(MLA_HEADS):
            out = jnp.where(head_of_lane == h, rs[h], out)
        return out

    def rotate(x, rs, g_ref):
        y = x * per_lane(rs) * g_ref[...]
        return y * cos_ref[...] + _swap_rope_halves(y) * sin_ref[...]

    q_rope = rotate(qr, rq, gqr_ref) * MLA_EXP2_SCALE
    k_rope = rotate(kr, rk, gkr_ref)
    lane_tile = lax.broadcasted_iota(jnp.int32, (1, LANE), 1) // MLA_ROPE
    for h in range(MLA_HEADS):
        sl = slice(h * MLA_NOPE, (h + 1) * MLA_NOPE)
        base = h * MLA_PAD_QK
        tile = slice((h // 2) * LANE, (h // 2 + 1) * LANE)
        q_ref[:, base:base + MLA_NOPE] = (qn[:, sl] * rq[h] * gqn_ref[:, sl] * MLA_EXP2_SCALE).astype(BF16)
        k_ref[:, base:base + MLA_NOPE] = (kn[:, sl] * rk[h] * gkn_ref[:, sl]).astype(BF16)
        q_ref[:, base + MLA_NOPE:base + MLA_PAD_QK] = jnp.where(
            lane_tile == h % 2, q_rope[:, tile], 0.0).astype(BF16)
        k_ref[:, base + MLA_NOPE:base + MLA_PAD_QK] = k_rope[:, tile].astype(BF16)


def mla_prep(p_rest, cos4, sin4, qa_g, wuq, kva_g, wukv, gqn, gqr, gkn, gkr, *, tm=512):
    s = p_rest.shape[0]
    rope_w = MLA_HEADS * MLA_ROPE

    def const(shape):
        return pl.BlockSpec(shape, lambda i: (0,) * len(shape))

    return pl.pallas_call(
        _mla_prep_body,
        grid=(s // tm,),
        in_specs=[
            pl.BlockSpec((tm, MLA_Q_LORA), lambda i: (i, REST_CQ // MLA_Q_LORA)),
            pl.BlockSpec((tm, MLA_KV_LORA), lambda i: (i, REST_CKV // MLA_KV_LORA)),
            pl.BlockSpec((tm, LANE), lambda i: (i, REST_KR // LANE)),
            pl.BlockSpec((tm, rope_w), lambda i: (i, 0)),
            pl.BlockSpec((tm, rope_w), lambda i: (i, 0)),
            const((1, MLA_Q_LORA)), const(wuq.shape), const((1, MLA_KV_LORA)), const(wukv.shape),
            const(gqn.shape), const(gqr.shape), const(gkn.shape), const(gkr.shape),
        ],
        out_specs=[
            pl.BlockSpec((tm, MLA_HEADS * MLA_PAD_QK), lambda i: (i, 0)),
            pl.BlockSpec((tm, MLA_HEADS * MLA_PAD_QK), lambda i: (i, 0)),
            pl.BlockSpec((tm, MLA_HEADS * MLA_V), lambda i: (i, 0)),
        ],
        out_shape=[
            jax.ShapeDtypeStruct((s, MLA_HEADS * MLA_PAD_QK), BF16),
            jax.ShapeDtypeStruct((s, MLA_HEADS * MLA_PAD_QK), BF16),
            jax.ShapeDtypeStruct((s, MLA_HEADS * MLA_V), BF16),
        ],
        compiler_params=_params("parallel"),
        name="mla_prep",
    )(p_rest, p_rest, p_rest, cos4, sin4, qa_g.reshape(1, -1), wuq, kva_g.reshape(1, -1), wukv,
      gqn, gqr, gkn, gkr)


def _mla_attn_body(q_ref, k_ref, v_ref, o_ref, m_ref, l_ref, acc_ref):
    j = pl.program_id(2)

    @pl.when(j == 0)
    def _():
        m_ref[...] = jnp.full(m_ref.shape, NEG_BIG, F32)
        l_ref[...] = jnp.zeros(l_ref.shape, F32)
        acc_ref[...] = jnp.zeros(acc_ref.shape, F32)

    tk = k_ref.shape[0]
    for h in range(MLA_HEADS_PER_STEP):
        qk = slice(h * MLA_PAD_QK, (h + 1) * MLA_PAD_QK)
        vs = slice(h * MLA_V, (h + 1) * MLA_V)
        s = lax.dot_general(q_ref[:, qk], k_ref[:, qk], (((1,), (1,)), ((), ())),
                            preferred_element_type=F32)
        m_prev = m_ref[h]
        m_new = jnp.maximum(m_prev, jnp.max(s, axis=-1, keepdims=True))
        alpha = jnp.exp2(m_prev - m_new)
        p = jnp.exp2(s - jnp.tile(m_new, (1, tk // LANE)))
        l_ref[h] = alpha * l_ref[h] + jnp.sum(p, axis=-1, keepdims=True)
        acc_ref[h] = alpha * acc_ref[h] + jnp.dot(p.astype(BF16), v_ref[:, vs], preferred_element_type=F32)
        m_ref[h] = m_new

    @pl.when(j == pl.num_programs(2) - 1)
    def _():
        for h in range(MLA_HEADS_PER_STEP):
            o_ref[:, h * MLA_V:(h + 1) * MLA_V] = (acc_ref[h] / l_ref[h]).astype(BF16)


def mla_attn(q, k, v, *, tq=1024, tk=2048):
    s = q.shape[0]
    hp = MLA_HEADS_PER_STEP
    stat = pltpu.VMEM((hp, tq, LANE), F32)
    return pl.pallas_call(
        _mla_attn_body,
        grid=(s // tq, MLA_HEADS // hp, s // tk),
        in_specs=[
            pl.BlockSpec((tq, hp * MLA_PAD_QK), lambda i, h, j: (i, h)),
            pl.BlockSpec((tk, hp * MLA_PAD_QK), lambda i, h, j: (j, h)),
            pl.BlockSpec((tk, hp * MLA_V), lambda i, h, j: (j, h)),
        ],
        out_specs=pl.BlockSpec((tq, hp * MLA_V), lambda i, h, j: (i, h)),
        out_shape=jax.ShapeDtypeStruct((s, MLA_HEADS * MLA_V), BF16),
        scratch_shapes=[stat, stat, stat],
        compiler_params=_params("parallel", "parallel", "arbitrary"),
        name="mla_attn",
    )(q, k, v)


def _merge_body(h_ref, *refs):
    z_refs, wg_refs, wb_refs = refs[:N_BRANCH], refs[N_BRANCH:2 * N_BRANCH], refs[2 * N_BRANCH:3 * N_BRANCH]
    o_ref = refs[3 * N_BRANCH]
    h = h_ref[...]
    acc = None
    for z_ref, wg_ref, wb_ref in zip(z_refs, wg_refs, wb_refs):
        gate = jax.nn.sigmoid(_dot_nt(h, wg_ref[0]))
        term = gate * jnp.dot(z_ref[...], wb_ref[...], preferred_element_type=F32)
        acc = term if acc is None else acc + term
    o_ref[...] = acc.astype(BF16)


GATE_ROW_ALIGN = 64


def merge(h, zs, wt, w_branch, layer, *, tm=1024, tn=512):
    s, d = h.shape
    ct = d // tn
    assert OFF_GATE % GATE_ROW_ALIGN == 0 and d % GATE_ROW_ALIGN == 0 and tn % GATE_ROW_ALIGN == 0
    z_spec = pl.BlockSpec((tm, BRANCH_W), lambda i, c: (i, 0))
    g_specs = [pl.BlockSpec((pl.Element(1), pl.Element(tn), pl.Element(d)),
                            lambda i, c, n=n: (layer, pl.multiple_of(OFF_GATE + n * d + c * tn, GATE_ROW_ALIGN), 0))
               for n in range(N_BRANCH)]
    b_specs = [pl.BlockSpec((None, None, BRANCH_W, tn), lambda i, c, n=n: (layer, n, 0, c))
               for n in range(N_BRANCH)]
    return pl.pallas_call(
        _merge_body,
        grid=(s // tm, ct),
        in_specs=[pl.BlockSpec((tm, d), lambda i, c: (i, 0))] + [z_spec] * N_BRANCH + g_specs + b_specs,
        out_specs=pl.BlockSpec((tm, tn), lambda i, c: (i, c)),
        out_shape=jax.ShapeDtypeStruct((s, d), BF16),
        compiler_params=_params("parallel", "arbitrary"),
        name="merge",
    )(h, *zs, *([wt] * N_BRANCH), *([w_branch] * N_BRANCH))


def _out_proj_body(x_ref, y_ref, w_ref, o_ref):
    o_ref[...] = x_ref[...] + jnp.dot(y_ref[...], w_ref[...], preferred_element_type=F32)


def out_proj(x, y, w, layer, *, tm=1024, tn=1024):
    s, d = x.shape
    return pl.pallas_call(
        _out_proj_body,
        grid=(s // tm, d // tn),
        in_specs=[
            pl.BlockSpec((tm, tn), lambda i, j: (i, j)),
            pl.BlockSpec((tm, d), lambda i, j: (i, 0)),
            pl.BlockSpec((None, d, tn), lambda i, j: (layer, 0, j)),
        ],
        out_specs=pl.BlockSpec((tm, tn), lambda i, j: (i, j)),
        out_shape=jax.ShapeDtypeStruct((s, d), F32),
        compiler_params=_params("parallel", "arbitrary"),
        name="out_proj",
    )(x, y, w)


def _cast_body(x_ref, o_ref):
    o_ref[...] = x_ref[...].astype(BF16)


def cast_rows(wt, *, tr=1024):
    nl, n, d = wt.shape
    return pl.pallas_call(
        _cast_body,
        grid=(nl, pl.cdiv(n, tr)),
        in_specs=[pl.BlockSpec((None, tr, d), lambda l, r: (l, r, 0))],
        out_specs=pl.BlockSpec((None, tr, d), lambda l, r: (l, r, 0)),
        out_shape=jax.ShapeDtypeStruct((nl, n, d), BF16),
        compiler_params=_params("parallel", "parallel"),
        name="cast_rows",
    )(wt)


def _rope_tables(s):
    pos = jnp.arange(s, dtype=F32)
    inv_freq = ROPE_THETA ** (-jnp.arange(0, MLA_ROPE, 2, dtype=F32) / MLA_ROPE)
    ang = pos[:, None] * inv_freq[None, :]
    cos, sin = jnp.cos(ang), jnp.sin(ang)
    cos4 = jnp.tile(jnp.concatenate([cos, cos], axis=-1), (1, MLA_HEADS))
    sin4 = jnp.tile(jnp.concatenate([-sin, sin], axis=-1), (1, MLA_HEADS))
    return cos4, sin4


def _split_heads(w, widths):
    per_head = sum(widths)
    w3 = w.reshape(w.shape[0], MLA_HEADS, per_head)
    parts, off = [], 0
    for wd in widths:
        parts.append(w3[:, :, off:off + wd].reshape(w.shape[0], MLA_HEADS * wd))
        off += wd
    return jnp.concatenate(parts, axis=1)


def kernel(x, ffn1_norm, ffn1_w_gate, ffn1_w_up, ffn1_w_down, mix_norm, w_in, pool_w, pool_scale,
           dil_q_norm, dil_k_norm, sgu_v_norm, sgu_w, sgu_b, mla_q_a_norm, mla_w_uq, mla_kv_a_norm,
           mla_w_ukv, mla_q_norm, mla_k_norm, w_branch, w_out, ffn2_norm, ffn2_w_gate, ffn2_w_up,
           ffn2_w_down):
    b, s, d = x.shape
    cos4, sin4 = _rope_tables(s)
    w_main = cast_rows(jnp.swapaxes(w_in, 1, 2))
    w_br, w_o = w_branch.astype(BF16), w_out.astype(BF16)
    pool_wb, sgu_wb = pool_w.astype(BF16), sgu_w.astype(BF16)
    ones = jnp.ones((BRANCH_W,), F32)

    outs = []
    for bi in range(b):
        xb = x[bi]
        for l in range(DEPTH):
            xb = ffn(xb, ffn1_norm[l], ffn1_w_gate, ffn1_w_up, ffn1_w_down, l)

            p_rest, h = proj_rest(xb, mix_norm[l], w_main, l)
            dil_gains = jnp.stack([jnp.tile(dil_q_norm[l], DIL_HEADS_PER_GROUP),
                                   jnp.tile(dil_k_norm[l], DIL_HEADS_PER_GROUP), ones]).reshape(3, 1, BRANCH_W)
            qkvs = proj_dil(h, w_main, dil_gains, l)
            z_a = pool_mixer(p_rest, pool_wb, pool_scale[l], l)
            o_l = [dil_attn(qkvs[g], g) for g in range(DIL_GROUPS)]
            z_b = dil_combine([o for o, _ in o_l], [ls for _, ls in o_l])
            b_full = jnp.broadcast_to(sgu_b[l][:, :, None], (SGU_GROUPS, SGU_CHUNK, LANE))
            z_c = sgu_mixer(p_rest, sgu_v_norm[l], sgu_wb, b_full, l)
            q, k, v = mla_prep(
                p_rest, cos4, sin4, mla_q_a_norm[l],
                _split_heads(mla_w_uq[l], (MLA_NOPE, MLA_ROPE)).astype(BF16),
                mla_kv_a_norm[l], _split_heads(mla_w_ukv[l], (MLA_NOPE, MLA_V)).astype(BF16),
                jnp.tile(mla_q_norm[l][:MLA_NOPE], MLA_HEADS).reshape(1, -1),
                jnp.tile(mla_q_norm[l][MLA_NOPE:], MLA_HEADS).reshape(1, -1),
                jnp.tile(mla_k_norm[l][:MLA_NOPE], MLA_HEADS).reshape(1, -1),
                jnp.tile(mla_k_norm[l][MLA_NOPE:], MLA_HEADS).reshape(1, -1))
            z_d = mla_attn(q, k, v)
            merged = merge(h, (z_a, z_b, z_c, z_d), w_main, w_br, l)
            xb = out_proj(xb, merged, w_o, l)

            xb = ffn(xb, ffn2_norm[l], ffn2_w_gate, ffn2_w_up, ffn2_w_down, l)
        outs.append(xb)
    return jnp.stack(outs, axis=0)
```

```python
import functools
import math

import numpy as np
import jax
import jax.numpy as jnp
from jax import lax
from jax.experimental import pallas as pl
from jax.experimental.pallas import tpu as pltpu

F32 = jnp.float32
BF16 = jnp.bfloat16

D_MODEL = 2048
DEPTH = 4
D_FF = 5632
N_BRANCH = 4
BRANCH_W = 512
POOL_WINDOWS = (2, 4, 8, 16)
POOL_HALO = 8
DIL_PATTERNS = ((128, 1), (512, 4), (2048, 16))
DIL_GROUPS = len(DIL_PATTERNS)
DIL_HEADS = 12
DIL_HEADS_PER_GROUP = 4
DIL_HEAD_DIM = 128
DIL_QKV = 3 * DIL_HEADS * DIL_HEAD_DIM
DIL_SIDE = 64
SGU_CHUNK = 128
SGU_GROUPS = 4
MLA_HEADS = 4
MLA_Q_LORA = 384
MLA_KV_LORA = 128
MLA_NOPE = 128
MLA_ROPE = 64
MLA_V = 128
MLA_QK = MLA_NOPE + MLA_ROPE
MLA_PAD_QK = 256
MLA_HEADS_PER_STEP = 4
ROPE_THETA = 10000.0
EPS = 1e-6
NEG_BIG = -1e30
MLA_EXP2_SCALE = (MLA_QK ** -0.5) * math.log2(math.e)

OFF_POOL = 0
OFF_DIL = OFF_POOL + BRANCH_W
OFF_SGU = OFF_DIL + DIL_QKV
OFF_MLA_Q = OFF_SGU + 2 * BRANCH_W
OFF_MLA_KV = OFF_MLA_Q + MLA_Q_LORA
OFF_MLA_KR = OFF_MLA_KV + MLA_KV_LORA
OFF_GATE = OFF_MLA_KR + MLA_ROPE

LANE = 128
REST_POOL = 0
REST_SGU = 512
REST_CQ = 1536
REST_CKV = 1920
REST_KR = 2048
REST_W = REST_KR + LANE

VMEM_LIMIT = 56 * 1024 * 1024


def _params(*sem):
    return pltpu.CompilerParams(dimension_semantics=sem, vmem_limit_bytes=VMEM_LIMIT)


def _rms(xf, g):
    return xf * lax.rsqrt(jnp.mean(xf * xf, axis=-1, keepdims=True) + EPS) * g


def _dot_nt(a, b_t):
    return lax.dot_general(a, b_t, (((1,), (1,)), ((), ())), preferred_element_type=F32)


FFN_PREFETCH_STEP = 1


def _ffn_body(x_hbm, g_ref, wg_ref, wu_ref, wd_ref, o_hbm, acc_ref, h_ref, in_sem, out_sem):
    i, f = pl.program_id(0), pl.program_id(1)
    ni, nf = pl.num_programs(0), pl.num_programs(1)
    tm = acc_ref.shape[1]
    slot = lax.rem(i, 2)
    other = 1 - slot

    def x_copy(tile, buf):
        return pltpu.make_async_copy(x_hbm.at[pl.ds(tile * tm, tm), :], acc_ref.at[buf], in_sem.at[buf])

    def out_copy(tile, buf):
        return pltpu.make_async_copy(acc_ref.at[buf], o_hbm.at[pl.ds(tile * tm, tm), :], out_sem.at[buf])

    @pl.when(f == 0)
    def _():
        @pl.when(i == 0)
        def _():
            x_copy(i, slot).start()
        x_copy(i, slot).wait()
        h_ref[...] = _rms(acc_ref[slot], g_ref[...]).astype(BF16)

    @pl.when((f == FFN_PREFETCH_STEP) & (i + 1 < ni))
    def _():
        @pl.when(i > 0)
        def _():
            out_copy(i - 1, other).wait()
        x_copy(i + 1, other).start()

    h = h_ref[...]
    g = jnp.dot(h, wg_ref[...].astype(BF16), preferred_element_type=F32)
    u = jnp.dot(h, wu_ref[...].astype(BF16), preferred_element_type=F32)
    a = (0.5 * (jax.nn.silu(g) * u)).astype(BF16)
    acc_ref[slot] += jnp.dot(a, wd_ref[...].astype(BF16), preferred_element_type=F32)

    @pl.when(f == nf - 1)
    def _():
        out_copy(i, slot).start()

        @pl.when(i == ni - 1)
        def _():
            @pl.when(ni > 1)
            def _():
                out_copy(i - 1, other).wait()
            out_copy(i, slot).wait()


def ffn(x, norm_g, wg, wu, wd, layer, *, tm=1024, tf=512):
    s, d = x.shape
    f = wg.shape[2]
    assert f // tf > FFN_PREFETCH_STEP
    return pl.pallas_call(
        _ffn_body,
        grid=(s // tm, f // tf),
        in_specs=[
            pl.BlockSpec(memory_space=pl.ANY),
            pl.BlockSpec((1, d), lambda i, j: (0, 0)),
            pl.BlockSpec((None, d, tf), lambda i, j: (layer, 0, j)),
            pl.BlockSpec((None, d, tf), lambda i, j: (layer, 0, j)),
            pl.BlockSpec((None, tf, d), lambda i, j: (layer, j, 0)),
        ],
        out_specs=pl.BlockSpec(memory_space=pl.ANY),
        out_shape=jax.ShapeDtypeStruct((s, d), F32),
        scratch_shapes=[pltpu.VMEM((2, tm, d), F32), pltpu.VMEM((tm, d), BF16),
                        pltpu.SemaphoreType.DMA((2,)), pltpu.SemaphoreType.DMA((2,))],
        compiler_params=_params("arbitrary", "arbitrary"),
        name="ffn",
    )(x, norm_g.reshape(1, d), wg, wu, wd)


def _proj_rest_body(x_ref, g_ref, wp_ref, ws_ref, wq_ref, wkv_ref, wkr_ref, p_ref, h_ref):
    h = _rms(x_ref[...], g_ref[...]).astype(BF16)
    h_ref[...] = h
    for w_ref, off in ((wp_ref, REST_POOL), (ws_ref, REST_SGU), (wq_ref, REST_CQ), (wkv_ref, REST_CKV),
                       (wkr_ref, REST_KR)):
        p_ref[:, off:off + w_ref.shape[0]] = _dot_nt(h, w_ref[...])


def proj_rest(x, norm_g, w_main, layer, *, tm=512):
    s, d = x.shape

    def wspec(off, width):
        return pl.BlockSpec((None, width, d), lambda i: (layer, off // width, 0))

    return pl.pallas_call(
        _proj_rest_body,
        grid=(s // tm,),
        in_specs=[
            pl.BlockSpec((tm, d), lambda i: (i, 0)),
            pl.BlockSpec((1, d), lambda i: (0, 0)),
            wspec(OFF_POOL, BRANCH_W),
            wspec(OFF_SGU, 2 * BRANCH_W),
            wspec(OFF_MLA_Q, MLA_Q_LORA),
            wspec(OFF_MLA_KV, MLA_KV_LORA),
            wspec(OFF_MLA_KR, LANE),
        ],
        out_specs=[
            pl.BlockSpec((tm, REST_W), lambda i: (i, 0)),
            pl.BlockSpec((tm, d), lambda i: (i, 0)),
        ],
        out_shape=[jax.ShapeDtypeStruct((s, REST_W), F32), jax.ShapeDtypeStruct((s, d), BF16)],
        compiler_params=_params("parallel"),
        name="proj_rest",
    )(x, norm_g.reshape(1, d), w_main, w_main, w_main, w_main, w_main)


def _proj_dil_body(h_ref, w0_ref, w1_ref, w2_ref, g_ref, o0_ref, o1_ref, o2_ref, scr1_ref, scr2_ref):
    normed = pl.program_id(1) < 2
    h = h_ref[...]
    tm = h.shape[0]
    for w_ref, o_ref, scr_ref, (_, d) in reversed(list(zip((w0_ref, w1_ref, w2_ref), (o0_ref, o1_ref, o2_ref),
                                                           (None, scr1_ref, scr2_ref), DIL_PATTERNS))):
        acc = _dot_nt(h, w_ref[...])
        for a in range(DIL_HEADS_PER_GROUP):
            sl = slice(a * DIL_HEAD_DIM, (a + 1) * DIL_HEAD_DIM)
            xh = acc[:, sl]
            r = lax.rsqrt(jnp.mean(xh * xh, axis=-1, keepdims=True) + EPS)
            y = xh * jnp.where(normed, r, 1.0) * g_ref[:, sl]
            if d == 1:
                o_ref[0, :, sl] = y.astype(BF16)
            else:
                scr_ref[a] = y
        if d > 1:
            for r in range(d):
                for a in range(DIL_HEADS_PER_GROUP):
                    sl = slice(a * DIL_HEAD_DIM, (a + 1) * DIL_HEAD_DIM)
                    o_ref[r, :, sl] = scr_ref[a, pl.ds(r, tm // d, stride=d), :].astype(BF16)


def proj_dil(h, w_main, gains, layer, *, tm=1024):
    s, dm = h.shape
    first = OFF_DIL // BRANCH_W

    def wspec(g):
        return pl.BlockSpec((None, BRANCH_W, dm), lambda i, j: (layer, first + j * DIL_GROUPS + g, 0))

    out_specs, out_shapes, scratch = [], [], []
    for _, d in DIL_PATTERNS:
        out_specs.append(pl.BlockSpec((d, tm // d, BRANCH_W), lambda i, j: (0, i, j)))
        out_shapes.append(jax.ShapeDtypeStruct((d, s // d, 3 * BRANCH_W), BF16))
        if d > 1:
            scratch.append(pltpu.VMEM((DIL_HEADS_PER_GROUP, tm, DIL_HEAD_DIM), F32))
    return pl.pallas_call(
        _proj_dil_body,
        grid=(s // tm, 3),
        in_specs=[
            pl.BlockSpec((tm, dm), lambda i, j: (i, 0)),
            wspec(0), wspec(1), wspec(2),
            pl.BlockSpec((None, 1, BRANCH_W), lambda i, j: (j, 0, 0)),
        ],
        out_specs=out_specs,
        out_shape=out_shapes,
        scratch_shapes=scratch,
        compiler_params=_params("parallel", "arbitrary"),
        name="proj_dil",
    )(h, w_main, w_main, w_main, gains)


def _pool_body(prev_ref, cur_ref, next_ref, w_ref, scale_ref, o_ref, ext_ref, *, seq):
    i = pl.program_id(0)
    tm = cur_ref.shape[0]
    cur = cur_ref[...]
    ext_ref[0:POOL_HALO, :] = jnp.where(i > 0, prev_ref[...], 0.0)
    ext_ref[POOL_HALO:POOL_HALO + tm, :] = cur
    ext_ref[POOL_HALO + tm:2 * POOL_HALO + tm, :] = jnp.where(
        i < pl.num_programs(0) - 1, next_ref[...], 0.0)
    t = i * tm + lax.broadcasted_iota(jnp.int32, (tm, 1), 0)
    for g, w in enumerate(POOL_WINDOWS):
        sl = slice(g * LANE, (g + 1) * LANE)
        tot = jnp.zeros((tm, LANE), F32)
        for o in range(-(w // 2), w // 2):
            tot = tot + ext_ref[POOL_HALO + o:POOL_HALO + o + tm, sl]
        cnt = (jnp.minimum(t + w // 2, seq) - jnp.maximum(t - w // 2, 0)).astype(F32)
        dev = (tot / cnt - cur[:, sl]).astype(BF16)
        y = jnp.dot(dev, w_ref[g], preferred_element_type=F32) * scale_ref[:, sl]
        o_ref[:, sl] = y.astype(BF16)


def pool_mixer(p_rest, pool_w, pool_scale, layer, *, tm=512):
    s = p_rest.shape[0]
    hb = tm // POOL_HALO
    last = s // POOL_HALO - 1
    n_win = len(POOL_WINDOWS)
    return pl.pallas_call(
        functools.partial(_pool_body, seq=s),
        grid=(s // tm,),
        in_specs=[
            pl.BlockSpec((POOL_HALO, BRANCH_W), lambda i: (jnp.maximum(i * hb - 1, 0), 0)),
            pl.BlockSpec((tm, BRANCH_W), lambda i: (i, 0)),
            pl.BlockSpec((POOL_HALO, BRANCH_W), lambda i: (jnp.minimum((i + 1) * hb, last), 0)),
            pl.BlockSpec((None, n_win, LANE, LANE), lambda i: (layer, 0, 0, 0)),
            pl.BlockSpec((1, BRANCH_W), lambda i: (0, 0)),
        ],
        out_specs=pl.BlockSpec((tm, BRANCH_W), lambda i: (i, 0)),
        out_shape=jax.ShapeDtypeStruct((s, BRANCH_W), BF16),
        scratch_shapes=[pltpu.VMEM((tm + 2 * POOL_HALO, BRANCH_W), F32)],
        compiler_params=_params("parallel"),
        name="pool_mixer",
    )(p_rest, p_rest, p_rest, pool_w, pool_scale.reshape(1, BRANCH_W))


def _alibi_slopes():
    n = DIL_HEADS
    return np.exp2(np.float32(-8.0) * np.arange(1, n + 1, dtype=np.float32) / np.float32(n))


LSE_LANES = LANE // DIL_HEADS_PER_GROUP


def _dil_attn_body(cur_ref, prev_ref, next_ref, o_ref, lse_ref, *, sub_len, penalties):
    i = pl.program_id(1)
    t = cur_ref.shape[0]
    blk = prev_ref.shape[0]
    nk = blk + 2 * DIL_SIDE
    row = lax.broadcasted_iota(jnp.int32, (blk, nk), 0)
    col = lax.broadcasted_iota(jnp.int32, (blk, nk), 1)
    dist = jnp.abs(col - DIL_SIDE - row)
    in_band = dist <= DIL_SIDE
    distf = dist.astype(F32)
    lane_head = lax.broadcasted_iota(jnp.int32, (1, LANE), 1) // LSE_LANES
    lse_tiles = [jnp.zeros((blk, LANE), F32) for _ in range(t // blk)]
    for h in range(DIL_HEADS_PER_GROUP):
        ql = slice(h * DIL_HEAD_DIM, (h + 1) * DIL_HEAD_DIM)
        kl = slice(BRANCH_W + h * DIL_HEAD_DIM, BRANCH_W + (h + 1) * DIL_HEAD_DIM)
        vl = slice(2 * BRANCH_W + h * DIL_HEAD_DIM, 2 * BRANCH_W + (h + 1) * DIL_HEAD_DIM)
        k_ext = jnp.concatenate([prev_ref[blk - DIL_SIDE:, kl], cur_ref[:, kl], next_ref[:DIL_SIDE, kl]], axis=0)
        v_ext = jnp.concatenate([prev_ref[blk - DIL_SIDE:, vl], cur_ref[:, vl], next_ref[:DIL_SIDE, vl]], axis=0)
        bias = jnp.where(in_band, -penalties[h] * distf, NEG_BIG)
        for b in range(t // blk):
            rows = slice(b * blk, (b + 1) * blk)
            kpos = i * t + b * blk + col - DIL_SIDE
            s = lax.dot_general(cur_ref[rows, ql], k_ext[b * blk:b * blk + nk], (((1,), (1,)), ((), ())),
                                preferred_element_type=F32) * (DIL_HEAD_DIM ** -0.5)
            s = jnp.where((kpos >= 0) & (kpos < sub_len), s + bias, NEG_BIG)
            m = jnp.max(s, axis=-1, keepdims=True)
            p = jnp.exp(s - m)
            den = jnp.sum(p, axis=-1, keepdims=True)
            o_ref[rows, ql] = jnp.dot((p / den).astype(BF16), v_ext[b * blk:b * blk + nk],
                                      preferred_element_type=F32).astype(BF16)
            lse_tiles[b] = jnp.where(lane_head == h, m + jnp.log(den), lse_tiles[b])
    for b, tile in enumerate(lse_tiles):
        lse_ref[b * blk:(b + 1) * blk, :] = tile


def dil_attn(qkv, group, *, blk=128, max_blocks=8):
    d, sub_len, width = qkv.shape
    n_blk = sub_len // blk
    qb = min(max_blocks, n_blk)
    t = blk * qb
    slopes = _alibi_slopes()[group * DIL_HEADS_PER_GROUP:(group + 1) * DIL_HEADS_PER_GROUP]
    penalties = tuple(float(np.float32(x) * np.float32(d)) for x in slopes)

    def halo(step):
        return pl.BlockSpec((None, blk, width), lambda r, i: (r, jnp.clip(i * qb + step, 0, n_blk - 1), 0))

    return pl.pallas_call(
        functools.partial(_dil_attn_body, sub_len=sub_len, penalties=penalties),
        grid=(d, sub_len // t),
        in_specs=[pl.BlockSpec((None, t, width), lambda r, i: (r, i, 0)), halo(-1), halo(qb)],
        out_specs=[pl.BlockSpec((None, t, BRANCH_W), lambda r, i: (r, i, 0)),
                   pl.BlockSpec((None, t, LANE), lambda r, i: (r, i, 0))],
        out_shape=[jax.ShapeDtypeStruct((d, sub_len, BRANCH_W), BF16),
                   jax.ShapeDtypeStruct((d, sub_len, LANE), F32)],
        compiler_params=_params("parallel", "parallel"),
        name=f"dil_attn_g{group}",
    )(qkv, qkv, qkv)


def _dil_combine_body(o0, o1, o2, l0, l1, l2, z_ref, so1, so2, sl1, sl2):
    tm = z_ref.shape[0]
    for src, scr, (_, d) in ((o1, so1, DIL_PATTERNS[1]), (o2, so2, DIL_PATTERNS[2])):
        for r in range(d):
            for a in range(DIL_HEADS_PER_GROUP):
                sl = slice(a * DIL_HEAD_DIM, (a + 1) * DIL_HEAD_DIM)
                scr[a, pl.ds(r, tm // d, stride=d), :] = src[r, :, sl].astype(F32)
    for src, scr, (_, d) in ((l1, sl1, DIL_PATTERNS[1]), (l2, sl2, DIL_PATTERNS[2])):
        for r in range(d):
            scr[pl.ds(r, tm // d, stride=d), :] = src[r]
    la, lb, lc = l0[0], sl1[...], sl2[...]
    m = jnp.maximum(jnp.maximum(la, lb), lc)
    ea, eb, ec = jnp.exp(la - m), jnp.exp(lb - m), jnp.exp(lc - m)
    tot = ea + eb + ec
    wa, wb, wc = ea / tot, eb / tot, ec / tot
    for a in range(DIL_HEADS_PER_GROUP):
        sl = slice(a * DIL_HEAD_DIM, (a + 1) * DIL_HEAD_DIM)
        one = slice(a * LSE_LANES, a * LSE_LANES + 1)
        shape = (tm, DIL_HEAD_DIM)
        z = (jnp.broadcast_to(wa[:, one], shape) * o0[0, :, sl].astype(F32)
             + jnp.broadcast_to(wb[:, one], shape) * so1[a]
             + jnp.broadcast_to(wc[:, one], shape) * so2[a])
        z_ref[:, sl] = z.astype(BF16)


def dil_combine(outs, lses, *, tm=512):
    s = outs[0].shape[1]
    o_specs = [pl.BlockSpec((d, tm // d, BRANCH_W), lambda i: (0, i, 0)) for _, d in DIL_PATTERNS]
    l_specs = [pl.BlockSpec((d, tm // d, LANE), lambda i: (0, i, 0)) for _, d in DIL_PATTERNS]
    slab = pltpu.VMEM((DIL_HEADS_PER_GROUP, tm, DIL_HEAD_DIM), F32)
    return pl.pallas_call(
        _dil_combine_body,
        grid=(s // tm,),
        in_specs=o_specs + l_specs,
        out_specs=pl.BlockSpec((tm, BRANCH_W), lambda i: (i, 0)),
        out_shape=jax.ShapeDtypeStruct((s, BRANCH_W), BF16),
        scratch_shapes=[slab, slab, pltpu.VMEM((tm, LANE), F32), pltpu.VMEM((tm, LANE), F32)],
        compiler_params=_params("parallel"),
        name="dil_combine",
    )(*outs, *lses)


def _sgu_body(u_ref, v_ref, g_ref, ws_ref, b_ref, o_ref):
    tm = u_ref.shape[0]
    u = jax.nn.gelu(u_ref[...])
    vn = _rms(jax.nn.gelu(v_ref[...]), g_ref[...]).astype(BF16)
    for n in range(tm // SGU_CHUNK):
        rows = slice(n * SGU_CHUNK, (n + 1) * SGU_CHUNK)
        for g in range(SGU_GROUPS):
            cols = slice(g * LANE, (g + 1) * LANE)
            mixed = jnp.dot(ws_ref[g], vn[rows, cols], preferred_element_type=F32) + b_ref[g]
            o_ref[rows, cols] = (u[rows, cols] * mixed).astype(BF16)


def sgu_mixer(p_rest, v_norm_g, ws, b_full, layer, *, tm=512):
    s = p_rest.shape[0]
    return pl.pallas_call(
        _sgu_body,
        grid=(s // tm,),
        in_specs=[
            pl.BlockSpec((tm, BRANCH_W), lambda i: (i, REST_SGU // BRANCH_W)),
            pl.BlockSpec((tm, BRANCH_W), lambda i: (i, REST_SGU // BRANCH_W + 1)),
            pl.BlockSpec((1, BRANCH_W), lambda i: (0, 0)),
            pl.BlockSpec((None, SGU_GROUPS, SGU_CHUNK, SGU_CHUNK), lambda i: (layer, 0, 0, 0)),
            pl.BlockSpec((SGU_GROUPS, SGU_CHUNK, LANE), lambda i: (0, 0, 0)),
        ],
        out_specs=pl.BlockSpec((tm, BRANCH_W), lambda i: (i, 0)),
        out_shape=jax.ShapeDtypeStruct((s, BRANCH_W), BF16),
        compiler_params=_params("parallel"),
        name="sgu_mixer",
    )(p_rest, p_rest, v_norm_g.reshape(1, BRANCH_W), ws, b_full)


def _swap_rope_halves(y):
    width = y.shape[1]
    lane = lax.broadcasted_iota(jnp.int32, y.shape, 1)
    first_half = (lane % MLA_ROPE) < MLA_ROPE // 2
    return jnp.where(first_half, pltpu.roll(y, width - MLA_ROPE // 2, 1),
                     pltpu.roll(y, MLA_ROPE // 2, 1))


def _mla_prep_body(cq_ref, ckv_ref, kr_ref, cos_ref, sin_ref, qa_g, wuq_ref, kva_g, wukv_ref,
                   gqn_ref, gqr_ref, gkn_ref, gkr_ref, q_ref, k_ref, v_ref):
    nope_w = MLA_HEADS * MLA_NOPE
    qall = jnp.dot(_rms(cq_ref[...], qa_g[...]).astype(BF16), wuq_ref[...], preferred_element_type=F32)
    kv = jnp.dot(_rms(ckv_ref[...], kva_g[...]).astype(BF16), wukv_ref[...], preferred_element_type=F32)
    qn, qr = qall[:, :nope_w], qall[:, nope_w:]
    kn = kv[:, :nope_w]
    v_ref[...] = kv[:, nope_w:].astype(BF16)
    kr_blk = kr_ref[...]
    lane128 = lax.broadcasted_iota(jnp.int32, kr_blk.shape, 1)
    kr2 = jnp.where(lane128 < MLA_ROPE, kr_blk, pltpu.roll(kr_blk, MLA_ROPE, 1))
    kr = jnp.concatenate([kr2] * (MLA_HEADS // 2), axis=1)

    rope_w = MLA_HEADS * MLA_ROPE
    head_of_lane = lax.broadcasted_iota(jnp.int32, (1, rope_w), 1) // MLA_ROPE
    qr_sq = qr * qr
    kr_ssq = jnp.sum(jnp.where(head_of_lane == 0, kr * kr, 0.0), axis=-1, keepdims=True)
    rq, rk = [], []
    for h in range(MLA_HEADS):
        sl = slice(h * MLA_NOPE, (h + 1) * MLA_NOPE)
        q_ssq = (jnp.sum(qn[:, sl] * qn[:, sl], axis=-1, keepdims=True)
                 + jnp.sum(jnp.where(head_of_lane == h, qr_sq, 0.0), axis=-1, keepdims=True))
        k_ssq = jnp.sum(kn[:, sl] * kn[:, sl], axis=-1, keepdims=True) + kr_ssq
        rq.append(lax.rsqrt(q_ssq / MLA_QK + EPS))
        rk.append(lax.rsqrt(k_ssq / MLA_QK + EPS))

    def per_lane(rs):
        out = jnp.zeros((kr.shape[0], rope_w), F32)
        for h in range(MLA_HEADS):
            out = jnp.where(head_of_lane == h, rs[h], out)
        return out

    def rotate(x, rs, g_ref):
        y = x * per_lane(rs) * g_ref[...]
        return y * cos_ref[...] + _swap_rope_halves(y) * sin_ref[...]

    q_rope = rotate(qr, rq, gqr_ref) * MLA_EXP2_SCALE
    k_rope = rotate(kr, rk, gkr_ref)
    lane_tile = lax.broadcasted_iota(jnp.int32, (1, LANE), 1) // MLA_ROPE
    for h in range(MLA_HEADS):
        sl = slice(h * MLA_NOPE, (h + 1) * MLA_NOPE)
        base = h * MLA_PAD_QK
        tile = slice((h // 2) * LANE, (h // 2 + 1) * LANE)
        q_ref[:, base:base + MLA_NOPE] = (qn[:, sl] * rq[h] * gqn_ref[:, sl] * MLA_EXP2_SCALE).astype(BF16)
        k_ref[:, base:base + MLA_NOPE] = (kn[:, sl] * rk[h] * gkn_ref[:, sl]).astype(BF16)
        q_ref[:, base + MLA_NOPE:base + MLA_PAD_QK] = jnp.where(
            lane_tile == h % 2, q_rope[:, tile], 0.0).astype(BF16)
        k_ref[:, base + MLA_NOPE:base + MLA_PAD_QK] = k_rope[:, tile].astype(BF16)


def mla_prep(p_rest, cos4, sin4, qa_g, wuq, kva_g, wukv, gqn, gqr, gkn, gkr, *, tm=512):
    s = p_rest.shape[0]
    rope_w = MLA_HEADS * MLA_ROPE

    def const(shape):
        return pl.BlockSpec(shape, lambda i: (0,) * len(shape))

    return pl.pallas_call(
        _mla_prep_body,
        grid=(s // tm,),
        in_specs=[
            pl.BlockSpec((tm, MLA_Q_LORA), lambda i: (i, REST_CQ // MLA_Q_LORA)),
            pl.BlockSpec((tm, MLA_KV_LORA), lambda i: (i, REST_CKV // MLA_KV_LORA)),
            pl.BlockSpec((tm, LANE), lambda i: (i, REST_KR // LANE)),
            pl.BlockSpec((tm, rope_w), lambda i: (i, 0)),
            pl.BlockSpec((tm, rope_w), lambda i: (i, 0)),
            const((1, MLA_Q_LORA)), const(wuq.shape), const((1, MLA_KV_LORA)), const(wukv.shape),
            const(gqn.shape), const(gqr.shape), const(gkn.shape), const(gkr.shape),
        ],
        out_specs=[
            pl.BlockSpec((tm, MLA_HEADS * MLA_PAD_QK), lambda i: (i, 0)),
            pl.BlockSpec((tm, MLA_HEADS * MLA_PAD_QK), lambda i: (i, 0)),
            pl.BlockSpec((tm, MLA_HEADS * MLA_V), lambda i: (i, 0)),
        ],
        out_shape=[
            jax.ShapeDtypeStruct((s, MLA_HEADS * MLA_PAD_QK), BF16),
            jax.ShapeDtypeStruct((s, MLA_HEADS * MLA_PAD_QK), BF16),
            jax.ShapeDtypeStruct((s, MLA_HEADS * MLA_V), BF16),
        ],
        compiler_params=_params("parallel"),
        name="mla_prep",
    )(p_rest, p_rest, p_rest, cos4, sin4, qa_g.reshape(1, -1), wuq, kva_g.reshape(1, -1), wukv,
      gqn, gqr, gkn, gkr)


def _mla_attn_body(q_ref, k_ref, v_ref, o_ref, m_ref, l_ref, acc_ref):
    j = pl.program_id(2)

    @pl.when(j == 0)
    def _():
        m_ref[...] = jnp.full(m_ref.shape, NEG_BIG, F32)
        l_ref[...] = jnp.zeros(l_ref.shape, F32)
        acc_ref[...] = jnp.zeros(acc_ref.shape, F32)

    tk = k_ref.shape[0]
    for h in range(MLA_HEADS_PER_STEP):
        qk = slice(h * MLA_PAD_QK, (h + 1) * MLA_PAD_QK)
        vs = slice(h * MLA_V, (h + 1) * MLA_V)
        s = lax.dot_general(q_ref[:, qk], k_ref[:, qk], (((1,), (1,)), ((), ())),
                            preferred_element_type=F32)
        m_prev = m_ref[h]
        m_new = jnp.maximum(m_prev, jnp.max(s, axis=-1, keepdims=True))
        alpha = jnp.exp2(m_prev - m_new)
        p = jnp.exp2(s - jnp.tile(m_new, (1, tk // LANE)))
        l_ref[h] = alpha * l_ref[h] + jnp.sum(p, axis=-1, keepdims=True)
        acc_ref[h] = alpha * acc_ref[h] + jnp.dot(p.astype(BF16), v_ref[:, vs], preferred_element_type=F32)
        m_ref[h] = m_new

    @pl.when(j == pl.num_programs(2) - 1)
    def _():
        for h in range(MLA_HEADS_PER_STEP):
            o_ref[:, h * MLA_V:(h + 1) * MLA_V] = (acc_ref[h] / l_ref[h]).astype(BF16)


def mla_attn(q, k, v, *, tq=1024, tk=2048):
    s = q.shape[0]
    hp = MLA_HEADS_PER_STEP
    stat = pltpu.VMEM((hp, tq, LANE), F32)
    return pl.pallas_call(
        _mla_attn_body,
        grid=(s // tq, MLA_HEADS // hp, s // tk),
        in_specs=[
            pl.BlockSpec((tq, hp * MLA_PAD_QK), lambda i, h, j: (i, h)),
            pl.BlockSpec((tk, hp * MLA_PAD_QK), lambda i, h, j: (j, h)),
            pl.BlockSpec((tk, hp * MLA_V), lambda i, h, j: (j, h)),
        ],
        out_specs=pl.BlockSpec((tq, hp * MLA_V), lambda i, h, j: (i, h)),
        out_shape=jax.ShapeDtypeStruct((s, MLA_HEADS * MLA_V), BF16),
        scratch_shapes=[stat, stat, stat],
        compiler_params=_params("parallel", "parallel", "arbitrary"),
        name="mla_attn",
    )(q, k, v)


def _merge_body(h_ref, *refs):
    z_refs, wg_refs, wb_refs = refs[:N_BRANCH], refs[N_BRANCH:2 * N_BRANCH], refs[2 * N_BRANCH:3 * N_BRANCH]
    o_ref = refs[3 * N_BRANCH]
    h = h_ref[...]
    acc = None
    for z_ref, wg_ref, wb_ref in zip(z_refs, wg_refs, wb_refs):
        gate = jax.nn.sigmoid(_dot_nt(h, wg_ref[0]))
        term = gate * jnp.dot(z_ref[...], wb_ref[...].astype(BF16), preferred_element_type=F32)
        acc = term if acc is None else acc + term
    o_ref[...] = acc.astype(BF16)


GATE_ROW_ALIGN = 64


def merge(h, zs, wt, w_branch, layer, *, tm=1024, tn=512):
    s, d = h.shape
    ct = d // tn
    assert OFF_GATE % GATE_ROW_ALIGN == 0 and d % GATE_ROW_ALIGN == 0 and tn % GATE_ROW_ALIGN == 0
    z_spec = pl.BlockSpec((tm, BRANCH_W), lambda i, c: (i, 0))
    g_specs = [pl.BlockSpec((pl.Element(1), pl.Element(tn), pl.Element(d)),
                            lambda i, c, n=n: (layer, pl.multiple_of(OFF_GATE + n * d + c * tn, GATE_ROW_ALIGN), 0))
               for n in range(N_BRANCH)]
    b_specs = [pl.BlockSpec((None, None, BRANCH_W, tn), lambda i, c, n=n: (layer, n, 0, c))
               for n in range(N_BRANCH)]
    return pl.pallas_call(
        _merge_body,
        grid=(s // tm, ct),
        in_specs=[pl.BlockSpec((tm, d), lambda i, c: (i, 0))] + [z_spec] * N_BRANCH + g_specs + b_specs,
        out_specs=pl.BlockSpec((tm, tn), lambda i, c: (i, c)),
        out_shape=jax.ShapeDtypeStruct((s, d), BF16),
        compiler_params=_params("parallel", "arbitrary"),
        name="merge",
    )(h, *zs, *([wt] * N_BRANCH), *([w_branch] * N_BRANCH))


def _out_proj_body(x_ref, y_ref, w_ref, o_ref):
    o_ref[...] = x_ref[...] + jnp.dot(y_ref[...], w_ref[...].astype(BF16), preferred_element_type=F32)


def out_proj(x, y, w, layer, *, tm=1024, tn=1024):
    s, d = x.shape
    return pl.pallas_call(
        _out_proj_body,
        grid=(s // tm, d // tn),
        in_specs=[
            pl.BlockSpec((tm, tn), lambda i, j: (i, j)),
            pl.BlockSpec((tm, d), lambda i, j: (i, 0)),
            pl.BlockSpec((None, d, tn), lambda i, j: (layer, 0, j)),
        ],
        out_specs=pl.BlockSpec((tm, tn), lambda i, j: (i, j)),
        out_shape=jax.ShapeDtypeStruct((s, d), F32),
        compiler_params=_params("parallel", "arbitrary"),
        name="out_proj",
    )(x, y, w)


def _cast_body(x_ref, o_ref):
    o_ref[...] = x_ref[...].astype(BF16)


def cast_rows(wt, *, tr=1024):
    nl, n, d = wt.shape
    return pl.pallas_call(
        _cast_body,
        grid=(nl, pl.cdiv(n, tr)),
        in_specs=[pl.BlockSpec((None, tr, d), lambda l, r: (l, r, 0))],
        out_specs=pl.BlockSpec((None, tr, d), lambda l, r: (l, r, 0)),
        out_shape=jax.ShapeDtypeStruct((nl, n, d), BF16),
        compiler_params=_params("parallel", "parallel"),
        name="cast_rows",
    )(wt)


def _rope_tables(s):
    pos = jnp.arange(s, dtype=F32)
    inv_freq = ROPE_THETA ** (-jnp.arange(0, MLA_ROPE, 2, dtype=F32) / MLA_ROPE)
    ang = pos[:, None] * inv_freq[None, :]
    cos, sin = jnp.cos(ang), jnp.sin(ang)
    cos4 = jnp.tile(jnp.concatenate([cos, cos], axis=-1), (1, MLA_HEADS))
    sin4 = jnp.tile(jnp.concatenate([-sin, sin], axis=-1), (1, MLA_HEADS))
    return cos4, sin4


def _split_heads(w, widths):
    per_head = sum(widths)
    w3 = w.reshape(w.shape[0], MLA_HEADS, per_head)
    parts, off = [], 0
    for wd in widths:
        parts.append(w3[:, :, off:off + wd].reshape(w.shape[0], MLA_HEADS * wd))
        off += wd
    return jnp.concatenate(parts, axis=1)


def kernel(x, ffn1_norm, ffn1_w_gate, ffn1_w_up, ffn1_w_down, mix_norm, w_in, pool_w, pool_scale,
           dil_q_norm, dil_k_norm, sgu_v_norm, sgu_w, sgu_b, mla_q_a_norm, mla_w_uq, mla_kv_a_norm,
           mla_w_ukv, mla_q_norm, mla_k_norm, w_branch, w_out, ffn2_norm, ffn2_w_gate, ffn2_w_up,
           ffn2_w_down):
    b, s, d = x.shape
    cos4, sin4 = _rope_tables(s)
    w_main = cast_rows(jnp.swapaxes(w_in, 1, 2))
    pool_wb, sgu_wb = pool_w.astype(BF16), sgu_w.astype(BF16)
    ones = jnp.ones((BRANCH_W,), F32)

    outs = []
    for bi in range(b):
        xb = x[bi]
        for l in range(DEPTH):
            xb = ffn(xb, ffn1_norm[l], ffn1_w_gate, ffn1_w_up, ffn1_w_down, l)

            p_rest, h = proj_rest(xb, mix_norm[l], w_main, l)
            dil_gains = jnp.stack([jnp.tile(dil_q_norm[l], DIL_HEADS_PER_GROUP),
                                   jnp.tile(dil_k_norm[l], DIL_HEADS_PER_GROUP), ones]).reshape(3, 1, BRANCH_W)
            qkvs = proj_dil(h, w_main, dil_gains, l)
            z_a = pool_mixer(p_rest, pool_wb, pool_scale[l], l)
            o_l = [dil_attn(qkvs[g], g) for g in range(DIL_GROUPS)]
            z_b = dil_combine([o for o, _ in o_l], [ls for _, ls in o_l])
            b_full = jnp.broadcast_to(sgu_b[l][:, :, None], (SGU_GROUPS, SGU_CHUNK, LANE))
            z_c = sgu_mixer(p_rest, sgu_v_norm[l], sgu_wb, b_full, l)
            q, k, v = mla_prep(
                p_rest, cos4, sin4, mla_q_a_norm[l],
                _split_heads(mla_w_uq[l], (MLA_NOPE, MLA_ROPE)).astype(BF16),
                mla_kv_a_norm[l], _split_heads(mla_w_ukv[l], (MLA_NOPE, MLA_V)).astype(BF16),
                jnp.tile(mla_q_norm[l][:MLA_NOPE], MLA_HEADS).reshape(1, -1),
                jnp.tile(mla_q_norm[l][MLA_NOPE:], MLA_HEADS).reshape(1, -1),
                jnp.tile(mla_k_norm[l][:MLA_NOPE], MLA_HEADS).reshape(1, -1),
                jnp.tile(mla_k_norm[l][MLA_NOPE:], MLA_HEADS).reshape(1, -1))
            z_d = mla_attn(q, k, v)
            merged = merge(h, (z_a, z_b, z_c, z_d), w_main, w_branch, l)
            xb = out_proj(xb, merged, w_out, l)

            xb = ffn(xb, ffn2_norm[l], ffn2_w_gate, ffn2_w_up, ffn2_w_down, l)
        outs.append(xb)
    return jnp.stack(outs, axis=0)
```

```python
import functools
import math

import numpy as np
import jax
import jax.numpy as jnp
from jax import lax
from jax.experimental import pallas as pl
from jax.experimental.pallas import tpu as pltpu

F32 = jnp.float32
BF16 = jnp.bfloat16

D_MODEL = 2048
DEPTH = 4
D_FF = 5632
N_BRANCH = 4
BRANCH_W = 512
POOL_WINDOWS = (2, 4, 8, 16)
POOL_HALO = 8
DIL_PATTERNS = ((128, 1), (512, 4), (2048, 16))
DIL_GROUPS = len(DIL_PATTERNS)
DIL_HEADS = 12
DIL_HEADS_PER_GROUP = 4
DIL_HEAD_DIM = 128
DIL_QKV = 3 * DIL_HEADS * DIL_HEAD_DIM
DIL_SIDE = 64
SGU_CHUNK = 128
SGU_GROUPS = 4
MLA_HEADS = 4
MLA_Q_LORA = 384
MLA_KV_LORA = 128
MLA_NOPE = 128
MLA_ROPE = 64
MLA_V = 128
MLA_QK = MLA_NOPE + MLA_ROPE
MLA_PAD_QK = 256
MLA_HEADS_PER_STEP = 4
ROPE_THETA = 10000.0
EPS = 1e-6
NEG_BIG = -1e30
MLA_EXP2_SCALE = (MLA_QK ** -0.5) * math.log2(math.e)

OFF_POOL = 0
OFF_DIL = OFF_POOL + BRANCH_W
OFF_SGU = OFF_DIL + DIL_QKV
OFF_MLA_Q = OFF_SGU + 2 * BRANCH_W
OFF_MLA_KV = OFF_MLA_Q + MLA_Q_LORA
OFF_MLA_KR = OFF_MLA_KV + MLA_KV_LORA
OFF_GATE = OFF_MLA_KR + MLA_ROPE

LANE = 128
REST_POOL = 0
REST_SGU = 512
REST_CQ = 1536
REST_CKV = 1920
REST_KR = 2048
REST_W = REST_KR + LANE

VMEM_LIMIT = 56 * 1024 * 1024


def _params(*sem):
    return pltpu.CompilerParams(dimension_semantics=sem, vmem_limit_bytes=VMEM_LIMIT)


def _rms(xf, g):
    return xf * lax.rsqrt(jnp.mean(xf * xf, axis=-1, keepdims=True) + EPS) * g


def _dot_nt(a, b_t):
    return lax.dot_general(a, b_t, (((1,), (1,)), ((), ())), preferred_element_type=F32)


FFN_PREFETCH_STEP = 1


def _ffn_body(x_hbm, g_ref, wg_ref, wu_ref, wd_ref, o_hbm, acc_ref, h_ref, in_sem, out_sem):
    i, f = pl.program_id(0), pl.program_id(1)
    ni, nf = pl.num_programs(0), pl.num_programs(1)
    tm = acc_ref.shape[1]
    slot = lax.rem(i, 2)
    other = 1 - slot

    def x_copy(tile, buf):
        return pltpu.make_async_copy(x_hbm.at[pl.ds(tile * tm, tm), :], acc_ref.at[buf], in_sem.at[buf])

    def out_copy(tile, buf):
        return pltpu.make_async_copy(acc_ref.at[buf], o_hbm.at[pl.ds(tile * tm, tm), :], out_sem.at[buf])

    @pl.when(f == 0)
    def _():
        @pl.when(i == 0)
        def _():
            x_copy(i, slot).start()
        x_copy(i, slot).wait()
        h_ref[...] = _rms(acc_ref[slot], g_ref[...]).astype(BF16)

    @pl.when((f == FFN_PREFETCH_STEP) & (i + 1 < ni))
    def _():
        @pl.when(i > 0)
        def _():
            out_copy(i - 1, other).wait()
        x_copy(i + 1, other).start()

    h = h_ref[...]
    g = jnp.dot(h, wg_ref[...].astype(BF16), preferred_element_type=F32)
    u = jnp.dot(h, wu_ref[...].astype(BF16), preferred_element_type=F32)
    a = (0.5 * (jax.nn.silu(g) * u)).astype(BF16)
    acc_ref[slot] += jnp.dot(a, wd_ref[...].astype(BF16), preferred_element_type=F32)

    @pl.when(f == nf - 1)
    def _():
        out_copy(i, slot).start()

        @pl.when(i == ni - 1)
        def _():
            @pl.when(ni > 1)
            def _():
                out_copy(i - 1, other).wait()
            out_copy(i, slot).wait()


def ffn(x, norm_g, wg, wu, wd, layer, *, tm=1024, tf=512):
    s, d = x.shape
    f = wg.shape[2]
    assert f // tf > FFN_PREFETCH_STEP
    return pl.pallas_call(
        _ffn_body,
        grid=(s // tm, f // tf),
        in_specs=[
            pl.BlockSpec(memory_space=pl.ANY),
            pl.BlockSpec((1, d), lambda i, j: (0, 0)),
            pl.BlockSpec((None, d, tf), lambda i, j: (layer, 0, j)),
            pl.BlockSpec((None, d, tf), lambda i, j: (layer, 0, j)),
            pl.BlockSpec((None, tf, d), lambda i, j: (layer, j, 0)),
        ],
        out_specs=pl.BlockSpec(memory_space=pl.ANY),
        out_shape=jax.ShapeDtypeStruct((s, d), F32),
        scratch_shapes=[pltpu.VMEM((2, tm, d), F32), pltpu.VMEM((tm, d), BF16),
                        pltpu.SemaphoreType.DMA((2,)), pltpu.SemaphoreType.DMA((2,))],
        compiler_params=_params("arbitrary", "arbitrary"),
        name="ffn",
    )(x, norm_g.reshape(1, d), wg, wu, wd)


def _proj_rest_body(x_ref, g_ref, wp_ref, ws_ref, wq_ref, wkv_ref, wkr_ref, p_ref, h_ref):
    h = _rms(x_ref[...], g_ref[...]).astype(BF16)
    h_ref[...] = h
    for w_ref, off in ((wp_ref, REST_POOL), (ws_ref, REST_SGU), (wq_ref, REST_CQ), (wkv_ref, REST_CKV),
                       (wkr_ref, REST_KR)):
        p_ref[:, off:off + w_ref.shape[0]] = _dot_nt(h, w_ref[...])


def proj_rest(x, norm_g, w_main, layer, *, tm=512):
    s, d = x.shape

    def wspec(off, width):
        return pl.BlockSpec((None, width, d), lambda i: (layer, off // width, 0))

    return pl.pallas_call(
        _proj_rest_body,
        grid=(s // tm,),
        in_specs=[
            pl.BlockSpec((tm, d), lambda i: (i, 0)),
            pl.BlockSpec((1, d), lambda i: (0, 0)),
            wspec(OFF_POOL, BRANCH_W),
            wspec(OFF_SGU, 2 * BRANCH_W),
            wspec(OFF_MLA_Q, MLA_Q_LORA),
            wspec(OFF_MLA_KV, MLA_KV_LORA),
            wspec(OFF_MLA_KR, LANE),
        ],
        out_specs=[
            pl.BlockSpec((tm, REST_W), lambda i: (i, 0)),
            pl.BlockSpec((tm, d), lambda i: (i, 0)),
        ],
        out_shape=[jax.ShapeDtypeStruct((s, REST_W), F32), jax.ShapeDtypeStruct((s, d), BF16)],
        compiler_params=_params("parallel"),
        name="proj_rest",
    )(x, norm_g.reshape(1, d), w_main, w_main, w_main, w_main, w_main)


DIL_SCRATCH_PITCH = {1: 1, 4: 4, 16: 24}


def _proj_dil_body(h_ref, w0_ref, w1_ref, w2_ref, g_ref, o0_ref, o1_ref, o2_ref, scr1_ref, scr2_ref):
    normed = pl.program_id(0) < 2
    h = h_ref[...]
    tm = h.shape[0]
    for w_ref, o_ref, scr_ref, (_, d) in zip((w0_ref, w1_ref, w2_ref), (o0_ref, o1_ref, o2_ref),
                                             (None, scr1_ref, scr2_ref), DIL_PATTERNS):
        pitch = DIL_SCRATCH_PITCH[d]
        acc = _dot_nt(h, w_ref[...])
        for a in range(DIL_HEADS_PER_GROUP):
            sl = slice(a * DIL_HEAD_DIM, (a + 1) * DIL_HEAD_DIM)
            xh = acc[:, sl]
            r = lax.rsqrt(jnp.mean(xh * xh, axis=-1, keepdims=True) + EPS)
            y = xh * jnp.where(normed, r, 1.0) * g_ref[:, sl]
            if d == 1:
                o_ref[0, :, sl] = y.astype(BF16)
            elif pitch == d:
                scr_ref[a] = y
            else:
                for k in range(tm // d):
                    scr_ref[a, k * pitch:k * pitch + d, :] = y[k * d:(k + 1) * d, :]
        if d > 1:
            for r in range(d):
                for a in range(DIL_HEADS_PER_GROUP):
                    sl = slice(a * DIL_HEAD_DIM, (a + 1) * DIL_HEAD_DIM)
                    o_ref[r, :, sl] = scr_ref[a, pl.ds(r, tm // d, stride=pitch), :].astype(BF16)


def proj_dil(h, w_main, gains, layer, *, tm=1024):
    s, dm = h.shape
    first = OFF_DIL // BRANCH_W

    def wspec(g):
        return pl.BlockSpec((None, BRANCH_W, dm), lambda j, i: (layer, first + j * DIL_GROUPS + g, 0))

    out_specs, out_shapes, scratch = [], [], []
    for _, d in DIL_PATTERNS:
        out_specs.append(pl.BlockSpec((d, tm // d, BRANCH_W), lambda j, i: (0, i, j)))
        out_shapes.append(jax.ShapeDtypeStruct((d, s // d, 3 * BRANCH_W), BF16))
        if d > 1:
            scratch.append(pltpu.VMEM((DIL_HEADS_PER_GROUP, (tm // d) * DIL_SCRATCH_PITCH[d], DIL_HEAD_DIM), F32))
    return pl.pallas_call(
        _proj_dil_body,
        grid=(3, s // tm),
        in_specs=[
            pl.BlockSpec((tm, dm), lambda j, i: (i, 0)),
            wspec(0), wspec(1), wspec(2),
            pl.BlockSpec((None, 1, BRANCH_W), lambda j, i: (j, 0, 0)),
        ],
        out_specs=out_specs,
        out_shape=out_shapes,
        scratch_shapes=scratch,
        compiler_params=_params("arbitrary", "arbitrary"),
        name="proj_dil",
    )(h, w_main, w_main, w_main, gains)


def _pool_body(prev_ref, cur_ref, next_ref, w_ref, scale_ref, o_ref, ext_ref, *, seq):
    i = pl.program_id(0)
    tm = cur_ref.shape[0]
    cur = cur_ref[...]
    ext_ref[0:POOL_HALO, :] = jnp.where(i > 0, prev_ref[...], 0.0)
    ext_ref[POOL_HALO:POOL_HALO + tm, :] = cur
    ext_ref[POOL_HALO + tm:2 * POOL_HALO + tm, :] = jnp.where(
        i < pl.num_programs(0) - 1, next_ref[...], 0.0)
    t = i * tm + lax.broadcasted_iota(jnp.int32, (tm, 1), 0)
    for g, w in enumerate(POOL_WINDOWS):
        sl = slice(g * LANE, (g + 1) * LANE)
        tot = jnp.zeros((tm, LANE), F32)
        for o in range(-(w // 2), w // 2):
            tot = tot + ext_ref[POOL_HALO + o:POOL_HALO + o + tm, sl]
        cnt = (jnp.minimum(t + w // 2, seq) - jnp.maximum(t - w // 2, 0)).astype(F32)
        dev = (tot / cnt - cur[:, sl]).astype(BF16)
        y = jnp.dot(dev, w_ref[g], preferred_element_type=F32) * scale_ref[:, sl]
        o_ref[:, sl] = y.astype(BF16)


def pool_mixer(p_rest, pool_w, pool_scale, layer, *, tm=512):
    s = p_rest.shape[0]
    hb = tm // POOL_HALO
    last = s // POOL_HALO - 1
    n_win = len(POOL_WINDOWS)
    return pl.pallas_call(
        functools.partial(_pool_body, seq=s),
        grid=(s // tm,),
        in_specs=[
            pl.BlockSpec((POOL_HALO, BRANCH_W), lambda i: (jnp.maximum(i * hb - 1, 0), 0)),
            pl.BlockSpec((tm, BRANCH_W), lambda i: (i, 0)),
            pl.BlockSpec((POOL_HALO, BRANCH_W), lambda i: (jnp.minimum((i + 1) * hb, last), 0)),
            pl.BlockSpec((None, n_win, LANE, LANE), lambda i: (layer, 0, 0, 0)),
            pl.BlockSpec((1, BRANCH_W), lambda i: (0, 0)),
        ],
        out_specs=pl.BlockSpec((tm, BRANCH_W), lambda i: (i, 0)),
        out_shape=jax.ShapeDtypeStruct((s, BRANCH_W), BF16),
        scratch_shapes=[pltpu.VMEM((tm + 2 * POOL_HALO, BRANCH_W), F32)],
        compiler_params=_params("parallel"),
        name="pool_mixer",
    )(p_rest, p_rest, p_rest, pool_w, pool_scale.reshape(1, BRANCH_W))


def _alibi_slopes():
    n = DIL_HEADS
    return np.exp2(np.float32(-8.0) * np.arange(1, n + 1, dtype=np.float32) / np.float32(n))


LSE_LANES = LANE // DIL_HEADS_PER_GROUP


def _dil_attn_body(cur_ref, prev_ref, next_ref, o_ref, lse_ref, *, sub_len, penalties):
    i = pl.program_id(1)
    t = cur_ref.shape[0]
    blk = prev_ref.shape[0]
    nk = blk + 2 * DIL_SIDE
    row = lax.broadcasted_iota(jnp.int32, (blk, nk), 0)
    col = lax.broadcasted_iota(jnp.int32, (blk, nk), 1)
    dist = jnp.abs(col - DIL_SIDE - row)
    in_band = dist <= DIL_SIDE
    distf = dist.astype(F32)
    lane_head = lax.broadcasted_iota(jnp.int32, (1, LANE), 1) // LSE_LANES
    lse_tiles = [jnp.zeros((blk, LANE), F32) for _ in range(t // blk)]
    for h in range(DIL_HEADS_PER_GROUP):
        ql = slice(h * DIL_HEAD_DIM, (h + 1) * DIL_HEAD_DIM)
        kl = slice(BRANCH_W + h * DIL_HEAD_DIM, BRANCH_W + (h + 1) * DIL_HEAD_DIM)
        vl = slice(2 * BRANCH_W + h * DIL_HEAD_DIM, 2 * BRANCH_W + (h + 1) * DIL_HEAD_DIM)
        k_ext = jnp.concatenate([prev_ref[blk - DIL_SIDE:, kl], cur_ref[:, kl], next_ref[:DIL_SIDE, kl]], axis=0)
        v_ext = jnp.concatenate([prev_ref[blk - DIL_SIDE:, vl], cur_ref[:, vl], next_ref[:DIL_SIDE, vl]], axis=0)
        bias = jnp.where(in_band, -penalties[h] * distf, NEG_BIG)
        for b in range(t // blk):
            rows = slice(b * blk, (b + 1) * blk)
            kpos = i * t + b * blk + col - DIL_SIDE
            s = lax.dot_general(cur_ref[rows, ql], k_ext[b * blk:b * blk + nk], (((1,), (1,)), ((), ())),
                                preferred_element_type=F32) * (DIL_HEAD_DIM ** -0.5)
            s = jnp.where((kpos >= 0) & (kpos < sub_len), s + bias, NEG_BIG)
            m = jnp.max(s, axis=-1, keepdims=True)
            p = jnp.exp(s - m)
            den = jnp.sum(p, axis=-1, keepdims=True)
            o_ref[rows, ql] = jnp.dot((p / den).astype(BF16), v_ext[b * blk:b * blk + nk],
                                      preferred_element_type=F32).astype(BF16)
            lse_tiles[b] = jnp.where(lane_head == h, m + jnp.log(den), lse_tiles[b])
    for b, tile in enumerate(lse_tiles):
        lse_ref[b * blk:(b + 1) * blk, :] = tile


def dil_attn(qkv, group, *, blk=128, max_blocks=8):
    d, sub_len, width = qkv.shape
    n_blk = sub_len // blk
    qb = min(max_blocks, n_blk)
    t = blk * qb
    slopes = _alibi_slopes()[group * DIL_HEADS_PER_GROUP:(group + 1) * DIL_HEADS_PER_GROUP]
    penalties = tuple(float(np.float32(x) * np.float32(d)) for x in slopes)

    def halo(step):
        return pl.BlockSpec((None, blk, width), lambda r, i: (r, jnp.clip(i * qb + step, 0, n_blk - 1), 0))

    return pl.pallas_call(
        functools.partial(_dil_attn_body, sub_len=sub_len, penalties=penalties),
        grid=(d, sub_len // t),
        in_specs=[pl.BlockSpec((None, t, width), lambda r, i: (r, i, 0)), halo(-1), halo(qb)],
        out_specs=[pl.BlockSpec((None, t, BRANCH_W), lambda r, i: (r, i, 0)),
                   pl.BlockSpec((None, t, LANE), lambda r, i: (r, i, 0))],
        out_shape=[jax.ShapeDtypeStruct((d, sub_len, BRANCH_W), BF16),
                   jax.ShapeDtypeStruct((d, sub_len, LANE), F32)],
        compiler_params=_params("parallel", "parallel"),
        name=f"dil_attn_g{group}",
    )(qkv, qkv, qkv)


def _dil_combine_body(o0, o1, o2, l0, l1, l2, z_ref, so1, so2, sl1, sl2):
    tm = z_ref.shape[0]
    for src, scr, (_, d) in ((o1, so1, DIL_PATTERNS[1]), (o2, so2, DIL_PATTERNS[2])):
        for r in range(d):
            for a in range(DIL_HEADS_PER_GROUP):
                sl = slice(a * DIL_HEAD_DIM, (a + 1) * DIL_HEAD_DIM)
                scr[a, pl.ds(r, tm // d, stride=d), :] = src[r, :, sl].astype(F32)
    for src, scr, (_, d) in ((l1, sl1, DIL_PATTERNS[1]), (l2, sl2, DIL_PATTERNS[2])):
        for r in range(d):
            scr[pl.ds(r, tm // d, stride=d), :] = src[r]
    la, lb, lc = l0[0], sl1[...], sl2[...]
    m = jnp.maximum(jnp.maximum(la, lb), lc)
    ea, eb, ec = jnp.exp(la - m), jnp.exp(lb - m), jnp.exp(lc - m)
    tot = ea + eb + ec
    wa, wb, wc = ea / tot, eb / tot, ec / tot
    for a in range(DIL_HEADS_PER_GROUP):
        sl = slice(a * DIL_HEAD_DIM, (a + 1) * DIL_HEAD_DIM)
        one = slice(a * LSE_LANES, a * LSE_LANES + 1)
        shape = (tm, DIL_HEAD_DIM)
        z = (jnp.broadcast_to(wa[:, one], shape) * o0[0, :, sl].astype(F32)
             + jnp.broadcast_to(wb[:, one], shape) * so1[a]
             + jnp.broadcast_to(wc[:, one], shape) * so2[a])
        z_ref[:, sl] = z.astype(BF16)


def dil_combine(outs, lses, *, tm=512):
    s = outs[0].shape[1]
    o_specs = [pl.BlockSpec((d, tm // d, BRANCH_W), lambda i: (0, i, 0)) for _, d in DIL_PATTERNS]
    l_specs = [pl.BlockSpec((d, tm // d, LANE), lambda i: (0, i, 0)) for _, d in DIL_PATTERNS]
    slab = pltpu.VMEM((DIL_HEADS_PER_GROUP, tm, DIL_HEAD_DIM), F32)
    return pl.pallas_call(
        _dil_combine_body,
        grid=(s // tm,),
        in_specs=o_specs + l_specs,
        out_specs=pl.BlockSpec((tm, BRANCH_W), lambda i: (i, 0)),
        out_shape=jax.ShapeDtypeStruct((s, BRANCH_W), BF16),
        scratch_shapes=[slab, slab, pltpu.VMEM((tm, LANE), F32), pltpu.VMEM((tm, LANE), F32)],
        compiler_params=_params("parallel"),
        name="dil_combine",
    )(*outs, *lses)


def _sgu_body(u_ref, v_ref, g_ref, ws_ref, b_ref, o_ref):
    tm = u_ref.shape[0]
    u = jax.nn.gelu(u_ref[...])
    vn = _rms(jax.nn.gelu(v_ref[...]), g_ref[...]).astype(BF16)
    for n in range(tm // SGU_CHUNK):
        rows = slice(n * SGU_CHUNK, (n + 1) * SGU_CHUNK)
        for g in range(SGU_GROUPS):
            cols = slice(g * LANE, (g + 1) * LANE)
            mixed = jnp.dot(ws_ref[g], vn[rows, cols], preferred_element_type=F32) + b_ref[g]
            o_ref[rows, cols] = (u[rows, cols] * mixed).astype(BF16)


def sgu_mixer(p_rest, v_norm_g, ws, b_full, layer, *, tm=512):
    s = p_rest.shape[0]
    return pl.pallas_call(
        _sgu_body,
        grid=(s // tm,),
        in_specs=[
            pl.BlockSpec((tm, BRANCH_W), lambda i: (i, REST_SGU // BRANCH_W)),
            pl.BlockSpec((tm, BRANCH_W), lambda i: (i, REST_SGU // BRANCH_W + 1)),
            pl.BlockSpec((1, BRANCH_W), lambda i: (0, 0)),
            pl.BlockSpec((None, SGU_GROUPS, SGU_CHUNK, SGU_CHUNK), lambda i: (layer, 0, 0, 0)),
            pl.BlockSpec((SGU_GROUPS, SGU_CHUNK, LANE), lambda i: (0, 0, 0)),
        ],
        out_specs=pl.BlockSpec((tm, BRANCH_W), lambda i: (i, 0)),
        out_shape=jax.ShapeDtypeStruct((s, BRANCH_W), BF16),
        compiler_params=_params("parallel"),
        name="sgu_mixer",
    )(p_rest, p_rest, v_norm_g.reshape(1, BRANCH_W), ws, b_full)


def _swap_rope_halves(y):
    width = y.shape[1]
    lane = lax.broadcasted_iota(jnp.int32, y.shape, 1)
    first_half = (lane % MLA_ROPE) < MLA_ROPE // 2
    return jnp.where(first_half, pltpu.roll(y, width - MLA_ROPE // 2, 1),
                     pltpu.roll(y, MLA_ROPE // 2, 1))


def _mla_prep_body(cq_ref, ckv_ref, kr_ref, cos_ref, sin_ref, qa_g, wuq_ref, kva_g, wukv_ref,
                   gqn_ref, gqr_ref, gkn_ref, gkr_ref, q_ref, k_ref, v_ref):
    nope_w = MLA_HEADS * MLA_NOPE
    qall = jnp.dot(_rms(cq_ref[...], qa_g[...]).astype(BF16), wuq_ref[...], preferred_element_type=F32)
    kv = jnp.dot(_rms(ckv_ref[...], kva_g[...]).astype(BF16), wukv_ref[...], preferred_element_type=F32)
    qn, qr = qall[:, :nope_w], qall[:, nope_w:]
    kn = kv[:, :nope_w]
    v_ref[...] = kv[:, nope_w:].astype(BF16)
    kr_blk = kr_ref[...]
    lane128 = lax.broadcasted_iota(jnp.int32, kr_blk.shape, 1)
    kr2 = jnp.where(lane128 < MLA_ROPE, kr_blk, pltpu.roll(kr_blk, MLA_ROPE, 1))
    kr = jnp.concatenate([kr2] * (MLA_HEADS // 2), axis=1)

    rope_w = MLA_HEADS * MLA_ROPE
    head_of_lane = lax.broadcasted_iota(jnp.int32, (1, rope_w), 1) // MLA_ROPE
    qr_sq = qr * qr
    kr_ssq = jnp.sum(jnp.where(head_of_lane == 0, kr * kr, 0.0), axis=-1, keepdims=True)
    rq, rk = [], []
    for h in range(MLA_HEADS):
        sl = slice(h * MLA_NOPE, (h + 1) * MLA_NOPE)
        q_ssq = (jnp.sum(qn[:, sl] * qn[:, sl], axis=-1, keepdims=True)
                 + jnp.sum(jnp.where(head_of_lane == h, qr_sq, 0.0), axis=-1, keepdims=True))
        k_ssq = jnp.sum(kn[:, sl] * kn[:, sl], axis=-1, keepdims=True) + kr_ssq
        rq.append(lax.rsqrt(q_ssq / MLA_QK + EPS))
        rk.append(lax.rsqrt(k_ssq / MLA_QK + EPS))

    def per_lane(rs):
        out = jnp.zeros((kr.shape[0], rope_w), F32)
        for h in range(MLA_HEADS):
            out = jnp.where(head_of_lane == h, rs[h], out)
        return out

    def rotate(x, rs, g_ref):
        y = x * per_lane(rs) * g_ref[...]
        return y * cos_ref[...] + _swap_rope_halves(y) * sin_ref[...]

    q_rope = rotate(qr, rq, gqr_ref) * MLA_EXP2_SCALE
    k_rope = rotate(kr, rk, gkr_ref)
    lane_tile = lax.broadcasted_iota(jnp.int32, (1, LANE), 1) // MLA_ROPE
    for h in range(MLA_HEADS):
        sl = slice(h * MLA_NOPE, (h + 1) * MLA_NOPE)
        base = h * MLA_PAD_QK
        tile = slice((h // 2) * LANE, (h // 2 + 1) * LANE)
        q_ref[:, base:base + MLA_NOPE] = (qn[:, sl] * rq[h] * gqn_ref[:, sl] * MLA_EXP2_SCALE).astype(BF16)
        k_ref[:, base:base + MLA_NOPE] = (kn[:, sl] * rk[h] * gkn_ref[:, sl]).astype(BF16)
        q_ref[:, base + MLA_NOPE:base + MLA_PAD_QK] = jnp.where(
            lane_tile == h % 2, q_rope[:, tile], 0.0).astype(BF16)
        k_ref[:, base + MLA_NOPE:base + MLA_PAD_QK] = k_rope[:, tile].astype(BF16)


def mla_prep(p_rest, cos4, sin4, qa_g, wuq, kva_g, wukv, gqn, gqr, gkn, gkr, *, tm=512):
    s = p_rest.shape[0]
    rope_w = MLA_HEADS * MLA_ROPE

    def const(shape):
        return pl.BlockSpec(shape, lambda i: (0,) * len(shape))

    return pl.pallas_call(
        _mla_prep_body,
        grid=(s // tm,),
        in_specs=[
            pl.BlockSpec((tm, MLA_Q_LORA), lambda i: (i, REST_CQ // MLA_Q_LORA)),
            pl.BlockSpec((tm, MLA_KV_LORA), lambda i: (i, REST_CKV // MLA_KV_LORA)),
            pl.BlockSpec((tm, LANE), lambda i: (i, REST_KR // LANE)),
            pl.BlockSpec((tm, rope_w), lambda i: (i, 0)),
            pl.BlockSpec((tm, rope_w), lambda i: (i, 0)),
            const((1, MLA_Q_LORA)), const(wuq.shape), const((1, MLA_KV_LORA)), const(wukv.shape),
            const(gqn.shape), const(gqr.shape), const(gkn.shape), const(gkr.shape),
        ],
        out_specs=[
            pl.BlockSpec((tm, MLA_HEADS * MLA_PAD_QK), lambda i: (i, 0)),
            pl.BlockSpec((tm, MLA_HEADS * MLA_PAD_QK), lambda i: (i, 0)),
            pl.BlockSpec((tm, MLA_HEADS * MLA_V), lambda i: (i, 0)),
        ],
        out_shape=[
            jax.ShapeDtypeStruct((s, MLA_HEADS * MLA_PAD_QK), BF16),
            jax.ShapeDtypeStruct((s, MLA_HEADS * MLA_PAD_QK), BF16),
            jax.ShapeDtypeStruct((s, MLA_HEADS * MLA_V), BF16),
        ],
        compiler_params=_params("parallel"),
        name="mla_prep",
    )(p_rest, p_rest, p_rest, cos4, sin4, qa_g.reshape(1, -1), wuq, kva_g.reshape(1, -1), wukv,
      gqn, gqr, gkn, gkr)


def _mla_attn_body(q_ref, k_ref, v_ref, o_ref, m_ref, l_ref, acc_ref):
    j = pl.program_id(2)

    @pl.when(j == 0)
    def _():
        m_ref[...] = jnp.full(m_ref.shape, NEG_BIG, F32)
        l_ref[...] = jnp.zeros(l_ref.shape, F32)
        acc_ref[...] = jnp.zeros(acc_ref.shape, F32)

    tk = k_ref.shape[0]
    for h in range(MLA_HEADS_PER_STEP):
        qk = slice(h * MLA_PAD_QK, (h + 1) * MLA_PAD_QK)
        vs = slice(h * MLA_V, (h + 1) * MLA_V)
        s = lax.dot_general(q_ref[:, qk], k_ref[:, qk], (((1,), (1,)), ((), ())),
                            preferred_element_type=F32)
        m_prev = m_ref[h]
        m_new = jnp.maximum(m_prev, jnp.max(s, axis=-1, keepdims=True))
        alpha = jnp.exp2(m_prev - m_new)
        p = jnp.exp2(s - jnp.tile(m_new, (1, tk // LANE)))
        l_ref[h] = alpha * l_ref[h] + jnp.sum(p, axis=-1, keepdims=True)
        acc_ref[h] = alpha * acc_ref[h] + jnp.dot(p.astype(BF16), v_ref[:, vs], preferred_element_type=F32)
        m_ref[h] = m_new

    @pl.when(j == pl.num_programs(2) - 1)
    def _():
        for h in range(MLA_HEADS_PER_STEP):
            o_ref[:, h * MLA_V:(h + 1) * MLA_V] = (acc_ref[h] / l_ref[h]).astype(BF16)


def mla_attn(q, k, v, *, tq=1024, tk=2048):
    s = q.shape[0]
    hp = MLA_HEADS_PER_STEP
    stat = pltpu.VMEM((hp, tq, LANE), F32)
    return pl.pallas_call(
        _mla_attn_body,
        grid=(s // tq, MLA_HEADS // hp, s // tk),
        in_specs=[
            pl.BlockSpec((tq, hp * MLA_PAD_QK), lambda i, h, j: (i, h)),
            pl.BlockSpec((tk, hp * MLA_PAD_QK), lambda i, h, j: (j, h)),
            pl.BlockSpec((tk, hp * MLA_V), lambda i, h, j: (j, h)),
        ],
        out_specs=pl.BlockSpec((tq, hp * MLA_V), lambda i, h, j: (i, h)),
        out_shape=jax.ShapeDtypeStruct((s, MLA_HEADS * MLA_V), BF16),
        scratch_shapes=[stat, stat, stat],
        compiler_params=_params("parallel", "parallel", "arbitrary"),
        name="mla_attn",
    )(q, k, v)


def _merge_body(h_ref, *refs):
    z_refs, wg_refs, wb_refs = refs[:N_BRANCH], refs[N_BRANCH:2 * N_BRANCH], refs[2 * N_BRANCH:3 * N_BRANCH]
    o_ref = refs[3 * N_BRANCH]
    h = h_ref[...]
    acc = None
    for z_ref, wg_ref, wb_ref in zip(z_refs, wg_refs, wb_refs):
        gate = jax.nn.sigmoid(_dot_nt(h, wg_ref[0]))
        term = gate * jnp.dot(z_ref[...], wb_ref[...].astype(BF16), preferred_element_type=F32)
        acc = term if acc is None else acc + term
    o_ref[...] = acc.astype(BF16)


GATE_ROW_ALIGN = 64


def merge(h, zs, wt, w_branch, layer, *, tm=1024, tn=512):
    s, d = h.shape
    ct = d // tn
    assert OFF_GATE % GATE_ROW_ALIGN == 0 and d % GATE_ROW_ALIGN == 0 and tn % GATE_ROW_ALIGN == 0
    z_spec = pl.BlockSpec((tm, BRANCH_W), lambda i, c: (i, 0))
    g_specs = [pl.BlockSpec((pl.Element(1), pl.Element(tn), pl.Element(d)),
                            lambda i, c, n=n: (layer, pl.multiple_of(OFF_GATE + n * d + c * tn, GATE_ROW_ALIGN), 0))
               for n in range(N_BRANCH)]
    b_specs = [pl.BlockSpec((None, None, BRANCH_W, tn), lambda i, c, n=n: (layer, n, 0, c))
               for n in range(N_BRANCH)]
    return pl.pallas_call(
        _merge_body,
        grid=(s // tm, ct),
        in_specs=[pl.BlockSpec((tm, d), lambda i, c: (i, 0))] + [z_spec] * N_BRANCH + g_specs + b_specs,
        out_specs=pl.BlockSpec((tm, tn), lambda i, c: (i, c)),
        out_shape=jax.ShapeDtypeStruct((s, d), BF16),
        compiler_params=_params("parallel", "arbitrary"),
        name="merge",
    )(h, *zs, *([wt] * N_BRANCH), *([w_branch] * N_BRANCH))


def _out_proj_body(x_ref, y_ref, w_ref, o_ref):
    o_ref[...] = x_ref[...] + jnp.dot(y_ref[...], w_ref[...].astype(BF16), preferred_element_type=F32)


def out_proj(x, y, w, layer, *, tm=1024, tn=1024):
    s, d = x.shape
    return pl.pallas_call(
        _out_proj_body,
        grid=(d // tn, s // tm),
        in_specs=[
            pl.BlockSpec((tm, tn), lambda j, i: (i, j)),
            pl.BlockSpec((tm, d), lambda j, i: (i, 0)),
            pl.BlockSpec((None, d, tn), lambda j, i: (layer, 0, j)),
        ],
        out_specs=pl.BlockSpec((tm, tn), lambda j, i: (i, j)),
        out_shape=jax.ShapeDtypeStruct((s, d), F32),
        compiler_params=_params("arbitrary", "arbitrary"),
        name="out_proj",
    )(x, y, w)


def _cast_body(x_ref, o_ref):
    o_ref[...] = x_ref[...].astype(BF16)


def cast_rows(wt, *, tr=1024):
    nl, n, d = wt.shape
    return pl.pallas_call(
        _cast_body,
        grid=(nl, pl.cdiv(n, tr)),
        in_specs=[pl.BlockSpec((None, tr, d), lambda l, r: (l, r, 0))],
        out_specs=pl.BlockSpec((None, tr, d), lambda l, r: (l, r, 0)),
        out_shape=jax.ShapeDtypeStruct((nl, n, d), BF16),
        compiler_params=_params("parallel", "parallel"),
        name="cast_rows",
    )(wt)


def _rope_tables(s):
    pos = jnp.arange(s, dtype=F32)
    inv_freq = ROPE_THETA ** (-jnp.arange(0, MLA_ROPE, 2, dtype=F32) / MLA_ROPE)
    ang = pos[:, None] * inv_freq[None, :]
    cos, sin = jnp.cos(ang), jnp.sin(ang)
    cos4 = jnp.tile(jnp.concatenate([cos, cos], axis=-1), (1, MLA_HEADS))
    sin4 = jnp.tile(jnp.concatenate([-sin, sin], axis=-1), (1, MLA_HEADS))
    return cos4, sin4


def _split_heads(w, widths):
    per_head = sum(widths)
    w3 = w.reshape(w.shape[0], MLA_HEADS, per_head)
    parts, off = [], 0
    for wd in widths:
        parts.append(w3[:, :, off:off + wd].reshape(w.shape[0], MLA_HEADS * wd))
        off += wd
    return jnp.concatenate(parts, axis=1)


def kernel(x, ffn1_norm, ffn1_w_gate, ffn1_w_up, ffn1_w_down, mix_norm, w_in, pool_w, pool_scale,
           dil_q_norm, dil_k_norm, sgu_v_norm, sgu_w, sgu_b, mla_q_a_norm, mla_w_uq, mla_kv_a_norm,
           mla_w_ukv, mla_q_norm, mla_k_norm, w_branch, w_out, ffn2_norm, ffn2_w_gate, ffn2_w_up,
           ffn2_w_down):
    b, s, d = x.shape
    cos4, sin4 = _rope_tables(s)
    w_main = cast_rows(jnp.swapaxes(w_in, 1, 2))
    pool_wb, sgu_wb = pool_w.astype(BF16), sgu_w.astype(BF16)
    ones = jnp.ones((BRANCH_W,), F32)

    outs = []
    for bi in range(b):
        xb = x[bi]
        for l in range(DEPTH):
            xb = ffn(xb, ffn1_norm[l], ffn1_w_gate, ffn1_w_up, ffn1_w_down, l)

            p_rest, h = proj_rest(xb, mix_norm[l], w_main, l)
            dil_gains = jnp.stack([jnp.tile(dil_q_norm[l], DIL_HEADS_PER_GROUP),
                                   jnp.tile(dil_k_norm[l], DIL_HEADS_PER_GROUP), ones]).reshape(3, 1, BRANCH_W)
            qkvs = proj_dil(h, w_main, dil_gains, l)
            z_a = pool_mixer(p_rest, pool_wb, pool_scale[l], l)
            o_l = [dil_attn(qkvs[g], g) for g in range(DIL_GROUPS)]
            z_b = dil_combine([o for o, _ in o_l], [ls for _, ls in o_l])
            b_full = jnp.broadcast_to(sgu_b[l][:, :, None], (SGU_GROUPS, SGU_CHUNK, LANE))
            z_c = sgu_mixer(p_rest, sgu_v_norm[l], sgu_wb, b_full, l)
            q, k, v = mla_prep(
                p_rest, cos4, sin4, mla_q_a_norm[l],
                _split_heads(mla_w_uq[l], (MLA_NOPE, MLA_ROPE)).astype(BF16),
                mla_kv_a_norm[l], _split_heads(mla_w_ukv[l], (MLA_NOPE, MLA_V)).astype(BF16),
                jnp.tile(mla_q_norm[l][:MLA_NOPE], MLA_HEADS).reshape(1, -1),
                jnp.tile(mla_q_norm[l][MLA_NOPE:], MLA_HEADS).reshape(1, -1),
                jnp.tile(mla_k_norm[l][:MLA_NOPE], MLA_HEADS).reshape(1, -1),
                jnp.tile(mla_k_norm[l][MLA_NOPE:], MLA_HEADS).reshape(1, -1))
            z_d = mla_attn(q, k, v)
            merged = merge(h, (z_a, z_b, z_c, z_d), w_main, w_branch, l)
            xb = out_proj(xb, merged, w_out, l)

            xb = ffn(xb, ffn2_norm[l], ffn2_w_gate, ffn2_w_up, ffn2_w_down, l)
        outs.append(xb)
    return jnp.stack(outs, axis=0)
```

```python
import functools
import math

import numpy as np
import jax
import jax.numpy as jnp
from jax import lax
from jax.experimental import pallas as pl
from jax.experimental.pallas import tpu as pltpu

F32 = jnp.float32
BF16 = jnp.bfloat16

D_MODEL = 2048
DEPTH = 4
D_FF = 5632
N_BRANCH = 4
BRANCH_W = 512
POOL_WINDOWS = (2, 4, 8, 16)
POOL_HALO = 8
DIL_PATTERNS = ((128, 1), (512, 4), (2048, 16))
DIL_GROUPS = len(DIL_PATTERNS)
DIL_HEADS = 12
DIL_HEADS_PER_GROUP = 4
DIL_HEAD_DIM = 128
DIL_QKV = 3 * DIL_HEADS * DIL_HEAD_DIM
DIL_SIDE = 64
SGU_CHUNK = 128
SGU_GROUPS = 4
MLA_HEADS = 4
MLA_Q_LORA = 384
MLA_KV_LORA = 128
MLA_NOPE = 128
MLA_ROPE = 64
MLA_V = 128
MLA_QK = MLA_NOPE + MLA_ROPE
MLA_PAD_QK = 256
MLA_HEADS_PER_STEP = 4
ROPE_THETA = 10000.0
EPS = 1e-6
NEG_BIG = -1e30
MLA_EXP2_SCALE = (MLA_QK ** -0.5) * math.log2(math.e)

OFF_POOL = 0
OFF_DIL = OFF_POOL + BRANCH_W
OFF_SGU = OFF_DIL + DIL_QKV
OFF_MLA_Q = OFF_SGU + 2 * BRANCH_W
OFF_MLA_KV = OFF_MLA_Q + MLA_Q_LORA
OFF_MLA_KR = OFF_MLA_KV + MLA_KV_LORA
OFF_GATE = OFF_MLA_KR + MLA_ROPE

LANE = 128
REST_POOL = 0
REST_SGU = 512
REST_CQ = 1536
REST_CKV = 1920
REST_KR = 2048
REST_W = REST_KR + LANE

VMEM_LIMIT = 56 * 1024 * 1024


def _params(*sem):
    return pltpu.CompilerParams(dimension_semantics=sem, vmem_limit_bytes=VMEM_LIMIT)


def _rms(xf, g):
    return xf * lax.rsqrt(jnp.mean(xf * xf, axis=-1, keepdims=True) + EPS) * g


def _dot_nt(a, b_t):
    return lax.dot_general(a, b_t, (((1,), (1,)), ((), ())), preferred_element_type=F32)


FFN_PREFETCH_STEP = 1


def _ffn_body(x_hbm, g_ref, wg_ref, wu_ref, wd_ref, o_hbm, acc_ref, h_ref, in_sem, out_sem):
    i, f = pl.program_id(0), pl.program_id(1)
    ni, nf = pl.num_programs(0), pl.num_programs(1)
    tm = acc_ref.shape[1]
    slot = lax.rem(i, 2)
    other = 1 - slot

    def x_copy(tile, buf):
        return pltpu.make_async_copy(x_hbm.at[pl.ds(tile * tm, tm), :], acc_ref.at[buf], in_sem.at[buf])

    def out_copy(tile, buf):
        return pltpu.make_async_copy(acc_ref.at[buf], o_hbm.at[pl.ds(tile * tm, tm), :], out_sem.at[buf])

    @pl.when(f == 0)
    def _():
        @pl.when(i == 0)
        def _():
            x_copy(i, slot).start()
        x_copy(i, slot).wait()
        h_ref[...] = _rms(acc_ref[slot], g_ref[...]).astype(BF16)

    @pl.when((f == FFN_PREFETCH_STEP) & (i + 1 < ni))
    def _():
        @pl.when(i > 0)
        def _():
            out_copy(i - 1, other).wait()
        x_copy(i + 1, other).start()

    h = h_ref[...]
    g = jnp.dot(h, wg_ref[...].astype(BF16), preferred_element_type=F32)
    u = jnp.dot(h, wu_ref[...].astype(BF16), preferred_element_type=F32)
    a = (0.5 * (jax.nn.silu(g) * u)).astype(BF16)
    acc_ref[slot] += jnp.dot(a, wd_ref[...].astype(BF16), preferred_element_type=F32)

    @pl.when(f == nf - 1)
    def _():
        out_copy(i, slot).start()

        @pl.when(i == ni - 1)
        def _():
            @pl.when(ni > 1)
            def _():
                out_copy(i - 1, other).wait()
            out_copy(i, slot).wait()


def ffn(x, norm_g, wg, wu, wd, layer, *, tm=1024, tf=512):
    s, d = x.shape
    f = wg.shape[2]
    assert f // tf > FFN_PREFETCH_STEP
    return pl.pallas_call(
        _ffn_body,
        grid=(s // tm, f // tf),
        in_specs=[
            pl.BlockSpec(memory_space=pl.ANY),
            pl.BlockSpec((1, d), lambda i, j: (0, 0)),
            pl.BlockSpec((None, d, tf), lambda i, j: (layer, 0, j)),
            pl.BlockSpec((None, d, tf), lambda i, j: (layer, 0, j)),
            pl.BlockSpec((None, tf, d), lambda i, j: (layer, j, 0)),
        ],
        out_specs=pl.BlockSpec(memory_space=pl.ANY),
        out_shape=jax.ShapeDtypeStruct((s, d), F32),
        scratch_shapes=[pltpu.VMEM((2, tm, d), F32), pltpu.VMEM((tm, d), BF16),
                        pltpu.SemaphoreType.DMA((2,)), pltpu.SemaphoreType.DMA((2,))],
        compiler_params=_params("arbitrary", "arbitrary"),
        name="ffn",
    )(x, norm_g.reshape(1, d), wg, wu, wd)


def _proj_rest_body(x_ref, g_ref, wp_ref, ws_ref, wq_ref, wkv_ref, wkr_ref, p_ref, h_ref):
    h = _rms(x_ref[...], g_ref[...]).astype(BF16)
    h_ref[...] = h
    for w_ref, off in ((wp_ref, REST_POOL), (ws_ref, REST_SGU), (wq_ref, REST_CQ), (wkv_ref, REST_CKV),
                       (wkr_ref, REST_KR)):
        p_ref[:, off:off + w_ref.shape[0]] = _dot_nt(h, w_ref[...])


def proj_rest(x, norm_g, w_main, layer, *, tm=512):
    s, d = x.shape

    def wspec(off, width):
        return pl.BlockSpec((None, width, d), lambda i: (layer, off // width, 0))

    return pl.pallas_call(
        _proj_rest_body,
        grid=(s // tm,),
        in_specs=[
            pl.BlockSpec((tm, d), lambda i: (i, 0)),
            pl.BlockSpec((1, d), lambda i: (0, 0)),
            wspec(OFF_POOL, BRANCH_W),
            wspec(OFF_SGU, 2 * BRANCH_W),
            wspec(OFF_MLA_Q, MLA_Q_LORA),
            wspec(OFF_MLA_KV, MLA_KV_LORA),
            wspec(OFF_MLA_KR, LANE),
        ],
        out_specs=[
            pl.BlockSpec((tm, REST_W), lambda i: (i, 0)),
            pl.BlockSpec((tm, d), lambda i: (i, 0)),
        ],
        out_shape=[jax.ShapeDtypeStruct((s, REST_W), F32), jax.ShapeDtypeStruct((s, d), BF16)],
        compiler_params=_params("parallel"),
        name="proj_rest",
    )(x, norm_g.reshape(1, d), w_main, w_main, w_main, w_main, w_main)


DIL_SCRATCH_PITCH = {1: 1, 4: 4, 16: 24}


def _proj_dil_body(h_ref, w0_ref, w1_ref, w2_ref, g_ref, o0_ref, o1_ref, o2_ref, scr1_ref, scr2_ref):
    normed = pl.program_id(0) < 2
    h = h_ref[...]
    tm = h.shape[0]
    for w_ref, o_ref, scr_ref, (_, d) in zip((w0_ref, w1_ref, w2_ref), (o0_ref, o1_ref, o2_ref),
                                             (None, scr1_ref, scr2_ref), DIL_PATTERNS):
        pitch = DIL_SCRATCH_PITCH[d]
        acc = _dot_nt(h, w_ref[...])
        for a in range(DIL_HEADS_PER_GROUP):
            sl = slice(a * DIL_HEAD_DIM, (a + 1) * DIL_HEAD_DIM)
            xh = acc[:, sl]
            r = lax.rsqrt(jnp.mean(xh * xh, axis=-1, keepdims=True) + EPS)
            y = xh * jnp.where(normed, r, 1.0) * g_ref[:, sl]
            if d == 1:
                o_ref[0, :, sl] = y.astype(BF16)
            elif pitch == d:
                scr_ref[a] = y
            else:
                for k in range(tm // d):
                    scr_ref[a, k * pitch:k * pitch + d, :] = y[k * d:(k + 1) * d, :]
        if d > 1:
            for r in range(d):
                for a in range(DIL_HEADS_PER_GROUP):
                    sl = slice(a * DIL_HEAD_DIM, (a + 1) * DIL_HEAD_DIM)
                    o_ref[r, :, sl] = scr_ref[a, pl.ds(r, tm // d, stride=pitch), :].astype(BF16)


def proj_dil(h, w_main, gains, layer, *, tm=1024):
    s, dm = h.shape
    first = OFF_DIL // BRANCH_W

    def wspec(g):
        return pl.BlockSpec((None, BRANCH_W, dm), lambda j, i: (layer, first + j * DIL_GROUPS + g, 0))

    out_specs, out_shapes, scratch = [], [], []
    for _, d in DIL_PATTERNS:
        out_specs.append(pl.BlockSpec((d, tm // d, BRANCH_W), lambda j, i: (0, i, j)))
        out_shapes.append(jax.ShapeDtypeStruct((d, s // d, 3 * BRANCH_W), BF16))
        if d > 1:
            scratch.append(pltpu.VMEM((DIL_HEADS_PER_GROUP, (tm // d) * DIL_SCRATCH_PITCH[d], DIL_HEAD_DIM), F32))
    return pl.pallas_call(
        _proj_dil_body,
        grid=(3, s // tm),
        in_specs=[
            pl.BlockSpec((tm, dm), lambda j, i: (i, 0)),
            wspec(0), wspec(1), wspec(2),
            pl.BlockSpec((None, 1, BRANCH_W), lambda j, i: (j, 0, 0)),
        ],
        out_specs=out_specs,
        out_shape=out_shapes,
        scratch_shapes=scratch,
        compiler_params=_params("arbitrary", "arbitrary"),
        name="proj_dil",
    )(h, w_main, w_main, w_main, gains)


def _pool_body(prev_ref, cur_ref, next_ref, w_ref, scale_ref, o_ref, ext_ref, *, seq):
    i = pl.program_id(0)
    tm = cur_ref.shape[0]
    cur = cur_ref[...]
    ext_ref[0:POOL_HALO, :] = jnp.where(i > 0, prev_ref[...], 0.0)
    ext_ref[POOL_HALO:POOL_HALO + tm, :] = cur
    ext_ref[POOL_HALO + tm:2 * POOL_HALO + tm, :] = jnp.where(
        i < pl.num_programs(0) - 1, next_ref[...], 0.0)
    t = i * tm + lax.broadcasted_iota(jnp.int32, (tm, 1), 0)
    for g, w in enumerate(POOL_WINDOWS):
        sl = slice(g * LANE, (g + 1) * LANE)
        tot = jnp.zeros((tm, LANE), F32)
        for o in range(-(w // 2), w // 2):
            tot = tot + ext_ref[POOL_HALO + o:POOL_HALO + o + tm, sl]
        cnt = (jnp.minimum(t + w // 2, seq) - jnp.maximum(t - w // 2, 0)).astype(F32)
        dev = (tot / cnt - cur[:, sl]).astype(BF16)
        y = jnp.dot(dev, w_ref[g], preferred_element_type=F32) * scale_ref[:, sl]
        o_ref[:, sl] = y.astype(BF16)


def pool_mixer(p_rest, pool_w, pool_scale, layer, *, tm=1024):
    s = p_rest.shape[0]
    hb = tm // POOL_HALO
    last = s // POOL_HALO - 1
    n_win = len(POOL_WINDOWS)
    return pl.pallas_call(
        functools.partial(_pool_body, seq=s),
        grid=(s // tm,),
        in_specs=[
            pl.BlockSpec((POOL_HALO, BRANCH_W), lambda i: (jnp.maximum(i * hb - 1, 0), 0)),
            pl.BlockSpec((tm, BRANCH_W), lambda i: (i, 0)),
            pl.BlockSpec((POOL_HALO, BRANCH_W), lambda i: (jnp.minimum((i + 1) * hb, last), 0)),
            pl.BlockSpec((None, n_win, LANE, LANE), lambda i: (layer, 0, 0, 0)),
            pl.BlockSpec((1, BRANCH_W), lambda i: (0, 0)),
        ],
        out_specs=pl.BlockSpec((tm, BRANCH_W), lambda i: (i, 0)),
        out_shape=jax.ShapeDtypeStruct((s, BRANCH_W), BF16),
        scratch_shapes=[pltpu.VMEM((tm + 2 * POOL_HALO, BRANCH_W), F32)],
        compiler_params=_params("parallel"),
        name="pool_mixer",
    )(p_rest, p_rest, p_rest, pool_w, pool_scale.reshape(1, BRANCH_W))


def _alibi_slopes():
    n = DIL_HEADS
    return np.exp2(np.float32(-8.0) * np.arange(1, n + 1, dtype=np.float32) / np.float32(n))


LSE_LANES = LANE // DIL_HEADS_PER_GROUP


def _dil_attn_body(cur_ref, prev_ref, next_ref, o_ref, lse_ref, *, sub_len, penalties):
    i = pl.program_id(1)
    t = cur_ref.shape[0]
    blk = prev_ref.shape[0]
    nk = blk + 2 * DIL_SIDE
    row = lax.broadcasted_iota(jnp.int32, (blk, nk), 0)
    col = lax.broadcasted_iota(jnp.int32, (blk, nk), 1)
    dist = jnp.abs(col - DIL_SIDE - row)
    in_band = dist <= DIL_SIDE
    distf = dist.astype(F32)
    lane_head = lax.broadcasted_iota(jnp.int32, (1, LANE), 1) // LSE_LANES
    lse_tiles = [jnp.zeros((blk, LANE), F32) for _ in range(t // blk)]
    for h in range(DIL_HEADS_PER_GROUP):
        ql = slice(h * DIL_HEAD_DIM, (h + 1) * DIL_HEAD_DIM)
        kl = slice(BRANCH_W + h * DIL_HEAD_DIM, BRANCH_W + (h + 1) * DIL_HEAD_DIM)
        vl = slice(2 * BRANCH_W + h * DIL_HEAD_DIM, 2 * BRANCH_W + (h + 1) * DIL_HEAD_DIM)
        k_ext = jnp.concatenate([prev_ref[blk - DIL_SIDE:, kl], cur_ref[:, kl], next_ref[:DIL_SIDE, kl]], axis=0)
        v_ext = jnp.concatenate([prev_ref[blk - DIL_SIDE:, vl], cur_ref[:, vl], next_ref[:DIL_SIDE, vl]], axis=0)
        bias = jnp.where(in_band, -penalties[h] * distf, NEG_BIG)
        for b in range(t // blk):
            rows = slice(b * blk, (b + 1) * blk)
            kpos = i * t + b * blk + col - DIL_SIDE
            s = lax.dot_general(cur_ref[rows, ql], k_ext[b * blk:b * blk + nk], (((1,), (1,)), ((), ())),
                                preferred_element_type=F32) * (DIL_HEAD_DIM ** -0.5)
            s = jnp.where((kpos >= 0) & (kpos < sub_len), s + bias, NEG_BIG)
            m = jnp.max(s, axis=-1, keepdims=True)
            p = jnp.exp(s - m)
            den = jnp.sum(p, axis=-1, keepdims=True)
            o_ref[rows, ql] = jnp.dot((p / den).astype(BF16), v_ext[b * blk:b * blk + nk],
                                      preferred_element_type=F32).astype(BF16)
            lse_tiles[b] = jnp.where(lane_head == h, m + jnp.log(den), lse_tiles[b])
    for b, tile in enumerate(lse_tiles):
        lse_ref[b * blk:(b + 1) * blk, :] = tile


def dil_attn(qkv, group, *, blk=128, max_blocks=16):
    d, sub_len, width = qkv.shape
    n_blk = sub_len // blk
    qb = min(max_blocks, n_blk)
    t = blk * qb
    slopes = _alibi_slopes()[group * DIL_HEADS_PER_GROUP:(group + 1) * DIL_HEADS_PER_GROUP]
    penalties = tuple(float(np.float32(x) * np.float32(d)) for x in slopes)

    def halo(step):
        return pl.BlockSpec((None, blk, width), lambda r, i: (r, jnp.clip(i * qb + step, 0, n_blk - 1), 0))

    return pl.pallas_call(
        functools.partial(_dil_attn_body, sub_len=sub_len, penalties=penalties),
        grid=(d, sub_len // t),
        in_specs=[pl.BlockSpec((None, t, width), lambda r, i: (r, i, 0)), halo(-1), halo(qb)],
        out_specs=[pl.BlockSpec((None, t, BRANCH_W), lambda r, i: (r, i, 0)),
                   pl.BlockSpec((None, t, LANE), lambda r, i: (r, i, 0))],
        out_shape=[jax.ShapeDtypeStruct((d, sub_len, BRANCH_W), BF16),
                   jax.ShapeDtypeStruct((d, sub_len, LANE), F32)],
        compiler_params=_params("parallel", "parallel"),
        name=f"dil_attn_g{group}",
    )(qkv, qkv, qkv)


def _dil_combine_body(o0, o1, o2, l0, l1, l2, z_ref, so1, so2, sl1, sl2):
    tm = z_ref.shape[0]
    for src, scr, (_, d) in ((o1, so1, DIL_PATTERNS[1]), (o2, so2, DIL_PATTERNS[2])):
        for r in range(d):
            for a in range(DIL_HEADS_PER_GROUP):
                sl = slice(a * DIL_HEAD_DIM, (a + 1) * DIL_HEAD_DIM)
                scr[a, pl.ds(r, tm // d, stride=d), :] = src[r, :, sl].astype(F32)
    for src, scr, (_, d) in ((l1, sl1, DIL_PATTERNS[1]), (l2, sl2, DIL_PATTERNS[2])):
        for r in range(d):
            scr[pl.ds(r, tm // d, stride=d), :] = src[r]
    la, lb, lc = l0[0], sl1[...], sl2[...]
    m = jnp.maximum(jnp.maximum(la, lb), lc)
    ea, eb, ec = jnp.exp(la - m), jnp.exp(lb - m), jnp.exp(lc - m)
    tot = ea + eb + ec
    wa, wb, wc = ea / tot, eb / tot, ec / tot
    for a in range(DIL_HEADS_PER_GROUP):
        sl = slice(a * DIL_HEAD_DIM, (a + 1) * DIL_HEAD_DIM)
        one = slice(a * LSE_LANES, a * LSE_LANES + 1)
        shape = (tm, DIL_HEAD_DIM)
        z = (jnp.broadcast_to(wa[:, one], shape) * o0[0, :, sl].astype(F32)
             + jnp.broadcast_to(wb[:, one], shape) * so1[a]
             + jnp.broadcast_to(wc[:, one], shape) * so2[a])
        z_ref[:, sl] = z.astype(BF16)


def dil_combine(outs, lses, *, tm=1024):
    s = outs[0].shape[1]
    o_specs = [pl.BlockSpec((d, tm // d, BRANCH_W), lambda i: (0, i, 0)) for _, d in DIL_PATTERNS]
    l_specs = [pl.BlockSpec((d, tm // d, LANE), lambda i: (0, i, 0)) for _, d in DIL_PATTERNS]
    slab = pltpu.VMEM((DIL_HEADS_PER_GROUP, tm, DIL_HEAD_DIM), F32)
    return pl.pallas_call(
        _dil_combine_body,
        grid=(s // tm,),
        in_specs=o_specs + l_specs,
        out_specs=pl.BlockSpec((tm, BRANCH_W), lambda i: (i, 0)),
        out_shape=jax.ShapeDtypeStruct((s, BRANCH_W), BF16),
        scratch_shapes=[slab, slab, pltpu.VMEM((tm, LANE), F32), pltpu.VMEM((tm, LANE), F32)],
        compiler_params=_params("parallel"),
        name="dil_combine",
    )(*outs, *lses)


def _sgu_body(u_ref, v_ref, g_ref, ws_ref, b_ref, o_ref):
    tm = u_ref.shape[0]
    u = jax.nn.gelu(u_ref[...])
    vn = _rms(jax.nn.gelu(v_ref[...]), g_ref[...]).astype(BF16)
    for n in range(tm // SGU_CHUNK):
        rows = slice(n * SGU_CHUNK, (n + 1) * SGU_CHUNK)
        for g in range(SGU_GROUPS):
            cols = slice(g * LANE, (g + 1) * LANE)
            mixed = jnp.dot(ws_ref[g], vn[rows, cols], preferred_element_type=F32) + b_ref[g]
            o_ref[rows, cols] = (u[rows, cols] * mixed).astype(BF16)


def sgu_mixer(p_rest, v_norm_g, ws, b_full, layer, *, tm=1024):
    s = p_rest.shape[0]
    return pl.pallas_call(
        _sgu_body,
        grid=(s // tm,),
        in_specs=[
            pl.BlockSpec((tm, BRANCH_W), lambda i: (i, REST_SGU // BRANCH_W)),
            pl.BlockSpec((tm, BRANCH_W), lambda i: (i, REST_SGU // BRANCH_W + 1)),
            pl.BlockSpec((1, BRANCH_W), lambda i: (0, 0)),
            pl.BlockSpec((None, SGU_GROUPS, SGU_CHUNK, SGU_CHUNK), lambda i: (layer, 0, 0, 0)),
            pl.BlockSpec((SGU_GROUPS, SGU_CHUNK, LANE), lambda i: (0, 0, 0)),
        ],
        out_specs=pl.BlockSpec((tm, BRANCH_W), lambda i: (i, 0)),
        out_shape=jax.ShapeDtypeStruct((s, BRANCH_W), BF16),
        compiler_params=_params("parallel"),
        name="sgu_mixer",
    )(p_rest, p_rest, v_norm_g.reshape(1, BRANCH_W), ws, b_full)


def _swap_rope_halves(y):
    width = y.shape[1]
    lane = lax.broadcasted_iota(jnp.int32, y.shape, 1)
    first_half = (lane % MLA_ROPE) < MLA_ROPE // 2
    return jnp.where(first_half, pltpu.roll(y, width - MLA_ROPE // 2, 1),
                     pltpu.roll(y, MLA_ROPE // 2, 1))


def _mla_prep_body(cq_ref, ckv_ref, kr_ref, cos_ref, sin_ref, qa_g, wuq_ref, kva_g, wukv_ref,
                   gqn_ref, gqr_ref, gkn_ref, gkr_ref, q_ref, k_ref, v_ref):
    nope_w = MLA_HEADS * MLA_NOPE
    qall = jnp.dot(_rms(cq_ref[...], qa_g[...]).astype(BF16), wuq_ref[...], preferred_element_type=F32)
    kv = jnp.dot(_rms(ckv_ref[...], kva_g[...]).astype(BF16), wukv_ref[...], preferred_element_type=F32)
    qn, qr = qall[:, :nope_w], qall[:, nope_w:]
    kn = kv[:, :nope_w]
    v_ref[...] = kv[:, nope_w:].astype(BF16)
    kr_blk = kr_ref[...]
    lane128 = lax.broadcasted_iota(jnp.int32, kr_blk.shape, 1)
    kr2 = jnp.where(lane128 < MLA_ROPE, kr_blk, pltpu.roll(kr_blk, MLA_ROPE, 1))
    kr = jnp.concatenate([kr2] * (MLA_HEADS // 2), axis=1)

    rope_w = MLA_HEADS * MLA_ROPE
    head_of_lane = lax.broadcasted_iota(jnp.int32, (1, rope_w), 1) // MLA_ROPE
    qr_sq = qr * qr
    kr_ssq = jnp.sum(jnp.where(head_of_lane == 0, kr * kr, 0.0), axis=-1, keepdims=True)
    rq, rk = [], []
    for h in range(MLA_HEADS):
        sl = slice(h * MLA_NOPE, (h + 1) * MLA_NOPE)
        q_ssq = (jnp.sum(qn[:, sl] * qn[:, sl], axis=-1, keepdims=True)
                 + jnp.sum(jnp.where(head_of_lane == h, qr_sq, 0.0), axis=-1, keepdims=True))
        k_ssq = jnp.sum(kn[:, sl] * kn[:, sl], axis=-1, keepdims=True) + kr_ssq
        rq.append(lax.rsqrt(q_ssq / MLA_QK + EPS))
        rk.append(lax.rsqrt(k_ssq / MLA_QK + EPS))

    def per_lane(rs):
        out = jnp.zeros((kr.shape[0], rope_w), F32)
        for h in range(MLA_HEADS):
            out = jnp.where(head_of_lane == h, rs[h], out)
        return out

    def rotate(x, rs, g_ref):
        y = x * per_lane(rs) * g_ref[...]
        return y * cos_ref[...] + _swap_rope_halves(y) * sin_ref[...]

    q_rope = rotate(qr, rq, gqr_ref) * MLA_EXP2_SCALE
    k_rope = rotate(kr, rk, gkr_ref)
    lane_tile = lax.broadcasted_iota(jnp.int32, (1, LANE), 1) // MLA_ROPE
    for h in range(MLA_HEADS):
        sl = slice(h * MLA_NOPE, (h + 1) * MLA_NOPE)
        base = h * MLA_PAD_QK
        tile = slice((h // 2) * LANE, (h // 2 + 1) * LANE)
        q_ref[:, base:base + MLA_NOPE] = (qn[:, sl] * rq[h] * gqn_ref[:, sl] * MLA_EXP2_SCALE).astype(BF16)
        k_ref[:, base:base + MLA_NOPE] = (kn[:, sl] * rk[h] * gkn_ref[:, sl]).astype(BF16)
        q_ref[:, base + MLA_NOPE:base + MLA_PAD_QK] = jnp.where(
            lane_tile == h % 2, q_rope[:, tile], 0.0).astype(BF16)
        k_ref[:, base + MLA_NOPE:base + MLA_PAD_QK] = k_rope[:, tile].astype(BF16)


def mla_prep(p_rest, cos4, sin4, qa_g, wuq, kva_g, wukv, gqn, gqr, gkn, gkr, *, tm=1024):
    s = p_rest.shape[0]
    rope_w = MLA_HEADS * MLA_ROPE

    def const(shape):
        return pl.BlockSpec(shape, lambda i: (0,) * len(shape))

    return pl.pallas_call(
        _mla_prep_body,
        grid=(s // tm,),
        in_specs=[
            pl.BlockSpec((tm, MLA_Q_LORA), lambda i: (i, REST_CQ // MLA_Q_LORA)),
            pl.BlockSpec((tm, MLA_KV_LORA), lambda i: (i, REST_CKV // MLA_KV_LORA)),
            pl.BlockSpec((tm, LANE), lambda i: (i, REST_KR // LANE)),
            pl.BlockSpec((tm, rope_w), lambda i: (i, 0)),
            pl.BlockSpec((tm, rope_w), lambda i: (i, 0)),
            const((1, MLA_Q_LORA)), const(wuq.shape), const((1, MLA_KV_LORA)), const(wukv.shape),
            const(gqn.shape), const(gqr.shape), const(gkn.shape), const(gkr.shape),
        ],
        out_specs=[
            pl.BlockSpec((tm, MLA_HEADS * MLA_PAD_QK), lambda i: (i, 0)),
            pl.BlockSpec((tm, MLA_HEADS * MLA_PAD_QK), lambda i: (i, 0)),
            pl.BlockSpec((tm, MLA_HEADS * MLA_V), lambda i: (i, 0)),
        ],
        out_shape=[
            jax.ShapeDtypeStruct((s, MLA_HEADS * MLA_PAD_QK), BF16),
            jax.ShapeDtypeStruct((s, MLA_HEADS * MLA_PAD_QK), BF16),
            jax.ShapeDtypeStruct((s, MLA_HEADS * MLA_V), BF16),
        ],
        compiler_params=_params("parallel"),
        name="mla_prep",
    )(p_rest, p_rest, p_rest, cos4, sin4, qa_g.reshape(1, -1), wuq, kva_g.reshape(1, -1), wukv,
      gqn, gqr, gkn, gkr)


def _mla_attn_body(q_ref, k_ref, v_ref, o_ref, m_ref, l_ref, acc_ref):
    j = pl.program_id(2)

    @pl.when(j == 0)
    def _():
        m_ref[...] = jnp.full(m_ref.shape, NEG_BIG, F32)
        l_ref[...] = jnp.zeros(l_ref.shape, F32)
        acc_ref[...] = jnp.zeros(acc_ref.shape, F32)

    tk = k_ref.shape[0]
    for h in range(MLA_HEADS_PER_STEP):
        qk = slice(h * MLA_PAD_QK, (h + 1) * MLA_PAD_QK)
        vs = slice(h * MLA_V, (h + 1) * MLA_V)
        s = lax.dot_general(q_ref[:, qk], k_ref[:, qk], (((1,), (1,)), ((), ())),
                            preferred_element_type=F32)
        m_prev = m_ref[h]
        m_new = jnp.maximum(m_prev, jnp.max(s, axis=-1, keepdims=True))
        alpha = jnp.exp2(m_prev - m_new)
        p = jnp.exp2(s - jnp.tile(m_new, (1, tk // LANE)))
        l_ref[h] = alpha * l_ref[h] + jnp.sum(p, axis=-1, keepdims=True)
        acc_ref[h] = alpha * acc_ref[h] + jnp.dot(p.astype(BF16), v_ref[:, vs], preferred_element_type=F32)
        m_ref[h] = m_new

    @pl.when(j == pl.num_programs(2) - 1)
    def _():
        for h in range(MLA_HEADS_PER_STEP):
            o_ref[:, h * MLA_V:(h + 1) * MLA_V] = (acc_ref[h] / l_ref[h]).astype(BF16)


def mla_attn(q, k, v, *, tq=1024, tk=2048):
    s = q.shape[0]
    hp = MLA_HEADS_PER_STEP
    stat = pltpu.VMEM((hp, tq, LANE), F32)
    return pl.pallas_call(
        _mla_attn_body,
        grid=(s // tq, MLA_HEADS // hp, s // tk),
        in_specs=[
            pl.BlockSpec((tq, hp * MLA_PAD_QK), lambda i, h, j: (i, h)),
            pl.BlockSpec((tk, hp * MLA_PAD_QK), lambda i, h, j: (j, h)),
            pl.BlockSpec((tk, hp * MLA_V), lambda i, h, j: (j, h)),
        ],
        out_specs=pl.BlockSpec((tq, hp * MLA_V), lambda i, h, j: (i, h)),
        out_shape=jax.ShapeDtypeStruct((s, MLA_HEADS * MLA_V), BF16),
        scratch_shapes=[stat, stat, stat],
        compiler_params=_params("parallel", "parallel", "arbitrary"),
        name="mla_attn",
    )(q, k, v)


def _merge_body(h_ref, *refs):
    z_refs, wg_refs, wb_refs = refs[:N_BRANCH], refs[N_BRANCH:2 * N_BRANCH], refs[2 * N_BRANCH:3 * N_BRANCH]
    o_ref = refs[3 * N_BRANCH]
    h = h_ref[...]
    acc = None
    for z_ref, wg_ref, wb_ref in zip(z_refs, wg_refs, wb_refs):
        gate = jax.nn.sigmoid(_dot_nt(h, wg_ref[0]))
        term = gate * jnp.dot(z_ref[...], wb_ref[...].astype(BF16), preferred_element_type=F32)
        acc = term if acc is None else acc + term
    o_ref[...] = acc.astype(BF16)


GATE_ROW_ALIGN = 64


def merge(h, zs, wt, w_branch, layer, *, tm=1024, tn=512):
    s, d = h.shape
    ct = d // tn
    assert OFF_GATE % GATE_ROW_ALIGN == 0 and d % GATE_ROW_ALIGN == 0 and tn % GATE_ROW_ALIGN == 0
    z_spec = pl.BlockSpec((tm, BRANCH_W), lambda i, c: (i, 0))
    g_specs = [pl.BlockSpec((pl.Element(1), pl.Element(tn), pl.Element(d)),
                            lambda i, c, n=n: (layer, pl.multiple_of(OFF_GATE + n * d + c * tn, GATE_ROW_ALIGN), 0))
               for n in range(N_BRANCH)]
    b_specs = [pl.BlockSpec((None, None, BRANCH_W, tn), lambda i, c, n=n: (layer, n, 0, c))
               for n in range(N_BRANCH)]
    return pl.pallas_call(
        _merge_body,
        grid=(s // tm, ct),
        in_specs=[pl.BlockSpec((tm, d), lambda i, c: (i, 0))] + [z_spec] * N_BRANCH + g_specs + b_specs,
        out_specs=pl.BlockSpec((tm, tn), lambda i, c: (i, c)),
        out_shape=jax.ShapeDtypeStruct((s, d), BF16),
        compiler_params=_params("parallel", "arbitrary"),
        name="merge",
    )(h, *zs, *([wt] * N_BRANCH), *([w_branch] * N_BRANCH))


def _out_proj_body(x_ref, y_ref, w_ref, o_ref):
    o_ref[...] = x_ref[...] + jnp.dot(y_ref[...], w_ref[...].astype(BF16), preferred_element_type=F32)


def out_proj(x, y, w, layer, *, tm=1024, tn=1024):
    s, d = x.shape
    return pl.pallas_call(
        _out_proj_body,
        grid=(d // tn, s // tm),
        in_specs=[
            pl.BlockSpec((tm, tn), lambda j, i: (i, j)),
            pl.BlockSpec((tm, d), lambda j, i: (i, 0)),
            pl.BlockSpec((None, d, tn), lambda j, i: (layer, 0, j)),
        ],
        out_specs=pl.BlockSpec((tm, tn), lambda j, i: (i, j)),
        out_shape=jax.ShapeDtypeStruct((s, d), F32),
        compiler_params=_params("arbitrary", "arbitrary"),
        name="out_proj",
    )(x, y, w)


def _cast_body(x_ref, o_ref):
    o_ref[...] = x_ref[...].astype(BF16)


def cast_rows(wt, *, tr=1024):
    nl, n, d = wt.shape
    return pl.pallas_call(
        _cast_body,
        grid=(nl, pl.cdiv(n, tr)),
        in_specs=[pl.BlockSpec((None, tr, d), lambda l, r: (l, r, 0))],
        out_specs=pl.BlockSpec((None, tr, d), lambda l, r: (l, r, 0)),
        out_shape=jax.ShapeDtypeStruct((nl, n, d), BF16),
        compiler_params=_params("parallel", "parallel"),
        name="cast_rows",
    )(wt)


def _rope_tables(s):
    pos = jnp.arange(s, dtype=F32)
    inv_freq = ROPE_THETA ** (-jnp.arange(0, MLA_ROPE, 2, dtype=F32) / MLA_ROPE)
    ang = pos[:, None] * inv_freq[None, :]
    cos, sin = jnp.cos(ang), jnp.sin(ang)
    cos4 = jnp.tile(jnp.concatenate([cos, cos], axis=-1), (1, MLA_HEADS))
    sin4 = jnp.tile(jnp.concatenate([-sin, sin], axis=-1), (1, MLA_HEADS))
    return cos4, sin4


def _split_heads(w, widths):
    per_head = sum(widths)
    w3 = w.reshape(w.shape[0], MLA_HEADS, per_head)
    parts, off = [], 0
    for wd in widths:
        parts.append(w3[:, :, off:off + wd].reshape(w.shape[0], MLA_HEADS * wd))
        off += wd
    return jnp.concatenate(parts, axis=1)


def kernel(x, ffn1_norm, ffn1_w_gate, ffn1_w_up, ffn1_w_down, mix_norm, w_in, pool_w, pool_scale,
           dil_q_norm, dil_k_norm, sgu_v_norm, sgu_w, sgu_b, mla_q_a_norm, mla_w_uq, mla_kv_a_norm,
           mla_w_ukv, mla_q_norm, mla_k_norm, w_branch, w_out, ffn2_norm, ffn2_w_gate, ffn2_w_up,
           ffn2_w_down):
    b, s, d = x.shape
    cos4, sin4 = _rope_tables(s)
    w_main = cast_rows(jnp.swapaxes(w_in, 1, 2))
    pool_wb, sgu_wb = pool_w.astype(BF16), sgu_w.astype(BF16)
    ones = jnp.ones((BRANCH_W,), F32)

    outs = []
    for bi in range(b):
        xb = x[bi]
        for l in range(DEPTH):
            xb = ffn(xb, ffn1_norm[l], ffn1_w_gate, ffn1_w_up, ffn1_w_down, l)

            p_rest, h = proj_rest(xb, mix_norm[l], w_main, l)
            dil_gains = jnp.stack([jnp.tile(dil_q_norm[l], DIL_HEADS_PER_GROUP),
                                   jnp.tile(dil_k_norm[l], DIL_HEADS_PER_GROUP), ones]).reshape(3, 1, BRANCH_W)
            qkvs = proj_dil(h, w_main, dil_gains, l)
            z_a = pool_mixer(p_rest, pool_wb, pool_scale[l], l)
            o_l = [dil_attn(qkvs[g], g) for g in range(DIL_GROUPS)]
            z_b = dil_combine([o for o, _ in o_l], [ls for _, ls in o_l])
            b_full = jnp.broadcast_to(sgu_b[l][:, :, None], (SGU_GROUPS, SGU_CHUNK, LANE))
            z_c = sgu_mixer(p_rest, sgu_v_norm[l], sgu_wb, b_full, l)
            q, k, v = mla_prep(
                p_rest, cos4, sin4, mla_q_a_norm[l],
                _split_heads(mla_w_uq[l], (MLA_NOPE, MLA_ROPE)).astype(BF16),
                mla_kv_a_norm[l], _split_heads(mla_w_ukv[l], (MLA_NOPE, MLA_V)).astype(BF16),
                jnp.tile(mla_q_norm[l][:MLA_NOPE], MLA_HEADS).reshape(1, -1),
                jnp.tile(mla_q_norm[l][MLA_NOPE:], MLA_HEADS).reshape(1, -1),
                jnp.tile(mla_k_norm[l][:MLA_NOPE], MLA_HEADS).reshape(1, -1),
                jnp.tile(mla_k_norm[l][MLA_NOPE:], MLA_HEADS).reshape(1, -1))
            z_d = mla_attn(q, k, v)
            merged = merge(h, (z_a, z_b, z_c, z_d), w_main, w_branch, l)
            xb = out_proj(xb, merged, w_out, l)

            xb = ffn(xb, ffn2_norm[l], ffn2_w_gate, ffn2_w_up, ffn2_w_down, l)
        outs.append(xb)
    return jnp.stack(outs, axis=0)
```

```python
import functools
import math

import numpy as np
import jax
import jax.numpy as jnp
from jax import lax
from jax.experimental import pallas as pl
from jax.experimental.pallas import tpu as pltpu

F32 = jnp.float32
BF16 = jnp.bfloat16

D_MODEL = 2048
DEPTH = 4
N_BRANCH = 4
BRANCH_W = 512
POOL_WINDOWS = (2, 4, 8, 16)
POOL_HALO = max(POOL_WINDOWS) // 2
DIL_PATTERNS = ((128, 1), (512, 4), (2048, 16))
DIL_GROUPS = len(DIL_PATTERNS)
DIL_HEADS = 12
DIL_HEADS_PER_GROUP = 4
DIL_HEAD_DIM = 128
DIL_QKV = 3 * DIL_HEADS * DIL_HEAD_DIM
DIL_SIDE = 64
SGU_CHUNK = 128
SGU_GROUPS = 4
MLA_HEADS = 4
MLA_Q_LORA = 384
MLA_KV_LORA = 128
MLA_NOPE = 128
MLA_ROPE = 64
MLA_V = 128
MLA_QK = MLA_NOPE + MLA_ROPE
MLA_PAD_QK = 256
MLA_HEADS_PER_STEP = 4
ROPE_THETA = 10000.0
EPS = 1e-6
NEG_BIG = -1e30
MLA_EXP2_SCALE = (MLA_QK ** -0.5) * math.log2(math.e)

OFF_POOL = 0
OFF_DIL = OFF_POOL + BRANCH_W
OFF_SGU = OFF_DIL + DIL_QKV
OFF_MLA_Q = OFF_SGU + 2 * BRANCH_W
OFF_MLA_KV = OFF_MLA_Q + MLA_Q_LORA
OFF_MLA_KR = OFF_MLA_KV + MLA_KV_LORA
OFF_GATE = OFF_MLA_KR + MLA_ROPE

LANE = 128
REST_POOL = 0
REST_SGU = REST_POOL + BRANCH_W
REST_CQ = REST_SGU + 2 * BRANCH_W
REST_CKV = REST_CQ + MLA_Q_LORA
REST_KR = REST_CKV + MLA_KV_LORA
REST_W = REST_KR + LANE

VMEM_LIMIT = 56 * 1024 * 1024
SEQ_MULTIPLE = 2048


def _params(*sem):
    return pltpu.CompilerParams(dimension_semantics=sem, vmem_limit_bytes=VMEM_LIMIT)


def _rms(xf, g):
    return xf * lax.rsqrt(jnp.mean(xf * xf, axis=-1, keepdims=True) + EPS) * g


def _dot_nt(a, b_t):
    return lax.dot_general(a, b_t, (((1,), (1,)), ((), ())), preferred_element_type=F32)


FFN_PREFETCH_STEP = 1


def _ffn_body(x_hbm, g_ref, wg_ref, wu_ref, wd_ref, o_hbm, acc_ref, h_ref, in_sem, out_sem):
    i, f = pl.program_id(0), pl.program_id(1)
    ni, nf = pl.num_programs(0), pl.num_programs(1)
    tm = acc_ref.shape[1]
    slot = lax.rem(i, 2)
    other = 1 - slot

    def x_copy(tile, buf):
        return pltpu.make_async_copy(x_hbm.at[pl.ds(tile * tm, tm), :], acc_ref.at[buf], in_sem.at[buf])

    def out_copy(tile, buf):
        return pltpu.make_async_copy(acc_ref.at[buf], o_hbm.at[pl.ds(tile * tm, tm), :], out_sem.at[buf])

    @pl.when(f == 0)
    def _():
        @pl.when(i == 0)
        def _():
            x_copy(i, slot).start()
        x_copy(i, slot).wait()
        h_ref[...] = _rms(acc_ref[slot], g_ref[...]).astype(BF16)

    @pl.when((f == FFN_PREFETCH_STEP) & (i + 1 < ni))
    def _():
        @pl.when(i > 0)
        def _():
            out_copy(i - 1, other).wait()
        x_copy(i + 1, other).start()

    h = h_ref[...]
    g = jnp.dot(h, wg_ref[...].astype(BF16), preferred_element_type=F32)
    u = jnp.dot(h, wu_ref[...].astype(BF16), preferred_element_type=F32)
    a = (0.5 * (jax.nn.silu(g) * u)).astype(BF16)
    acc_ref[slot] += jnp.dot(a, wd_ref[...].astype(BF16), preferred_element_type=F32)

    @pl.when(f == nf - 1)
    def _():
        out_copy(i, slot).start()

        @pl.when(i == ni - 1)
        def _():
            @pl.when(ni > 1)
            def _():
                out_copy(i - 1, other).wait()
            out_copy(i, slot).wait()


def ffn(x, norm_g, wg, wu, wd, layer, *, tm=1024, tf=512):
    s, d = x.shape
    f = wg.shape[2]
    assert f // tf > FFN_PREFETCH_STEP
    return pl.pallas_call(
        _ffn_body,
        grid=(s // tm, f // tf),
        in_specs=[
            pl.BlockSpec(memory_space=pl.ANY),
            pl.BlockSpec((1, d), lambda i, j: (0, 0)),
            pl.BlockSpec((None, d, tf), lambda i, j: (layer, 0, j)),
            pl.BlockSpec((None, d, tf), lambda i, j: (layer, 0, j)),
            pl.BlockSpec((None, tf, d), lambda i, j: (layer, j, 0)),
        ],
        out_specs=pl.BlockSpec(memory_space=pl.ANY),
        out_shape=jax.ShapeDtypeStruct((s, d), F32),
        scratch_shapes=[pltpu.VMEM((2, tm, d), F32), pltpu.VMEM((tm, d), BF16),
                        pltpu.SemaphoreType.DMA((2,)), pltpu.SemaphoreType.DMA((2,))],
        compiler_params=_params("arbitrary", "arbitrary"),
        name="ffn",
    )(x, norm_g.reshape(1, d), wg, wu, wd)


def _proj_rest_body(x_ref, g_ref, wp_ref, ws_ref, wq_ref, wkv_ref, wkr_ref, p_ref, h_ref):
    h = _rms(x_ref[...], g_ref[...]).astype(BF16)
    h_ref[...] = h
    for w_ref, off in ((wp_ref, REST_POOL), (ws_ref, REST_SGU), (wq_ref, REST_CQ), (wkv_ref, REST_CKV),
                       (wkr_ref, REST_KR)):
        p_ref[:, off:off + w_ref.shape[0]] = _dot_nt(h, w_ref[...])


def proj_rest(x, norm_g, w_main, layer, *, tm=512):
    s, d = x.shape

    def wspec(off, width):
        return pl.BlockSpec((None, width, d), lambda i: (layer, off // width, 0))

    return pl.pallas_call(
        _proj_rest_body,
        grid=(s // tm,),
        in_specs=[
            pl.BlockSpec((tm, d), lambda i: (i, 0)),
            pl.BlockSpec((1, d), lambda i: (0, 0)),
            wspec(OFF_POOL, BRANCH_W),
            wspec(OFF_SGU, 2 * BRANCH_W),
            wspec(OFF_MLA_Q, MLA_Q_LORA),
            wspec(OFF_MLA_KV, MLA_KV_LORA),
            wspec(OFF_MLA_KR, LANE),
        ],
        out_specs=[
            pl.BlockSpec((tm, REST_W), lambda i: (i, 0)),
            pl.BlockSpec((tm, d), lambda i: (i, 0)),
        ],
        out_shape=[jax.ShapeDtypeStruct((s, REST_W), F32), jax.ShapeDtypeStruct((s, d), BF16)],
        compiler_params=_params("parallel"),
        name="proj_rest",
    )(x, norm_g.reshape(1, d), w_main, w_main, w_main, w_main, w_main)


DIL_SCRATCH_PITCH = {1: 1, 4: 4, 16: 24}


def _proj_dil_body(h_ref, w0_ref, w1_ref, w2_ref, g_ref, o0_ref, o1_ref, o2_ref, scr1_ref, scr2_ref):
    normed = pl.program_id(0) < 2
    h = h_ref[...]
    tm = h.shape[0]
    for w_ref, o_ref, scr_ref, (_, d) in zip((w0_ref, w1_ref, w2_ref), (o0_ref, o1_ref, o2_ref),
                                             (None, scr1_ref, scr2_ref), DIL_PATTERNS):
        pitch = DIL_SCRATCH_PITCH[d]
        acc = _dot_nt(h, w_ref[...])
        for a in range(DIL_HEADS_PER_GROUP):
            sl = slice(a * DIL_HEAD_DIM, (a + 1) * DIL_HEAD_DIM)
            xh = acc[:, sl]
            r = lax.rsqrt(jnp.mean(xh * xh, axis=-1, keepdims=True) + EPS)
            y = xh * jnp.where(normed, r, 1.0) * g_ref[:, sl]
            if d == 1:
                o_ref[0, :, sl] = y.astype(BF16)
            elif pitch == d:
                scr_ref[a] = y
            else:
                for k in range(tm // d):
                    scr_ref[a, k * pitch:k * pitch + d, :] = y[k * d:(k + 1) * d, :]
        if d > 1:
            for r in range(d):
                for a in range(DIL_HEADS_PER_GROUP):
                    sl = slice(a * DIL_HEAD_DIM, (a + 1) * DIL_HEAD_DIM)
                    o_ref[r, :, sl] = scr_ref[a, pl.ds(r, tm // d, stride=pitch), :].astype(BF16)


def proj_dil(h, w_main, gains, layer, *, tm=1024):
    s, dm = h.shape
    first = OFF_DIL // BRANCH_W

    def wspec(g):
        return pl.BlockSpec((None, BRANCH_W, dm), lambda j, i: (layer, first + j * DIL_GROUPS + g, 0))

    out_specs, out_shapes, scratch = [], [], []
    for _, d in DIL_PATTERNS:
        out_specs.append(pl.BlockSpec((d, tm // d, BRANCH_W), lambda j, i: (0, i, j)))
        out_shapes.append(jax.ShapeDtypeStruct((d, s // d, 3 * BRANCH_W), BF16))
        if d > 1:
            scratch.append(pltpu.VMEM((DIL_HEADS_PER_GROUP, (tm // d) * DIL_SCRATCH_PITCH[d], DIL_HEAD_DIM), F32))
    return pl.pallas_call(
        _proj_dil_body,
        grid=(3, s // tm),
        in_specs=[
            pl.BlockSpec((tm, dm), lambda j, i: (i, 0)),
            wspec(0), wspec(1), wspec(2),
            pl.BlockSpec((None, 1, BRANCH_W), lambda j, i: (j, 0, 0)),
        ],
        out_specs=out_specs,
        out_shape=out_shapes,
        scratch_shapes=scratch,
        compiler_params=_params("arbitrary", "arbitrary"),
        name="proj_dil",
    )(h, w_main, w_main, w_main, gains)


def _pool_body(prev_ref, cur_ref, next_ref, w_ref, scale_ref, o_ref, ext_ref, *, seq):
    i = pl.program_id(0)
    tm = cur_ref.shape[0]
    cur = cur_ref[...]
    ext_ref[0:POOL_HALO, :] = jnp.where(i > 0, prev_ref[...], 0.0)
    ext_ref[POOL_HALO:POOL_HALO + tm, :] = cur
    ext_ref[POOL_HALO + tm:2 * POOL_HALO + tm, :] = jnp.where(
        i < pl.num_programs(0) - 1, next_ref[...], 0.0)
    t = i * tm + lax.broadcasted_iota(jnp.int32, (tm, 1), 0)
    for g, w in enumerate(POOL_WINDOWS):
        sl = slice(g * LANE, (g + 1) * LANE)
        tot = jnp.zeros((tm, LANE), F32)
        for o in range(-(w // 2), w // 2):
            tot = tot + ext_ref[POOL_HALO + o:POOL_HALO + o + tm, sl]
        cnt = (jnp.minimum(t + w // 2, seq) - jnp.maximum(t - w // 2, 0)).astype(F32)
        dev = (tot / cnt - cur[:, sl]).astype(BF16)
        y = jnp.dot(dev, w_ref[g], preferred_element_type=F32) * scale_ref[:, sl]
        o_ref[:, sl] = y.astype(BF16)


def pool_mixer(p_rest, pool_w, pool_scale, layer, *, tm=1024):
    s = p_rest.shape[0]
    hb = tm // POOL_HALO
    last = s // POOL_HALO - 1
    n_win = len(POOL_WINDOWS)
    return pl.pallas_call(
        functools.partial(_pool_body, seq=s),
        grid=(s // tm,),
        in_specs=[
            pl.BlockSpec((POOL_HALO, BRANCH_W), lambda i: (jnp.maximum(i * hb - 1, 0), 0)),
            pl.BlockSpec((tm, BRANCH_W), lambda i: (i, 0)),
            pl.BlockSpec((POOL_HALO, BRANCH_W), lambda i: (jnp.minimum((i + 1) * hb, last), 0)),
            pl.BlockSpec((None, n_win, LANE, LANE), lambda i: (layer, 0, 0, 0)),
            pl.BlockSpec((1, BRANCH_W), lambda i: (0, 0)),
        ],
        out_specs=pl.BlockSpec((tm, BRANCH_W), lambda i: (i, 0)),
        out_shape=jax.ShapeDtypeStruct((s, BRANCH_W), BF16),
        scratch_shapes=[pltpu.VMEM((tm + 2 * POOL_HALO, BRANCH_W), F32)],
        compiler_params=_params("parallel"),
        name="pool_mixer",
    )(p_rest, p_rest, p_rest, pool_w, pool_scale.reshape(1, BRANCH_W))


def _alibi_slopes():
    n = DIL_HEADS
    return np.exp2(np.float32(-8.0) * np.arange(1, n + 1, dtype=np.float32) / np.float32(n))


LSE_LANES = LANE // DIL_HEADS_PER_GROUP


def _dil_attn_body(cur_ref, prev_ref, next_ref, o_ref, lse_ref, *, sub_len, penalties):
    i = pl.program_id(1)
    t = cur_ref.shape[0]
    blk = prev_ref.shape[0]
    nk = blk + 2 * DIL_SIDE
    row = lax.broadcasted_iota(jnp.int32, (blk, nk), 0)
    col = lax.broadcasted_iota(jnp.int32, (blk, nk), 1)
    dist = jnp.abs(col - DIL_SIDE - row)
    in_band = dist <= DIL_SIDE
    distf = dist.astype(F32)
    lane_head = lax.broadcasted_iota(jnp.int32, (1, LANE), 1) // LSE_LANES
    lse_tiles = [jnp.zeros((blk, LANE), F32) for _ in range(t // blk)]
    for h in range(DIL_HEADS_PER_GROUP):
        ql = slice(h * DIL_HEAD_DIM, (h + 1) * DIL_HEAD_DIM)
        kl = slice(BRANCH_W + h * DIL_HEAD_DIM, BRANCH_W + (h + 1) * DIL_HEAD_DIM)
        vl = slice(2 * BRANCH_W + h * DIL_HEAD_DIM, 2 * BRANCH_W + (h + 1) * DIL_HEAD_DIM)
        k_ext = jnp.concatenate([prev_ref[blk - DIL_SIDE:, kl], cur_ref[:, kl], next_ref[:DIL_SIDE, kl]], axis=0)
        v_ext = jnp.concatenate([prev_ref[blk - DIL_SIDE:, vl], cur_ref[:, vl], next_ref[:DIL_SIDE, vl]], axis=0)
        bias = jnp.where(in_band, -penalties[h] * distf, NEG_BIG)
        for b in range(t // blk):
            rows = slice(b * blk, (b + 1) * blk)
            kpos = i * t + b * blk + col - DIL_SIDE
            s = lax.dot_general(cur_ref[rows, ql], k_ext[b * blk:b * blk + nk], (((1,), (1,)), ((), ())),
                                preferred_element_type=F32) * (DIL_HEAD_DIM ** -0.5)
            s = jnp.where((kpos >= 0) & (kpos < sub_len), s + bias, NEG_BIG)
            m = jnp.max(s, axis=-1, keepdims=True)
            p = jnp.exp(s - m)
            den = jnp.sum(p, axis=-1, keepdims=True)
            o_ref[rows, ql] = jnp.dot((p / den).astype(BF16), v_ext[b * blk:b * blk + nk],
                                      preferred_element_type=F32).astype(BF16)
            lse_tiles[b] = jnp.where(lane_head == h, m + jnp.log(den), lse_tiles[b])
    for b, tile in enumerate(lse_tiles):
        lse_ref[b * blk:(b + 1) * blk, :] = tile


def dil_attn(qkv, group, *, blk=128, max_blocks=16):
    d, sub_len, width = qkv.shape
    n_blk = sub_len // blk
    qb = min(max_blocks, n_blk)
    t = blk * qb
    slopes = _alibi_slopes()[group * DIL_HEADS_PER_GROUP:(group + 1) * DIL_HEADS_PER_GROUP]
    penalties = tuple(float(np.float32(x) * np.float32(d)) for x in slopes)

    def halo(step):
        return pl.BlockSpec((None, blk, width), lambda r, i: (r, jnp.clip(i * qb + step, 0, n_blk - 1), 0))

    return pl.pallas_call(
        functools.partial(_dil_attn_body, sub_len=sub_len, penalties=penalties),
        grid=(d, sub_len // t),
        in_specs=[pl.BlockSpec((None, t, width), lambda r, i: (r, i, 0)), halo(-1), halo(qb)],
        out_specs=[pl.BlockSpec((None, t, BRANCH_W), lambda r, i: (r, i, 0)),
                   pl.BlockSpec((None, t, LANE), lambda r, i: (r, i, 0))],
        out_shape=[jax.ShapeDtypeStruct((d, sub_len, BRANCH_W), BF16),
                   jax.ShapeDtypeStruct((d, sub_len, LANE), F32)],
        compiler_params=_params("parallel", "parallel"),
        name=f"dil_attn_g{group}",
    )(qkv, qkv, qkv)


def _dil_combine_body(o0, o1, o2, l0, l1, l2, z_ref, so1, so2, sl1, sl2):
    tm = z_ref.shape[0]
    for src, scr, (_, d) in ((o1, so1, DIL_PATTERNS[1]), (o2, so2, DIL_PATTERNS[2])):
        for r in range(d):
            for a in range(DIL_HEADS_PER_GROUP):
                sl = slice(a * DIL_HEAD_DIM, (a + 1) * DIL_HEAD_DIM)
                scr[a, pl.ds(r, tm // d, stride=d), :] = src[r, :, sl].astype(F32)
    for src, scr, (_, d) in ((l1, sl1, DIL_PATTERNS[1]), (l2, sl2, DIL_PATTERNS[2])):
        for r in range(d):
            scr[pl.ds(r, tm // d, stride=d), :] = src[r]
    la, lb, lc = l0[0], sl1[...], sl2[...]
    m = jnp.maximum(jnp.maximum(la, lb), lc)
    ea, eb, ec = jnp.exp(la - m), jnp.exp(lb - m), jnp.exp(lc - m)
    tot = ea + eb + ec
    wa, wb, wc = ea / tot, eb / tot, ec / tot
    for a in range(DIL_HEADS_PER_GROUP):
        sl = slice(a * DIL_HEAD_DIM, (a + 1) * DIL_HEAD_DIM)
        one = slice(a * LSE_LANES, a * LSE_LANES + 1)
        shape = (tm, DIL_HEAD_DIM)
        z = (jnp.broadcast_to(wa[:, one], shape) * o0[0, :, sl].astype(F32)
             + jnp.broadcast_to(wb[:, one], shape) * so1[a]
             + jnp.broadcast_to(wc[:, one], shape) * so2[a])
        z_ref[:, sl] = z.astype(BF16)


def dil_combine(outs, lses, *, tm=1024):
    s = outs[0].shape[1]
    o_specs = [pl.BlockSpec((d, tm // d, BRANCH_W), lambda i: (0, i, 0)) for _, d in DIL_PATTERNS]
    l_specs = [pl.BlockSpec((d, tm // d, LANE), lambda i: (0, i, 0)) for _, d in DIL_PATTERNS]
    slab = pltpu.VMEM((DIL_HEADS_PER_GROUP, tm, DIL_HEAD_DIM), F32)
    return pl.pallas_call(
        _dil_combine_body,
        grid=(s // tm,),
        in_specs=o_specs + l_specs,
        out_specs=pl.BlockSpec((tm, BRANCH_W), lambda i: (i, 0)),
        out_shape=jax.ShapeDtypeStruct((s, BRANCH_W), BF16),
        scratch_shapes=[slab, slab, pltpu.VMEM((tm, LANE), F32), pltpu.VMEM((tm, LANE), F32)],
        compiler_params=_params("parallel"),
        name="dil_combine",
    )(*outs, *lses)


def _sgu_body(u_ref, v_ref, g_ref, ws_ref, b_ref, o_ref):
    tm = u_ref.shape[0]
    u = jax.nn.gelu(u_ref[...])
    vn = _rms(jax.nn.gelu(v_ref[...]), g_ref[...]).astype(BF16)
    for n in range(tm // SGU_CHUNK):
        rows = slice(n * SGU_CHUNK, (n + 1) * SGU_CHUNK)
        for g in range(SGU_GROUPS):
            cols = slice(g * LANE, (g + 1) * LANE)
            mixed = jnp.dot(ws_ref[g], vn[rows, cols], preferred_element_type=F32) + b_ref[g]
            o_ref[rows, cols] = (u[rows, cols] * mixed).astype(BF16)


def sgu_mixer(p_rest, v_norm_g, ws, b_full, layer, *, tm=1024):
    s = p_rest.shape[0]
    return pl.pallas_call(
        _sgu_body,
        grid=(s // tm,),
        in_specs=[
            pl.BlockSpec((tm, BRANCH_W), lambda i: (i, REST_SGU // BRANCH_W)),
            pl.BlockSpec((tm, BRANCH_W), lambda i: (i, REST_SGU // BRANCH_W + 1)),
            pl.BlockSpec((1, BRANCH_W), lambda i: (0, 0)),
            pl.BlockSpec((None, SGU_GROUPS, SGU_CHUNK, SGU_CHUNK), lambda i: (layer, 0, 0, 0)),
            pl.BlockSpec((SGU_GROUPS, SGU_CHUNK, LANE), lambda i: (0, 0, 0)),
        ],
        out_specs=pl.BlockSpec((tm, BRANCH_W), lambda i: (i, 0)),
        out_shape=jax.ShapeDtypeStruct((s, BRANCH_W), BF16),
        compiler_params=_params("parallel"),
        name="sgu_mixer",
    )(p_rest, p_rest, v_norm_g.reshape(1, BRANCH_W), ws, b_full)


def _swap_rope_halves(y):
    width = y.shape[1]
    lane = lax.broadcasted_iota(jnp.int32, y.shape, 1)
    first_half = (lane % MLA_ROPE) < MLA_ROPE // 2
    return jnp.where(first_half, pltpu.roll(y, width - MLA_ROPE // 2, 1),
                     pltpu.roll(y, MLA_ROPE // 2, 1))


def _mla_prep_body(cq_ref, ckv_ref, kr_ref, cos_ref, sin_ref, qa_g, wuq_ref, kva_g, wukv_ref,
                   gqn_ref, gqr_ref, gkn_ref, gkr_ref, q_ref, k_ref, v_ref):
    nope_w = MLA_HEADS * MLA_NOPE
    qall = jnp.dot(_rms(cq_ref[...], qa_g[...]).astype(BF16), wuq_ref[...], preferred_element_type=F32)
    kv = jnp.dot(_rms(ckv_ref[...], kva_g[...]).astype(BF16), wukv_ref[...], preferred_element_type=F32)
    qn, qr = qall[:, :nope_w], qall[:, nope_w:]
    kn = kv[:, :nope_w]
    v_ref[...] = kv[:, nope_w:].astype(BF16)
    kr_blk = kr_ref[...]
    lane128 = lax.broadcasted_iota(jnp.int32, kr_blk.shape, 1)
    kr2 = jnp.where(lane128 < MLA_ROPE, kr_blk, pltpu.roll(kr_blk, MLA_ROPE, 1))
    kr = jnp.concatenate([kr2] * (MLA_HEADS // 2), axis=1)

    rope_w = MLA_HEADS * MLA_ROPE
    head_of_lane = lax.broadcasted_iota(jnp.int32, (1, rope_w), 1) // MLA_ROPE
    qr_sq = qr * qr
    kr_ssq = jnp.sum(jnp.where(head_of_lane == 0, kr * kr, 0.0), axis=-1, keepdims=True)
    rq, rk = [], []
    for h in range(MLA_HEADS):
        sl = slice(h * MLA_NOPE, (h + 1) * MLA_NOPE)
        q_ssq = (jnp.sum(qn[:, sl] * qn[:, sl], axis=-1, keepdims=True)
                 + jnp.sum(jnp.where(head_of_lane == h, qr_sq, 0.0), axis=-1, keepdims=True))
        k_ssq = jnp.sum(kn[:, sl] * kn[:, sl], axis=-1, keepdims=True) + kr_ssq
        rq.append(lax.rsqrt(q_ssq / MLA_QK + EPS))
        rk.append(lax.rsqrt(k_ssq / MLA_QK + EPS))

    def per_lane(rs):
        out = jnp.zeros((kr.shape[0], rope_w), F32)
        for h in range(MLA_HEADS):
            out = jnp.where(head_of_lane == h, rs[h], out)
        return out

    def rotate(x, rs, g_ref):
        y = x * per_lane(rs) * g_ref[...]
        return y * cos_ref[...] + _swap_rope_halves(y) * sin_ref[...]

    q_rope = rotate(qr, rq, gqr_ref) * MLA_EXP2_SCALE
    k_rope = rotate(kr, rk, gkr_ref)
    lane_tile = lax.broadcasted_iota(jnp.int32, (1, LANE), 1) // MLA_ROPE
    for h in range(MLA_HEADS):
        sl = slice(h * MLA_NOPE, (h + 1) * MLA_NOPE)
        base = h * MLA_PAD_QK
        tile = slice((h // 2) * LANE, (h // 2 + 1) * LANE)
        q_ref[:, base:base + MLA_NOPE] = (qn[:, sl] * rq[h] * gqn_ref[:, sl] * MLA_EXP2_SCALE).astype(BF16)
        k_ref[:, base:base + MLA_NOPE] = (kn[:, sl] * rk[h] * gkn_ref[:, sl]).astype(BF16)
        q_ref[:, base + MLA_NOPE:base + MLA_PAD_QK] = jnp.where(
            lane_tile == h % 2, q_rope[:, tile], 0.0).astype(BF16)
        k_ref[:, base + MLA_NOPE:base + MLA_PAD_QK] = k_rope[:, tile].astype(BF16)


def mla_prep(p_rest, cos4, sin4, qa_g, wuq, kva_g, wukv, gqn, gqr, gkn, gkr, *, tm=1024):
    s = p_rest.shape[0]
    rope_w = MLA_HEADS * MLA_ROPE

    def const(shape):
        return pl.BlockSpec(shape, lambda i: (0,) * len(shape))

    return pl.pallas_call(
        _mla_prep_body,
        grid=(s // tm,),
        in_specs=[
            pl.BlockSpec((tm, MLA_Q_LORA), lambda i: (i, REST_CQ // MLA_Q_LORA)),
            pl.BlockSpec((tm, MLA_KV_LORA), lambda i: (i, REST_CKV // MLA_KV_LORA)),
            pl.BlockSpec((tm, LANE), lambda i: (i, REST_KR // LANE)),
            pl.BlockSpec((tm, rope_w), lambda i: (i, 0)),
            pl.BlockSpec((tm, rope_w), lambda i: (i, 0)),
            const((1, MLA_Q_LORA)), const(wuq.shape), const((1, MLA_KV_LORA)), const(wukv.shape),
            const(gqn.shape), const(gqr.shape), const(gkn.shape), const(gkr.shape),
        ],
        out_specs=[
            pl.BlockSpec((tm, MLA_HEADS * MLA_PAD_QK), lambda i: (i, 0)),
            pl.BlockSpec((tm, MLA_HEADS * MLA_PAD_QK), lambda i: (i, 0)),
            pl.BlockSpec((tm, MLA_HEADS * MLA_V), lambda i: (i, 0)),
        ],
        out_shape=[
            jax.ShapeDtypeStruct((s, MLA_HEADS * MLA_PAD_QK), BF16),
            jax.ShapeDtypeStruct((s, MLA_HEADS * MLA_PAD_QK), BF16),
            jax.ShapeDtypeStruct((s, MLA_HEADS * MLA_V), BF16),
        ],
        compiler_params=_params("parallel"),
        name="mla_prep",
    )(p_rest, p_rest, p_rest, cos4, sin4, qa_g.reshape(1, -1), wuq, kva_g.reshape(1, -1), wukv,
      gqn, gqr, gkn, gkr)


def _mla_attn_body(q_ref, k_ref, v_ref, o_ref, m_ref, l_ref, acc_ref):
    j = pl.program_id(2)

    @pl.when(j == 0)
    def _():
        m_ref[...] = jnp.full(m_ref.shape, NEG_BIG, F32)
        l_ref[...] = jnp.zeros(l_ref.shape, F32)
        acc_ref[...] = jnp.zeros(acc_ref.shape, F32)

    tk = k_ref.shape[0]
    for h in range(MLA_HEADS_PER_STEP):
        qk = slice(h * MLA_PAD_QK, (h + 1) * MLA_PAD_QK)
        vs = slice(h * MLA_V, (h + 1) * MLA_V)
        s = lax.dot_general(q_ref[:, qk], k_ref[:, qk], (((1,), (1,)), ((), ())),
                            preferred_element_type=F32)
        m_prev = m_ref[h]
        m_new = jnp.maximum(m_prev, jnp.max(s, axis=-1, keepdims=True))
        alpha = jnp.exp2(m_prev - m_new)
        p = jnp.exp2(s - jnp.tile(m_new, (1, tk // LANE)))
        l_ref[h] = alpha * l_ref[h] + jnp.sum(p, axis=-1, keepdims=True)
        acc_ref[h] = alpha * acc_ref[h] + jnp.dot(p.astype(BF16), v_ref[:, vs], preferred_element_type=F32)
        m_ref[h] = m_new

    @pl.when(j == pl.num_programs(2) - 1)
    def _():
        for h in range(MLA_HEADS_PER_STEP):
            o_ref[:, h * MLA_V:(h + 1) * MLA_V] = (acc_ref[h] / l_ref[h]).astype(BF16)


def mla_attn(q, k, v, *, tq=1024, tk=2048):
    s = q.shape[0]
    hp = MLA_HEADS_PER_STEP
    stat = pltpu.VMEM((hp, tq, LANE), F32)
    return pl.pallas_call(
        _mla_attn_body,
        grid=(s // tq, MLA_HEADS // hp, s // tk),
        in_specs=[
            pl.BlockSpec((tq, hp * MLA_PAD_QK), lambda i, h, j: (i, h)),
            pl.BlockSpec((tk, hp * MLA_PAD_QK), lambda i, h, j: (j, h)),
            pl.BlockSpec((tk, hp * MLA_V), lambda i, h, j: (j, h)),
        ],
        out_specs=pl.BlockSpec((tq, hp * MLA_V), lambda i, h, j: (i, h)),
        out_shape=jax.ShapeDtypeStruct((s, MLA_HEADS * MLA_V), BF16),
        scratch_shapes=[stat, stat, stat],
        compiler_params=_params("parallel", "parallel", "arbitrary"),
        name="mla_attn",
    )(q, k, v)


def _merge_body(h_ref, *refs):
    z_refs, wg_refs, wb_refs = refs[:N_BRANCH], refs[N_BRANCH:2 * N_BRANCH], refs[2 * N_BRANCH:3 * N_BRANCH]
    o_ref = refs[3 * N_BRANCH]
    h = h_ref[...]
    acc = None
    for z_ref, wg_ref, wb_ref in zip(z_refs, wg_refs, wb_refs):
        gate = jax.nn.sigmoid(_dot_nt(h, wg_ref[0]))
        term = gate * jnp.dot(z_ref[...], wb_ref[...].astype(BF16), preferred_element_type=F32)
        acc = term if acc is None else acc + term
    o_ref[...] = acc.astype(BF16)


GATE_ROW_ALIGN = 64


def merge(h, zs, wt, w_branch, layer, *, tm=1024, tn=512):
    s, d = h.shape
    ct = d // tn
    assert OFF_GATE % GATE_ROW_ALIGN == 0 and d % GATE_ROW_ALIGN == 0 and tn % GATE_ROW_ALIGN == 0
    z_spec = pl.BlockSpec((tm, BRANCH_W), lambda i, c: (i, 0))
    g_specs = [pl.BlockSpec((pl.Element(1), pl.Element(tn), pl.Element(d)),
                            lambda i, c, n=n: (layer, pl.multiple_of(OFF_GATE + n * d + c * tn, GATE_ROW_ALIGN), 0))
               for n in range(N_BRANCH)]
    b_specs = [pl.BlockSpec((None, None, BRANCH_W, tn), lambda i, c, n=n: (layer, n, 0, c))
               for n in range(N_BRANCH)]
    return pl.pallas_call(
        _merge_body,
        grid=(s // tm, ct),
        in_specs=[pl.BlockSpec((tm, d), lambda i, c: (i, 0))] + [z_spec] * N_BRANCH + g_specs + b_specs,
        out_specs=pl.BlockSpec((tm, tn), lambda i, c: (i, c)),
        out_shape=jax.ShapeDtypeStruct((s, d), BF16),
        compiler_params=_params("parallel", "arbitrary"),
        name="merge",
    )(h, *zs, *([wt] * N_BRANCH), *([w_branch] * N_BRANCH))


def _out_proj_body(x_ref, y_ref, w_ref, o_ref):
    o_ref[...] = x_ref[...] + jnp.dot(y_ref[...], w_ref[...].astype(BF16), preferred_element_type=F32)


def out_proj(x, y, w, layer, *, tm=1024, tn=1024):
    s, d = x.shape
    return pl.pallas_call(
        _out_proj_body,
        grid=(d // tn, s // tm),
        in_specs=[
            pl.BlockSpec((tm, tn), lambda j, i: (i, j)),
            pl.BlockSpec((tm, d), lambda j, i: (i, 0)),
            pl.BlockSpec((None, d, tn), lambda j, i: (layer, 0, j)),
        ],
        out_specs=pl.BlockSpec((tm, tn), lambda j, i: (i, j)),
        out_shape=jax.ShapeDtypeStruct((s, d), F32),
        compiler_params=_params("arbitrary", "arbitrary"),
        name="out_proj",
    )(x, y, w)


def _cast_body(x_ref, o_ref):
    o_ref[...] = x_ref[...].astype(BF16)


def cast_rows(wt, *, tr=1024):
    nl, n, d = wt.shape
    return pl.pallas_call(
        _cast_body,
        grid=(nl, pl.cdiv(n, tr)),
        in_specs=[pl.BlockSpec((None, tr, d), lambda l, r: (l, r, 0))],
        out_specs=pl.BlockSpec((None, tr, d), lambda l, r: (l, r, 0)),
        out_shape=jax.ShapeDtypeStruct((nl, n, d), BF16),
        compiler_params=_params("parallel", "parallel"),
        name="cast_rows",
    )(wt)


def _rope_tables(s):
    pos = jnp.arange(s, dtype=F32)
    inv_freq = ROPE_THETA ** (-jnp.arange(0, MLA_ROPE, 2, dtype=F32) / MLA_ROPE)
    ang = pos[:, None] * inv_freq[None, :]
    cos, sin = jnp.cos(ang), jnp.sin(ang)
    cos4 = jnp.tile(jnp.concatenate([cos, cos], axis=-1), (1, MLA_HEADS))
    sin4 = jnp.tile(jnp.concatenate([-sin, sin], axis=-1), (1, MLA_HEADS))
    return cos4, sin4


def _split_heads(w, widths):
    per_head = sum(widths)
    w3 = w.reshape(w.shape[0], MLA_HEADS, per_head)
    parts, off = [], 0
    for wd in widths:
        parts.append(w3[:, :, off:off + wd].reshape(w.shape[0], MLA_HEADS * wd))
        off += wd
    return jnp.concatenate(parts, axis=1)


def kernel(x, ffn1_norm, ffn1_w_gate, ffn1_w_up, ffn1_w_down, mix_norm, w_in, pool_w, pool_scale,
           dil_q_norm, dil_k_norm, sgu_v_norm, sgu_w, sgu_b, mla_q_a_norm, mla_w_uq, mla_kv_a_norm,
           mla_w_ukv, mla_q_norm, mla_k_norm, w_branch, w_out, ffn2_norm, ffn2_w_gate, ffn2_w_up,
           ffn2_w_down):
    b, s, d = x.shape
    assert d == D_MODEL and s % SEQ_MULTIPLE == 0, (x.shape, SEQ_MULTIPLE)
    assert w_in.shape[2] == OFF_GATE + N_BRANCH * D_MODEL and ffn1_w_gate.shape[2] % 512 == 0
    cos4, sin4 = _rope_tables(s)
    w_main = cast_rows(jnp.swapaxes(w_in, 1, 2))
    pool_wb, sgu_wb = pool_w.astype(BF16), sgu_w.astype(BF16)
    ones = jnp.ones((BRANCH_W,), F32)

    outs = []
    for bi in range(b):
        xb = x[bi]
        for l in range(DEPTH):
            xb = ffn(xb, ffn1_norm[l], ffn1_w_gate, ffn1_w_up, ffn1_w_down, l)

            p_rest, h = proj_rest(xb, mix_norm[l], w_main, l)
            dil_gains = jnp.stack([jnp.tile(dil_q_norm[l], DIL_HEADS_PER_GROUP),
                                   jnp.tile(dil_k_norm[l], DIL_HEADS_PER_GROUP), ones]).reshape(3, 1, BRANCH_W)
            qkvs = proj_dil(h, w_main, dil_gains, l)
            z_a = pool_mixer(p_rest, pool_wb, pool_scale[l], l)
            o_l = [dil_attn(qkvs[g], g) for g in range(DIL_GROUPS)]
            z_b = dil_combine([o for o, _ in o_l], [ls for _, ls in o_l])
            b_full = jnp.broadcast_to(sgu_b[l][:, :, None], (SGU_GROUPS, SGU_CHUNK, LANE))
            z_c = sgu_mixer(p_rest, sgu_v_norm[l], sgu_wb, b_full, l)
            q, k, v = mla_prep(
                p_rest, cos4, sin4, mla_q_a_norm[l],
                _split_heads(mla_w_uq[l], (MLA_NOPE, MLA_ROPE)).astype(BF16),
                mla_kv_a_norm[l], _split_heads(mla_w_ukv[l], (MLA_NOPE, MLA_V)).astype(BF16),
                jnp.tile(mla_q_norm[l][:MLA_NOPE], MLA_HEADS).reshape(1, -1),
                jnp.tile(mla_q_norm[l][MLA_NOPE:], MLA_HEADS).reshape(1, -1),
                jnp.tile(mla_k_norm[l][:MLA_NOPE], MLA_HEADS).reshape(1, -1),
                jnp.tile(mla_k_norm[l][MLA_NOPE:], MLA_HEADS).reshape(1, -1))
            z_d = mla_attn(q, k, v)
            merged = merge(h, (z_a, z_b, z_c, z_d), w_main, w_branch, l)
            xb = out_proj(xb, merged, w_out, l)

            xb = ffn(xb, ffn2_norm[l], ffn2_w_gate, ffn2_w_up, ffn2_w_down, l)
        outs.append(xb)
    return jnp.stack(outs, axis=0)
```

```python
import functools
import math

import numpy as np
import jax
import jax.numpy as jnp
from jax import lax
from jax.experimental import pallas as pl
from jax.experimental.pallas import tpu as pltpu

F32 = jnp.float32
BF16 = jnp.bfloat16

D_MODEL = 2048
DEPTH = 4
N_BRANCH = 4
BRANCH_W = 512
POOL_WINDOWS = (2, 4, 8, 16)
POOL_HALO = max(POOL_WINDOWS) // 2
DIL_PATTERNS = ((128, 1), (512, 4), (2048, 16))
DIL_GROUPS = len(DIL_PATTERNS)
DIL_HEADS = 12
DIL_HEADS_PER_GROUP = 4
DIL_HEAD_DIM = 128
DIL_QKV = 3 * DIL_HEADS * DIL_HEAD_DIM
DIL_SIDE = 64
SGU_CHUNK = 128
SGU_GROUPS = 4
MLA_HEADS = 4
MLA_Q_LORA = 384
MLA_KV_LORA = 128
MLA_NOPE = 128
MLA_ROPE = 64
MLA_V = 128
MLA_QK = MLA_NOPE + MLA_ROPE
MLA_PAD_QK = 256
MLA_HEADS_PER_STEP = 4
ROPE_THETA = 10000.0
EPS = 1e-6
NEG_BIG = -1e30
MLA_EXP2_SCALE = (MLA_QK ** -0.5) * math.log2(math.e)

OFF_POOL = 0
OFF_DIL = OFF_POOL + BRANCH_W
OFF_SGU = OFF_DIL + DIL_QKV
OFF_MLA_Q = OFF_SGU + 2 * BRANCH_W
OFF_MLA_KV = OFF_MLA_Q + MLA_Q_LORA
OFF_MLA_KR = OFF_MLA_KV + MLA_KV_LORA
OFF_GATE = OFF_MLA_KR + MLA_ROPE

LANE = 128
REST_POOL = 0
REST_SGU = REST_POOL + BRANCH_W
REST_CQ = REST_SGU + 2 * BRANCH_W
REST_CKV = REST_CQ + MLA_Q_LORA
REST_KR = REST_CKV + MLA_KV_LORA
REST_W = REST_KR + LANE

VMEM_LIMIT = 56 * 1024 * 1024
MIXER_ROWS = 512
SEQ_MULTIPLE = 2048


def _params(*sem):
    return pltpu.CompilerParams(dimension_semantics=sem, vmem_limit_bytes=VMEM_LIMIT)


def _rms(xf, g):
    return xf * lax.rsqrt(jnp.mean(xf * xf, axis=-1, keepdims=True) + EPS) * g


def _dot_nt(a, b_t):
    return lax.dot_general(a, b_t, (((1,), (1,)), ((), ())), preferred_element_type=F32)


def _part(body, in_specs, args, out_specs, out_shapes, scratch=()):
    return dict(body=body, in_specs=list(in_specs), args=list(args), out_specs=list(out_specs),
                out_shapes=list(out_shapes), scratch=list(scratch))


def _run_parts(parts, grid, name):
    n_in = [len(p["in_specs"]) for p in parts]
    n_out = [len(p["out_specs"]) for p in parts]
    n_scr = [len(p["scratch"]) for p in parts]

    def body(*refs):
        ins, outs, scr = refs[:sum(n_in)], refs[sum(n_in):sum(n_in) + sum(n_out)], refs[sum(n_in) + sum(n_out):]
        a = b = c = 0
        for p, ni, no, ns in zip(parts, n_in, n_out, n_scr):
            p["body"](*ins[a:a + ni], *outs[b:b + no], *scr[c:c + ns])
            a, b, c = a + ni, b + no, c + ns

    res = pl.pallas_call(
        body,
        grid=grid,
        in_specs=[x for p in parts for x in p["in_specs"]],
        out_specs=[x for p in parts for x in p["out_specs"]],
        out_shape=[x for p in parts for x in p["out_shapes"]],
        scratch_shapes=[x for p in parts for x in p["scratch"]],
        compiler_params=_params(*(("parallel",) * len(grid))),
        name=name,
    )(*[x for p in parts for x in p["args"]])
    out, b = [], 0
    for no in n_out:
        out.append(tuple(res[b:b + no]))
        b += no
    return out


FFN_PREFETCH_STEP = 1


def _ffn_body(x_hbm, g_ref, wg_ref, wu_ref, wd_ref, o_hbm, acc_ref, h_ref, in_sem, out_sem):
    i, f = pl.program_id(0), pl.program_id(1)
    ni, nf = pl.num_programs(0), pl.num_programs(1)
    tm = acc_ref.shape[1]
    slot = lax.rem(i, 2)
    other = 1 - slot

    def x_copy(tile, buf):
        return pltpu.make_async_copy(x_hbm.at[pl.ds(tile * tm, tm), :], acc_ref.at[buf], in_sem.at[buf])

    def out_copy(tile, buf):
        return pltpu.make_async_copy(acc_ref.at[buf], o_hbm.at[pl.ds(tile * tm, tm), :], out_sem.at[buf])

    @pl.when(f == 0)
    def _():
        @pl.when(i == 0)
        def _():
            x_copy(i, slot).start()
        x_copy(i, slot).wait()
        h_ref[...] = _rms(acc_ref[slot], g_ref[...]).astype(BF16)

    @pl.when((f == FFN_PREFETCH_STEP) & (i + 1 < ni))
    def _():
        @pl.when(i > 0)
        def _():
            out_copy(i - 1, other).wait()
        x_copy(i + 1, other).start()

    h = h_ref[...]
    g = jnp.dot(h, wg_ref[...].astype(BF16), preferred_element_type=F32)
    u = jnp.dot(h, wu_ref[...].astype(BF16), preferred_element_type=F32)
    a = (0.5 * (jax.nn.silu(g) * u)).astype(BF16)
    acc_ref[slot] += jnp.dot(a, wd_ref[...].astype(BF16), preferred_element_type=F32)

    @pl.when(f == nf - 1)
    def _():
        out_copy(i, slot).start()

        @pl.when(i == ni - 1)
        def _():
            @pl.when(ni > 1)
            def _():
                out_copy(i - 1, other).wait()
            out_copy(i, slot).wait()


def ffn(x, norm_g, wg, wu, wd, layer, *, tm=1024, tf=512):
    s, d = x.shape
    f = wg.shape[2]
    assert f // tf > FFN_PREFETCH_STEP
    return pl.pallas_call(
        _ffn_body,
        grid=(s // tm, f // tf),
        in_specs=[
            pl.BlockSpec(memory_space=pl.ANY),
            pl.BlockSpec((1, d), lambda i, j: (0, 0)),
            pl.BlockSpec((None, d, tf), lambda i, j: (layer, 0, j)),
            pl.BlockSpec((None, d, tf), lambda i, j: (layer, 0, j)),
            pl.BlockSpec((None, tf, d), lambda i, j: (layer, j, 0)),
        ],
        out_specs=pl.BlockSpec(memory_space=pl.ANY),
        out_shape=jax.ShapeDtypeStruct((s, d), F32),
        scratch_shapes=[pltpu.VMEM((2, tm, d), F32), pltpu.VMEM((tm, d), BF16),
                        pltpu.SemaphoreType.DMA((2,)), pltpu.SemaphoreType.DMA((2,))],
        compiler_params=_params("arbitrary", "arbitrary"),
        name="ffn",
    )(x, norm_g.reshape(1, d), wg, wu, wd)


def _proj_rest_body(x_ref, g_ref, wp_ref, ws_ref, wq_ref, wkv_ref, wkr_ref, p_ref, h_ref):
    h = _rms(x_ref[...], g_ref[...]).astype(BF16)
    h_ref[...] = h
    for w_ref, off in ((wp_ref, REST_POOL), (ws_ref, REST_SGU), (wq_ref, REST_CQ), (wkv_ref, REST_CKV),
                       (wkr_ref, REST_KR)):
        p_ref[:, off:off + w_ref.shape[0]] = _dot_nt(h, w_ref[...])


def proj_rest(x, norm_g, w_main, layer, *, tm=512):
    s, d = x.shape

    def wspec(off, width):
        return pl.BlockSpec((None, width, d), lambda i: (layer, off // width, 0))

    return pl.pallas_call(
        _proj_rest_body,
        grid=(s // tm,),
        in_specs=[
            pl.BlockSpec((tm, d), lambda i: (i, 0)),
            pl.BlockSpec((1, d), lambda i: (0, 0)),
            wspec(OFF_POOL, BRANCH_W),
            wspec(OFF_SGU, 2 * BRANCH_W),
            wspec(OFF_MLA_Q, MLA_Q_LORA),
            wspec(OFF_MLA_KV, MLA_KV_LORA),
            wspec(OFF_MLA_KR, LANE),
        ],
        out_specs=[
            pl.BlockSpec((tm, REST_W), lambda i: (i, 0)),
            pl.BlockSpec((tm, d), lambda i: (i, 0)),
        ],
        out_shape=[jax.ShapeDtypeStruct((s, REST_W), F32), jax.ShapeDtypeStruct((s, d), BF16)],
        compiler_params=_params("parallel"),
        name="proj_rest",
    )(x, norm_g.reshape(1, d), w_main, w_main, w_main, w_main, w_main)


DIL_SCRATCH_PITCH = {1: 1, 4: 4, 16: 24}


def _proj_dil_body(h_ref, w0_ref, w1_ref, w2_ref, g_ref, o0_ref, o1_ref, o2_ref, scr1_ref, scr2_ref):
    normed = pl.program_id(0) < 2
    h = h_ref[...]
    tm = h.shape[0]
    for w_ref, o_ref, scr_ref, (_, d) in zip((w0_ref, w1_ref, w2_ref), (o0_ref, o1_ref, o2_ref),
                                             (None, scr1_ref, scr2_ref), DIL_PATTERNS):
        pitch = DIL_SCRATCH_PITCH[d]
        acc = _dot_nt(h, w_ref[...])
        for a in range(DIL_HEADS_PER_GROUP):
            sl = slice(a * DIL_HEAD_DIM, (a + 1) * DIL_HEAD_DIM)
            xh = acc[:, sl]
            r = lax.rsqrt(jnp.mean(xh * xh, axis=-1, keepdims=True) + EPS)
            y = xh * jnp.where(normed, r, 1.0) * g_ref[:, sl]
            if d == 1:
                o_ref[0, :, sl] = y.astype(BF16)
            elif pitch == d:
                scr_ref[a] = y
            else:
                for k in range(tm // d):
                    scr_ref[a, k * pitch:k * pitch + d, :] = y[k * d:(k + 1) * d, :]
        if d > 1:
            for r in range(d):
                for a in range(DIL_HEADS_PER_GROUP):
                    sl = slice(a * DIL_HEAD_DIM, (a + 1) * DIL_HEAD_DIM)
                    o_ref[r, :, sl] = scr_ref[a, pl.ds(r, tm // d, stride=pitch), :].astype(BF16)


def proj_dil(h, w_main, gains, layer, *, tm=1024):
    s, dm = h.shape
    first = OFF_DIL // BRANCH_W

    def wspec(g):
        return pl.BlockSpec((None, BRANCH_W, dm), lambda j, i: (layer, first + j * DIL_GROUPS + g, 0))

    out_specs, out_shapes, scratch = [], [], []
    for _, d in DIL_PATTERNS:
        out_specs.append(pl.BlockSpec((d, tm // d, BRANCH_W), lambda j, i: (0, i, j)))
        out_shapes.append(jax.ShapeDtypeStruct((d, s // d, 3 * BRANCH_W), BF16))
        if d > 1:
            scratch.append(pltpu.VMEM((DIL_HEADS_PER_GROUP, (tm // d) * DIL_SCRATCH_PITCH[d], DIL_HEAD_DIM), F32))
    return pl.pallas_call(
        _proj_dil_body,
        grid=(3, s // tm),
        in_specs=[
            pl.BlockSpec((tm, dm), lambda j, i: (i, 0)),
            wspec(0), wspec(1), wspec(2),
            pl.BlockSpec((None, 1, BRANCH_W), lambda j, i: (j, 0, 0)),
        ],
        out_specs=out_specs,
        out_shape=out_shapes,
        scratch_shapes=scratch,
        compiler_params=_params("arbitrary", "arbitrary"),
        name="proj_dil",
    )(h, w_main, w_main, w_main, gains)


def _pool_body(prev_ref, cur_ref, next_ref, w_ref, scale_ref, o_ref, ext_ref, *, seq):
    i = pl.program_id(0)
    tm = cur_ref.shape[0]
    cur = cur_ref[...]
    ext_ref[0:POOL_HALO, :] = jnp.where(i > 0, prev_ref[...], 0.0)
    ext_ref[POOL_HALO:POOL_HALO + tm, :] = cur
    ext_ref[POOL_HALO + tm:2 * POOL_HALO + tm, :] = jnp.where(
        i < pl.num_programs(0) - 1, next_ref[...], 0.0)
    t = i * tm + lax.broadcasted_iota(jnp.int32, (tm, 1), 0)
    for g, w in enumerate(POOL_WINDOWS):
        sl = slice(g * LANE, (g + 1) * LANE)
        tot = jnp.zeros((tm, LANE), F32)
        for o in range(-(w // 2), w // 2):
            tot = tot + ext_ref[POOL_HALO + o:POOL_HALO + o + tm, sl]
        cnt = (jnp.minimum(t + w // 2, seq) - jnp.maximum(t - w // 2, 0)).astype(F32)
        dev = (tot / cnt - cur[:, sl]).astype(BF16)
        y = jnp.dot(dev, w_ref[g], preferred_element_type=F32) * scale_ref[:, sl]
        o_ref[:, sl] = y.astype(BF16)


def pool_part(p_rest, pool_w, pool_scale, layer, tm):
    s = p_rest.shape[0]
    hb = tm // POOL_HALO
    last = s // POOL_HALO - 1
    n_win = len(POOL_WINDOWS)
    return _part(
        functools.partial(_pool_body, seq=s),
        [
            pl.BlockSpec((POOL_HALO, BRANCH_W), lambda i: (jnp.maximum(i * hb - 1, 0), 0)),
            pl.BlockSpec((tm, BRANCH_W), lambda i: (i, 0)),
            pl.BlockSpec((POOL_HALO, BRANCH_W), lambda i: (jnp.minimum((i + 1) * hb, last), 0)),
            pl.BlockSpec((None, n_win, LANE, LANE), lambda i: (layer, 0, 0, 0)),
            pl.BlockSpec((1, BRANCH_W), lambda i: (0, 0)),
        ],
        [p_rest, p_rest, p_rest, pool_w, pool_scale.reshape(1, BRANCH_W)],
        [pl.BlockSpec((tm, BRANCH_W), lambda i: (i, 0))],
        [jax.ShapeDtypeStruct((s, BRANCH_W), BF16)],
        [pltpu.VMEM((tm + 2 * POOL_HALO, BRANCH_W), F32)],
    )


def _alibi_slopes():
    n = DIL_HEADS
    return np.exp2(np.float32(-8.0) * np.arange(1, n + 1, dtype=np.float32) / np.float32(n))


LSE_LANES = LANE // DIL_HEADS_PER_GROUP


def _dil_attn_body(cur_ref, prev_ref, next_ref, o_ref, lse_ref, *, sub_len, penalties):
    i = pl.program_id(1)
    t = cur_ref.shape[0]
    blk = prev_ref.shape[0]
    nk = blk + 2 * DIL_SIDE
    row = lax.broadcasted_iota(jnp.int32, (blk, nk), 0)
    col = lax.broadcasted_iota(jnp.int32, (blk, nk), 1)
    dist = jnp.abs(col - DIL_SIDE - row)
    in_band = dist <= DIL_SIDE
    distf = dist.astype(F32)
    lane_head = lax.broadcasted_iota(jnp.int32, (1, LANE), 1) // LSE_LANES
    lse_tiles = [jnp.zeros((blk, LANE), F32) for _ in range(t // blk)]
    for h in range(DIL_HEADS_PER_GROUP):
        ql = slice(h * DIL_HEAD_DIM, (h + 1) * DIL_HEAD_DIM)
        kl = slice(BRANCH_W + h * DIL_HEAD_DIM, BRANCH_W + (h + 1) * DIL_HEAD_DIM)
        vl = slice(2 * BRANCH_W + h * DIL_HEAD_DIM, 2 * BRANCH_W + (h + 1) * DIL_HEAD_DIM)
        k_ext = jnp.concatenate([prev_ref[blk - DIL_SIDE:, kl], cur_ref[:, kl], next_ref[:DIL_SIDE, kl]], axis=0)
        v_ext = jnp.concatenate([prev_ref[blk - DIL_SIDE:, vl], cur_ref[:, vl], next_ref[:DIL_SIDE, vl]], axis=0)
        bias = jnp.where(in_band, -penalties[h] * distf, NEG_BIG)
        for b in range(t // blk):
            rows = slice(b * blk, (b + 1) * blk)
            kpos = i * t + b * blk + col - DIL_SIDE
            s = lax.dot_general(cur_ref[rows, ql], k_ext[b * blk:b * blk + nk], (((1,), (1,)), ((), ())),
                                preferred_element_type=F32) * (DIL_HEAD_DIM ** -0.5)
            s = jnp.where((kpos >= 0) & (kpos < sub_len), s + bias, NEG_BIG)
            m = jnp.max(s, axis=-1, keepdims=True)
            p = jnp.exp(s - m)
            den = jnp.sum(p, axis=-1, keepdims=True)
            o_ref[rows, ql] = jnp.dot((p / den).astype(BF16), v_ext[b * blk:b * blk + nk],
                                      preferred_element_type=F32).astype(BF16)
            lse_tiles[b] = jnp.where(lane_head == h, m + jnp.log(den), lse_tiles[b])
    for b, tile in enumerate(lse_tiles):
        lse_ref[b * blk:(b + 1) * blk, :] = tile


def dil_attn(qkv, group, *, blk=128, max_blocks=16):
    d, sub_len, width = qkv.shape
    n_blk = sub_len // blk
    qb = min(max_blocks, n_blk)
    t = blk * qb
    slopes = _alibi_slopes()[group * DIL_HEADS_PER_GROUP:(group + 1) * DIL_HEADS_PER_GROUP]
    penalties = tuple(float(np.float32(x) * np.float32(d)) for x in slopes)

    def halo(step):
        return pl.BlockSpec((None, blk, width), lambda r, i: (r, jnp.clip(i * qb + step, 0, n_blk - 1), 0))

    return pl.pallas_call(
        functools.partial(_dil_attn_body, sub_len=sub_len, penalties=penalties),
        grid=(d, sub_len // t),
        in_specs=[pl.BlockSpec((None, t, width), lambda r, i: (r, i, 0)), halo(-1), halo(qb)],
        out_specs=[pl.BlockSpec((None, t, BRANCH_W), lambda r, i: (r, i, 0)),
                   pl.BlockSpec((None, t, LANE), lambda r, i: (r, i, 0))],
        out_shape=[jax.ShapeDtypeStruct((d, sub_len, BRANCH_W), BF16),
                   jax.ShapeDtypeStruct((d, sub_len, LANE), F32)],
        compiler_params=_params("parallel", "parallel"),
        name=f"dil_attn_g{group}",
    )(qkv, qkv, qkv)


def _dil_combine_body(o0, o1, o2, l0, l1, l2, z_ref, so1, so2, sl1, sl2):
    tm = z_ref.shape[0]
    for src, scr, (_, d) in ((o1, so1, DIL_PATTERNS[1]), (o2, so2, DIL_PATTERNS[2])):
        for r in range(d):
            for a in range(DIL_HEADS_PER_GROUP):
                sl = slice(a * DIL_HEAD_DIM, (a + 1) * DIL_HEAD_DIM)
                scr[a, pl.ds(r, tm // d, stride=d), :] = src[r, :, sl].astype(F32)
    for src, scr, (_, d) in ((l1, sl1, DIL_PATTERNS[1]), (l2, sl2, DIL_PATTERNS[2])):
        for r in range(d):
            scr[pl.ds(r, tm // d, stride=d), :] = src[r]
    la, lb, lc = l0[0], sl1[...], sl2[...]
    m = jnp.maximum(jnp.maximum(la, lb), lc)
    ea, eb, ec = jnp.exp(la - m), jnp.exp(lb - m), jnp.exp(lc - m)
    tot = ea + eb + ec
    wa, wb, wc = ea / tot, eb / tot, ec / tot
    for a in range(DIL_HEADS_PER_GROUP):
        sl = slice(a * DIL_HEAD_DIM, (a + 1) * DIL_HEAD_DIM)
        one = slice(a * LSE_LANES, a * LSE_LANES + 1)
        shape = (tm, DIL_HEAD_DIM)
        z = (jnp.broadcast_to(wa[:, one], shape) * o0[0, :, sl].astype(F32)
             + jnp.broadcast_to(wb[:, one], shape) * so1[a]
             + jnp.broadcast_to(wc[:, one], shape) * so2[a])
        z_ref[:, sl] = z.astype(BF16)


def dil_combine_part(outs, lses, tm):
    s = outs[0].shape[1]
    o_specs = [pl.BlockSpec((d, tm // d, BRANCH_W), lambda i: (0, i, 0)) for _, d in DIL_PATTERNS]
    l_specs = [pl.BlockSpec((d, tm // d, LANE), lambda i: (0, i, 0)) for _, d in DIL_PATTERNS]
    slab = pltpu.VMEM((DIL_HEADS_PER_GROUP, tm, DIL_HEAD_DIM), F32)
    return _part(
        _dil_combine_body,
        o_specs + l_specs,
        [*outs, *lses],
        [pl.BlockSpec((tm, BRANCH_W), lambda i: (i, 0))],
        [jax.ShapeDtypeStruct((s, BRANCH_W), BF16)],
        [slab, slab, pltpu.VMEM((tm, LANE), F32), pltpu.VMEM((tm, LANE), F32)],
    )


def _sgu_body(u_ref, v_ref, g_ref, ws_ref, b_ref, o_ref):
    tm = u_ref.shape[0]
    u = jax.nn.gelu(u_ref[...])
    vn = _rms(jax.nn.gelu(v_ref[...]), g_ref[...]).astype(BF16)
    for n in range(tm // SGU_CHUNK):
        rows = slice(n * SGU_CHUNK, (n + 1) * SGU_CHUNK)
        for g in range(SGU_GROUPS):
            cols = slice(g * LANE, (g + 1) * LANE)
            mixed = jnp.dot(ws_ref[g], vn[rows, cols], preferred_element_type=F32) + b_ref[g]
            o_ref[rows, cols] = (u[rows, cols] * mixed).astype(BF16)


def sgu_part(p_rest, v_norm_g, ws, b_full, layer, tm):
    s = p_rest.shape[0]
    return _part(
        _sgu_body,
        [
            pl.BlockSpec((tm, BRANCH_W), lambda i: (i, REST_SGU // BRANCH_W)),
            pl.BlockSpec((tm, BRANCH_W), lambda i: (i, REST_SGU // BRANCH_W + 1)),
            pl.BlockSpec((1, BRANCH_W), lambda i: (0, 0)),
            pl.BlockSpec((None, SGU_GROUPS, SGU_CHUNK, SGU_CHUNK), lambda i: (layer, 0, 0, 0)),
            pl.BlockSpec((SGU_GROUPS, SGU_CHUNK, LANE), lambda i: (0, 0, 0)),
        ],
        [p_rest, p_rest, v_norm_g.reshape(1, BRANCH_W), ws, b_full],
        [pl.BlockSpec((tm, BRANCH_W), lambda i: (i, 0))],
        [jax.ShapeDtypeStruct((s, BRANCH_W), BF16)],
    )


def _swap_rope_halves(y):
    width = y.shape[1]
    lane = lax.broadcasted_iota(jnp.int32, y.shape, 1)
    first_half = (lane % MLA_ROPE) < MLA_ROPE // 2
    return jnp.where(first_half, pltpu.roll(y, width - MLA_ROPE // 2, 1),
                     pltpu.roll(y, MLA_ROPE // 2, 1))


def _mla_prep_body(cq_ref, ckv_ref, kr_ref, cos_ref, sin_ref, qa_g, wuq_ref, kva_g, wukv_ref,
                   gqn_ref, gqr_ref, gkn_ref, gkr_ref, q_ref, k_ref, v_ref):
    nope_w = MLA_HEADS * MLA_NOPE
    qall = jnp.dot(_rms(cq_ref[...], qa_g[...]).astype(BF16), wuq_ref[...], preferred_element_type=F32)
    kv = jnp.dot(_rms(ckv_ref[...], kva_g[...]).astype(BF16), wukv_ref[...], preferred_element_type=F32)
    qn, qr = qall[:, :nope_w], qall[:, nope_w:]
    kn = kv[:, :nope_w]
    v_ref[...] = kv[:, nope_w:].astype(BF16)
    kr_blk = kr_ref[...]
    lane128 = lax.broadcasted_iota(jnp.int32, kr_blk.shape, 1)
    kr2 = jnp.where(lane128 < MLA_ROPE, kr_blk, pltpu.roll(kr_blk, MLA_ROPE, 1))
    kr = jnp.concatenate([kr2] * (MLA_HEADS // 2), axis=1)

    rope_w = MLA_HEADS * MLA_ROPE
    head_of_lane = lax.broadcasted_iota(jnp.int32, (1, rope_w), 1) // MLA_ROPE
    qr_sq = qr * qr
    kr_ssq = jnp.sum(jnp.where(head_of_lane == 0, kr * kr, 0.0), axis=-1, keepdims=True)
    rq, rk = [], []
    for h in range(MLA_HEADS):
        sl = slice(h * MLA_NOPE, (h + 1) * MLA_NOPE)
        q_ssq = (jnp.sum(qn[:, sl] * qn[:, sl], axis=-1, keepdims=True)
                 + jnp.sum(jnp.where(head_of_lane == h, qr_sq, 0.0), axis=-1, keepdims=True))
        k_ssq = jnp.sum(kn[:, sl] * kn[:, sl], axis=-1, keepdims=True) + kr_ssq
        rq.append(lax.rsqrt(q_ssq / MLA_QK + EPS))
        rk.append(lax.rsqrt(k_ssq / MLA_QK + EPS))

    def per_lane(rs):
        out = jnp.zeros((kr.shape[0], rope_w), F32)
        for h in range(MLA_HEADS):
            out = jnp.where(head_of_lane == h, rs[h], out)
        return out

    def rotate(x, rs, g_ref):
        y = x * per_lane(rs) * g_ref[...]
        return y * cos_ref[...] + _swap_rope_halves(y) * sin_ref[...]

    q_rope = rotate(qr, rq, gqr_ref) * MLA_EXP2_SCALE
    k_rope = rotate(kr, rk, gkr_ref)
    lane_tile = lax.broadcasted_iota(jnp.int32, (1, LANE), 1) // MLA_ROPE
    for h in range(MLA_HEADS):
        sl = slice(h * MLA_NOPE, (h + 1) * MLA_NOPE)
        base = h * MLA_PAD_QK
        tile = slice((h // 2) * LANE, (h // 2 + 1) * LANE)
        q_ref[:, base:base + MLA_NOPE] = (qn[:, sl] * rq[h] * gqn_ref[:, sl] * MLA_EXP2_SCALE).astype(BF16)
        k_ref[:, base:base + MLA_NOPE] = (kn[:, sl] * rk[h] * gkn_ref[:, sl]).astype(BF16)
        q_ref[:, base + MLA_NOPE:base + MLA_PAD_QK] = jnp.where(
            lane_tile == h % 2, q_rope[:, tile], 0.0).astype(BF16)
        k_ref[:, base + MLA_NOPE:base + MLA_PAD_QK] = k_rope[:, tile].astype(BF16)


def mla_prep_part(p_rest, cos4, sin4, qa_g, wuq, kva_g, wukv, gqn, gqr, gkn, gkr, tm):
    s = p_rest.shape[0]
    rope_w = MLA_HEADS * MLA_ROPE

    def const(shape):
        return pl.BlockSpec(shape, lambda i: (0,) * len(shape))

    widths = (MLA_HEADS * MLA_PAD_QK, MLA_HEADS * MLA_PAD_QK, MLA_HEADS * MLA_V)
    return _part(
        _mla_prep_body,
        [
            pl.BlockSpec((tm, MLA_Q_LORA), lambda i: (i, REST_CQ // MLA_Q_LORA)),
            pl.BlockSpec((tm, MLA_KV_LORA), lambda i: (i, REST_CKV // MLA_KV_LORA)),
            pl.BlockSpec((tm, LANE), lambda i: (i, REST_KR // LANE)),
            pl.BlockSpec((tm, rope_w), lambda i: (i, 0)),
            pl.BlockSpec((tm, rope_w), lambda i: (i, 0)),
            const((1, MLA_Q_LORA)), const(wuq.shape), const((1, MLA_KV_LORA)), const(wukv.shape),
            const(gqn.shape), const(gqr.shape), const(gkn.shape), const(gkr.shape),
        ],
        [p_rest, p_rest, p_rest, cos4, sin4, qa_g.reshape(1, -1), wuq, kva_g.reshape(1, -1), wukv,
         gqn, gqr, gkn, gkr],
        [pl.BlockSpec((tm, w), lambda i: (i, 0)) for w in widths],
        [jax.ShapeDtypeStruct((s, w), BF16) for w in widths],
    )


def _mla_attn_body(q_ref, k_ref, v_ref, o_ref, m_ref, l_ref, acc_ref):
    j = pl.program_id(2)

    @pl.when(j == 0)
    def _():
        m_ref[...] = jnp.full(m_ref.shape, NEG_BIG, F32)
        l_ref[...] = jnp.zeros(l_ref.shape, F32)
        acc_ref[...] = jnp.zeros(acc_ref.shape, F32)

    tk = k_ref.shape[0]
    for h in range(MLA_HEADS_PER_STEP):
        qk = slice(h * MLA_PAD_QK, (h + 1) * MLA_PAD_QK)
        vs = slice(h * MLA_V, (h + 1) * MLA_V)
        s = lax.dot_general(q_ref[:, qk], k_ref[:, qk], (((1,), (1,)), ((), ())),
                            preferred_element_type=F32)
        m_prev = m_ref[h]
        m_new = jnp.maximum(m_prev, jnp.max(s, axis=-1, keepdims=True))
        alpha = jnp.exp2(m_prev - m_new)
        p = jnp.exp2(s - jnp.tile(m_new, (1, tk // LANE)))
        l_ref[h] = alpha * l_ref[h] + jnp.sum(p, axis=-1, keepdims=True)
        acc_ref[h] = alpha * acc_ref[h] + jnp.dot(p.astype(BF16), v_ref[:, vs], preferred_element_type=F32)
        m_ref[h] = m_new

    @pl.when(j == pl.num_programs(2) - 1)
    def _():
        for h in range(MLA_HEADS_PER_STEP):
            o_ref[:, h * MLA_V:(h + 1) * MLA_V] = (acc_ref[h] / l_ref[h]).astype(BF16)


def mla_attn(q, k, v, *, tq=1024, tk=2048):
    s = q.shape[0]
    hp = MLA_HEADS_PER_STEP
    stat = pltpu.VMEM((hp, tq, LANE), F32)
    return pl.pallas_call(
        _mla_attn_body,
        grid=(s // tq, MLA_HEADS // hp, s // tk),
        in_specs=[
            pl.BlockSpec((tq, hp * MLA_PAD_QK), lambda i, h, j: (i, h)),
            pl.BlockSpec((tk, hp * MLA_PAD_QK), lambda i, h, j: (j, h)),
            pl.BlockSpec((tk, hp * MLA_V), lambda i, h, j: (j, h)),
        ],
        out_specs=pl.BlockSpec((tq, hp * MLA_V), lambda i, h, j: (i, h)),
        out_shape=jax.ShapeDtypeStruct((s, MLA_HEADS * MLA_V), BF16),
        scratch_shapes=[stat, stat, stat],
        compiler_params=_params("parallel", "parallel", "arbitrary"),
        name="mla_attn",
    )(q, k, v)


def _merge_body(h_ref, *refs):
    z_refs, wg_refs, wb_refs = refs[:N_BRANCH], refs[N_BRANCH:2 * N_BRANCH], refs[2 * N_BRANCH:3 * N_BRANCH]
    o_ref = refs[3 * N_BRANCH]
    h = h_ref[...]
    acc = None
    for z_ref, wg_ref, wb_ref in zip(z_refs, wg_refs, wb_refs):
        gate = jax.nn.sigmoid(_dot_nt(h, wg_ref[0]))
        term = gate * jnp.dot(z_ref[...], wb_ref[...].astype(BF16), preferred_element_type=F32)
        acc = term if acc is None else acc + term
    o_ref[...] = acc.astype(BF16)


GATE_ROW_ALIGN = 64


def merge(h, zs, wt, w_branch, layer, *, tm=1024, tn=512):
    s, d = h.shape
    ct = d // tn
    assert OFF_GATE % GATE_ROW_ALIGN == 0 and d % GATE_ROW_ALIGN == 0 and tn % GATE_ROW_ALIGN == 0
    z_spec = pl.BlockSpec((tm, BRANCH_W), lambda i, c: (i, 0))
    g_specs = [pl.BlockSpec((pl.Element(1), pl.Element(tn), pl.Element(d)),
                            lambda i, c, n=n: (layer, pl.multiple_of(OFF_GATE + n * d + c * tn, GATE_ROW_ALIGN), 0))
               for n in range(N_BRANCH)]
    b_specs = [pl.BlockSpec((None, None, BRANCH_W, tn), lambda i, c, n=n: (layer, n, 0, c))
               for n in range(N_BRANCH)]
    return pl.pallas_call(
        _merge_body,
        grid=(s // tm, ct),
        in_specs=[pl.BlockSpec((tm, d), lambda i, c: (i, 0))] + [z_spec] * N_BRANCH + g_specs + b_specs,
        out_specs=pl.BlockSpec((tm, tn), lambda i, c: (i, c)),
        out_shape=jax.ShapeDtypeStruct((s, d), BF16),
        compiler_params=_params("parallel", "arbitrary"),
        name="merge",
    )(h, *zs, *([wt] * N_BRANCH), *([w_branch] * N_BRANCH))


def _out_proj_body(x_ref, y_ref, w_ref, o_ref):
    o_ref[...] = x_ref[...] + jnp.dot(y_ref[...], w_ref[...].astype(BF16), preferred_element_type=F32)


def out_proj(x, y, w, layer, *, tm=1024, tn=1024):
    s, d = x.shape
    return pl.pallas_call(
        _out_proj_body,
        grid=(d // tn, s // tm),
        in_specs=[
            pl.BlockSpec((tm, tn), lambda j, i: (i, j)),
            pl.BlockSpec((tm, d), lambda j, i: (i, 0)),
            pl.BlockSpec((None, d, tn), lambda j, i: (layer, 0, j)),
        ],
        out_specs=pl.BlockSpec((tm, tn), lambda j, i: (i, j)),
        out_shape=jax.ShapeDtypeStruct((s, d), F32),
        compiler_params=_params("arbitrary", "arbitrary"),
        name="out_proj",
    )(x, y, w)


def _cast_body(x_ref, o_ref):
    o_ref[...] = x_ref[...].astype(BF16)


def cast_rows(wt, *, tr=1024):
    nl, n, d = wt.shape
    return pl.pallas_call(
        _cast_body,
        grid=(nl, pl.cdiv(n, tr)),
        in_specs=[pl.BlockSpec((None, tr, d), lambda l, r: (l, r, 0))],
        out_specs=pl.BlockSpec((None, tr, d), lambda l, r: (l, r, 0)),
        out_shape=jax.ShapeDtypeStruct((nl, n, d), BF16),
        compiler_params=_params("parallel", "parallel"),
        name="cast_rows",
    )(wt)


def _rope_tables(s):
    pos = jnp.arange(s, dtype=F32)
    inv_freq = ROPE_THETA ** (-jnp.arange(0, MLA_ROPE, 2, dtype=F32) / MLA_ROPE)
    ang = pos[:, None] * inv_freq[None, :]
    cos, sin = jnp.cos(ang), jnp.sin(ang)
    cos4 = jnp.tile(jnp.concatenate([cos, cos], axis=-1), (1, MLA_HEADS))
    sin4 = jnp.tile(jnp.concatenate([-sin, sin], axis=-1), (1, MLA_HEADS))
    return cos4, sin4


def _split_heads(w, widths):
    per_head = sum(widths)
    w3 = w.reshape(w.shape[0], MLA_HEADS, per_head)
    parts, off = [], 0
    for wd in widths:
        parts.append(w3[:, :, off:off + wd].reshape(w.shape[0], MLA_HEADS * wd))
        off += wd
    return jnp.concatenate(parts, axis=1)


def kernel(x, ffn1_norm, ffn1_w_gate, ffn1_w_up, ffn1_w_down, mix_norm, w_in, pool_w, pool_scale,
           dil_q_norm, dil_k_norm, sgu_v_norm, sgu_w, sgu_b, mla_q_a_norm, mla_w_uq, mla_kv_a_norm,
           mla_w_ukv, mla_q_norm, mla_k_norm, w_branch, w_out, ffn2_norm, ffn2_w_gate, ffn2_w_up,
           ffn2_w_down):
    b, s, d = x.shape
    assert d == D_MODEL and s % SEQ_MULTIPLE == 0, (x.shape, SEQ_MULTIPLE)
    assert w_in.shape[2] == OFF_GATE + N_BRANCH * D_MODEL and ffn1_w_gate.shape[2] % 512 == 0
    cos4, sin4 = _rope_tables(s)
    w_main = cast_rows(jnp.swapaxes(w_in, 1, 2))
    pool_wb, sgu_wb = pool_w.astype(BF16), sgu_w.astype(BF16)
    ones = jnp.ones((BRANCH_W,), F32)

    outs = []
    for bi in range(b):
        xb = x[bi]
        for l in range(DEPTH):
            xb = ffn(xb, ffn1_norm[l], ffn1_w_gate, ffn1_w_up, ffn1_w_down, l)

            p_rest, h = proj_rest(xb, mix_norm[l], w_main, l)
            dil_gains = jnp.stack([jnp.tile(dil_q_norm[l], DIL_HEADS_PER_GROUP),
                                   jnp.tile(dil_k_norm[l], DIL_HEADS_PER_GROUP), ones]).reshape(3, 1, BRANCH_W)
            qkvs = proj_dil(h, w_main, dil_gains, l)
            o_l = [dil_attn(qkvs[g], g) for g in range(DIL_GROUPS)]
            b_full = jnp.broadcast_to(sgu_b[l][:, :, None], (SGU_GROUPS, SGU_CHUNK, LANE))
            (z_a,), (z_b,), (z_c,), (q, k, v) = _run_parts([
                pool_part(p_rest, pool_wb, pool_scale[l], l, MIXER_ROWS),
                dil_combine_part([o for o, _ in o_l], [ls for _, ls in o_l], MIXER_ROWS),
                sgu_part(p_rest, sgu_v_norm[l], sgu_wb, b_full, l, MIXER_ROWS),
                mla_prep_part(
                    p_rest, cos4, sin4, mla_q_a_norm[l],
                    _split_heads(mla_w_uq[l], (MLA_NOPE, MLA_ROPE)).astype(BF16),
                    mla_kv_a_norm[l], _split_heads(mla_w_ukv[l], (MLA_NOPE, MLA_V)).astype(BF16),
                    jnp.tile(mla_q_norm[l][:MLA_NOPE], MLA_HEADS).reshape(1, -1),
                    jnp.tile(mla_q_norm[l][MLA_NOPE:], MLA_HEADS).reshape(1, -1),
                    jnp.tile(mla_k_norm[l][:MLA_NOPE], MLA_HEADS).reshape(1, -1),
                    jnp.tile(mla_k_norm[l][MLA_NOPE:], MLA_HEADS).reshape(1, -1), MIXER_ROWS),
            ], (s // MIXER_ROWS,), "mixers")
            z_d = mla_attn(q, k, v)
            merged = merge(h, (z_a, z_b, z_c, z_d), w_main, w_branch, l)
            xb = out_proj(xb, merged, w_out, l)

            xb = ffn(xb, ffn2_norm[l], ffn2_w_gate, ffn2_w_up, ffn2_w_down, l)
        outs.append(xb)
    return jnp.stack(outs, axis=0)
```

```python
import functools
import math

import numpy as np
import jax
import jax.numpy as jnp
from jax import lax
from jax.experimental import pallas as pl
from jax.experimental.pallas import tpu as pltpu

F32 = jnp.float32
BF16 = jnp.bfloat16

D_MODEL = 2048
DEPTH = 4
N_BRANCH = 4
BRANCH_W = 512
POOL_WINDOWS = (2, 4, 8, 16)
POOL_HALO = max(POOL_WINDOWS) // 2
DIL_PATTERNS = ((128, 1), (512, 4), (2048, 16))
DIL_GROUPS = len(DIL_PATTERNS)
DIL_HEADS = 12
DIL_HEADS_PER_GROUP = 4
DIL_HEAD_DIM = 128
DIL_QKV = 3 * DIL_HEADS * DIL_HEAD_DIM
DIL_SIDE = 64
SGU_CHUNK = 128
SGU_GROUPS = 4
MLA_HEADS = 4
MLA_Q_LORA = 384
MLA_KV_LORA = 128
MLA_NOPE = 128
MLA_ROPE = 64
MLA_V = 128
MLA_QK = MLA_NOPE + MLA_ROPE
MLA_PAD_QK = 256
MLA_HEADS_PER_STEP = 4
ROPE_THETA = 10000.0
EPS = 1e-6
NEG_BIG = -1e30
MLA_EXP2_SCALE = (MLA_QK ** -0.5) * math.log2(math.e)

OFF_POOL = 0
OFF_DIL = OFF_POOL + BRANCH_W
OFF_SGU = OFF_DIL + DIL_QKV
OFF_MLA_Q = OFF_SGU + 2 * BRANCH_W
OFF_MLA_KV = OFF_MLA_Q + MLA_Q_LORA
OFF_MLA_KR = OFF_MLA_KV + MLA_KV_LORA
OFF_GATE = OFF_MLA_KR + MLA_ROPE

LANE = 128
REST_POOL = 0
REST_SGU = REST_POOL + BRANCH_W
REST_CQ = REST_SGU + 2 * BRANCH_W
REST_CKV = REST_CQ + MLA_Q_LORA
REST_KR = REST_CKV + MLA_KV_LORA
REST_W = REST_KR + LANE

VMEM_LIMIT = 56 * 1024 * 1024
MIXER_ROWS = 512
SEQ_MULTIPLE = 2048


def _params(*sem, allow_input_fusion=None):
    return pltpu.CompilerParams(dimension_semantics=sem, vmem_limit_bytes=VMEM_LIMIT,
                                allow_input_fusion=allow_input_fusion)


FUSIBLE_INPUT_BYTES = 2 * 1024 * 1024


def _rms(xf, g):
    return xf * lax.rsqrt(jnp.mean(xf * xf, axis=-1, keepdims=True) + EPS) * g


def _dot_nt(a, b_t):
    return lax.dot_general(a, b_t, (((1,), (1,)), ((), ())), preferred_element_type=F32)


def _part(body, in_specs, args, out_specs, out_shapes, scratch=()):
    return dict(body=body, in_specs=list(in_specs), args=list(args), out_specs=list(out_specs),
                out_shapes=list(out_shapes), scratch=list(scratch))


def _run_parts(parts, grid, name):
    n_in = [len(p["in_specs"]) for p in parts]
    n_out = [len(p["out_specs"]) for p in parts]
    n_scr = [len(p["scratch"]) for p in parts]

    def body(*refs):
        ins, outs, scr = refs[:sum(n_in)], refs[sum(n_in):sum(n_in) + sum(n_out)], refs[sum(n_in) + sum(n_out):]
        a = b = c = 0
        for p, ni, no, ns in zip(parts, n_in, n_out, n_scr):
            p["body"](*ins[a:a + ni], *outs[b:b + no], *scr[c:c + ns])
            a, b, c = a + ni, b + no, c + ns

    args = [x for p in parts for x in p["args"]]
    fusible = [x.size * x.dtype.itemsize <= FUSIBLE_INPUT_BYTES for x in args]
    res = pl.pallas_call(
        body,
        grid=grid,
        in_specs=[x for p in parts for x in p["in_specs"]],
        out_specs=[x for p in parts for x in p["out_specs"]],
        out_shape=[x for p in parts for x in p["out_shapes"]],
        scratch_shapes=[x for p in parts for x in p["scratch"]],
        compiler_params=_params(*(("parallel",) * len(grid)), allow_input_fusion=fusible),
        name=name,
    )(*args)
    out, b = [], 0
    for no in n_out:
        out.append(tuple(res[b:b + no]))
        b += no
    return out


FFN_PREFETCH_STEP = 1


def _ffn_body(x_hbm, g_ref, wg_ref, wu_ref, wd_ref, o_hbm, acc_ref, h_ref, in_sem, out_sem):
    i, f = pl.program_id(0), pl.program_id(1)
    ni, nf = pl.num_programs(0), pl.num_programs(1)
    tm = acc_ref.shape[1]
    slot = lax.rem(i, 2)
    other = 1 - slot

    def x_copy(tile, buf):
        return pltpu.make_async_copy(x_hbm.at[pl.ds(tile * tm, tm), :], acc_ref.at[buf], in_sem.at[buf])

    def out_copy(tile, buf):
        return pltpu.make_async_copy(acc_ref.at[buf], o_hbm.at[pl.ds(tile * tm, tm), :], out_sem.at[buf])

    @pl.when(f == 0)
    def _():
        @pl.when(i == 0)
        def _():
            x_copy(i, slot).start()
        x_copy(i, slot).wait()
        h_ref[...] = _rms(acc_ref[slot], g_ref[...]).astype(BF16)

    @pl.when((f == FFN_PREFETCH_STEP) & (i + 1 < ni))
    def _():
        @pl.when(i > 0)
        def _():
            out_copy(i - 1, other).wait()
        x_copy(i + 1, other).start()

    h = h_ref[...]
    g = jnp.dot(h, wg_ref[...].astype(BF16), preferred_element_type=F32)
    u = jnp.dot(h, wu_ref[...].astype(BF16), preferred_element_type=F32)
    a = (0.5 * (jax.nn.silu(g) * u)).astype(BF16)
    acc_ref[slot] += jnp.dot(a, wd_ref[...].astype(BF16), preferred_element_type=F32)

    @pl.when(f == nf - 1)
    def _():
        out_copy(i, slot).start()

        @pl.when(i == ni - 1)
        def _():
            @pl.when(ni > 1)
            def _():
                out_copy(i - 1, other).wait()
            out_copy(i, slot).wait()


def ffn(x, norm_g, wg, wu, wd, layer, *, tm=1024, tf=512):
    s, d = x.shape
    f = wg.shape[2]
    assert f // tf > FFN_PREFETCH_STEP
    return pl.pallas_call(
        _ffn_body,
        grid=(s // tm, f // tf),
        in_specs=[
            pl.BlockSpec(memory_space=pl.ANY),
            pl.BlockSpec((1, d), lambda i, j: (0, 0)),
            pl.BlockSpec((None, d, tf), lambda i, j: (layer, 0, j)),
            pl.BlockSpec((None, d, tf), lambda i, j: (layer, 0, j)),
            pl.BlockSpec((None, tf, d), lambda i, j: (layer, j, 0)),
        ],
        out_specs=pl.BlockSpec(memory_space=pl.ANY),
        out_shape=jax.ShapeDtypeStruct((s, d), F32),
        scratch_shapes=[pltpu.VMEM((2, tm, d), F32), pltpu.VMEM((tm, d), BF16),
                        pltpu.SemaphoreType.DMA((2,)), pltpu.SemaphoreType.DMA((2,))],
        compiler_params=_params("arbitrary", "arbitrary"),
        name="ffn",
    )(x, norm_g.reshape(1, d), wg, wu, wd)


def _proj_rest_body(x_ref, g_ref, wp_ref, ws_ref, wq_ref, wkv_ref, wkr_ref, p_ref, h_ref):
    h = _rms(x_ref[...], g_ref[...]).astype(BF16)
    h_ref[...] = h
    for w_ref, off in ((wp_ref, REST_POOL), (ws_ref, REST_SGU), (wq_ref, REST_CQ), (wkv_ref, REST_CKV),
                       (wkr_ref, REST_KR)):
        p_ref[:, off:off + w_ref.shape[0]] = _dot_nt(h, w_ref[...])


def proj_rest(x, norm_g, w_main, layer, *, tm=512):
    s, d = x.shape

    def wspec(off, width):
        return pl.BlockSpec((None, width, d), lambda i: (layer, off // width, 0))

    return pl.pallas_call(
        _proj_rest_body,
        grid=(s // tm,),
        in_specs=[
            pl.BlockSpec((tm, d), lambda i: (i, 0)),
            pl.BlockSpec((1, d), lambda i: (0, 0)),
            wspec(OFF_POOL, BRANCH_W),
            wspec(OFF_SGU, 2 * BRANCH_W),
            wspec(OFF_MLA_Q, MLA_Q_LORA),
            wspec(OFF_MLA_KV, MLA_KV_LORA),
            wspec(OFF_MLA_KR, LANE),
        ],
        out_specs=[
            pl.BlockSpec((tm, REST_W), lambda i: (i, 0)),
            pl.BlockSpec((tm, d), lambda i: (i, 0)),
        ],
        out_shape=[jax.ShapeDtypeStruct((s, REST_W), F32), jax.ShapeDtypeStruct((s, d), BF16)],
        compiler_params=_params("parallel"),
        name="proj_rest",
    )(x, norm_g.reshape(1, d), w_main, w_main, w_main, w_main, w_main)


DIL_SCRATCH_PITCH = {1: 1, 4: 4, 16: 24}


def _proj_dil_body(h_ref, w0_ref, w1_ref, w2_ref, g_ref, o0_ref, o1_ref, o2_ref, scr1_ref, scr2_ref):
    normed = pl.program_id(0) < 2
    h = h_ref[...]
    tm = h.shape[0]
    for w_ref, o_ref, scr_ref, (_, d) in zip((w0_ref, w1_ref, w2_ref), (o0_ref, o1_ref, o2_ref),
                                             (None, scr1_ref, scr2_ref), DIL_PATTERNS):
        pitch = DIL_SCRATCH_PITCH[d]
        acc = _dot_nt(h, w_ref[...])
        for a in range(DIL_HEADS_PER_GROUP):
            sl = slice(a * DIL_HEAD_DIM, (a + 1) * DIL_HEAD_DIM)
            xh = acc[:, sl]
            r = lax.rsqrt(jnp.mean(xh * xh, axis=-1, keepdims=True) + EPS)
            y = xh * jnp.where(normed, r, 1.0) * g_ref[:, sl]
            if d == 1:
                o_ref[0, :, sl] = y.astype(BF16)
            elif pitch == d:
                scr_ref[a] = y
            else:
                for k in range(tm // d):
                    scr_ref[a, k * pitch:k * pitch + d, :] = y[k * d:(k + 1) * d, :]
        if d > 1:
            for r in range(d):
                for a in range(DIL_HEADS_PER_GROUP):
                    sl = slice(a * DIL_HEAD_DIM, (a + 1) * DIL_HEAD_DIM)
                    o_ref[r, :, sl] = scr_ref[a, pl.ds(r, tm // d, stride=pitch), :].astype(BF16)


def proj_dil(h, w_main, gains, layer, *, tm=1024):
    s, dm = h.shape
    first = OFF_DIL // BRANCH_W

    def wspec(g):
        return pl.BlockSpec((None, BRANCH_W, dm), lambda j, i: (layer, first + j * DIL_GROUPS + g, 0))

    out_specs, out_shapes, scratch = [], [], []
    for _, d in DIL_PATTERNS:
        out_specs.append(pl.BlockSpec((d, tm // d, BRANCH_W), lambda j, i: (0, i, j)))
        out_shapes.append(jax.ShapeDtypeStruct((d, s // d, 3 * BRANCH_W), BF16))
        if d > 1:
            scratch.append(pltpu.VMEM((DIL_HEADS_PER_GROUP, (tm // d) * DIL_SCRATCH_PITCH[d], DIL_HEAD_DIM), F32))
    return pl.pallas_call(
        _proj_dil_body,
        grid=(3, s // tm),
        in_specs=[
            pl.BlockSpec((tm, dm), lambda j, i: (i, 0)),
            wspec(0), wspec(1), wspec(2),
            pl.BlockSpec((None, 1, BRANCH_W), lambda j, i: (j, 0, 0)),
        ],
        out_specs=out_specs,
        out_shape=out_shapes,
        scratch_shapes=scratch,
        compiler_params=_params("arbitrary", "arbitrary"),
        name="proj_dil",
    )(h, w_main, w_main, w_main, gains)


def _pool_body(prev_ref, cur_ref, next_ref, w_ref, scale_ref, o_ref, ext_ref, *, seq):
    i = pl.program_id(0)
    tm = cur_ref.shape[0]
    cur = cur_ref[...]
    ext_ref[0:POOL_HALO, :] = jnp.where(i > 0, prev_ref[...], 0.0)
    ext_ref[POOL_HALO:POOL_HALO + tm, :] = cur
    ext_ref[POOL_HALO + tm:2 * POOL_HALO + tm, :] = jnp.where(
        i < pl.num_programs(0) - 1, next_ref[...], 0.0)
    t = i * tm + lax.broadcasted_iota(jnp.int32, (tm, 1), 0)
    for g, w in enumerate(POOL_WINDOWS):
        sl = slice(g * LANE, (g + 1) * LANE)
        tot = jnp.zeros((tm, LANE), F32)
        for o in range(-(w // 2), w // 2):
            tot = tot + ext_ref[POOL_HALO + o:POOL_HALO + o + tm, sl]
        cnt = (jnp.minimum(t + w // 2, seq) - jnp.maximum(t - w // 2, 0)).astype(F32)
        dev = (tot / cnt - cur[:, sl]).astype(BF16)
        y = jnp.dot(dev, w_ref[g], preferred_element_type=F32) * scale_ref[:, sl]
        o_ref[:, sl] = y.astype(BF16)


def pool_part(p_rest, pool_w, pool_scale, layer, tm):
    s = p_rest.shape[0]
    hb = tm // POOL_HALO
    last = s // POOL_HALO - 1
    n_win = len(POOL_WINDOWS)
    return _part(
        functools.partial(_pool_body, seq=s),
        [
            pl.BlockSpec((POOL_HALO, BRANCH_W), lambda i: (jnp.maximum(i * hb - 1, 0), 0)),
            pl.BlockSpec((tm, BRANCH_W), lambda i: (i, 0)),
            pl.BlockSpec((POOL_HALO, BRANCH_W), lambda i: (jnp.minimum((i + 1) * hb, last), 0)),
            pl.BlockSpec((None, n_win, LANE, LANE), lambda i: (layer, 0, 0, 0)),
            pl.BlockSpec((1, BRANCH_W), lambda i: (0, 0)),
        ],
        [p_rest, p_rest, p_rest, pool_w, pool_scale.reshape(1, BRANCH_W)],
        [pl.BlockSpec((tm, BRANCH_W), lambda i: (i, 0))],
        [jax.ShapeDtypeStruct((s, BRANCH_W), BF16)],
        [pltpu.VMEM((tm + 2 * POOL_HALO, BRANCH_W), F32)],
    )


def _alibi_slopes():
    n = DIL_HEADS
    return np.exp2(np.float32(-8.0) * np.arange(1, n + 1, dtype=np.float32) / np.float32(n))


LSE_LANES = LANE // DIL_HEADS_PER_GROUP


def _dil_attn_body(cur_ref, prev_ref, next_ref, o_ref, lse_ref, *, sub_len, penalties):
    i = pl.program_id(1)
    t = cur_ref.shape[0]
    blk = prev_ref.shape[0]
    nk = blk + 2 * DIL_SIDE
    row = lax.broadcasted_iota(jnp.int32, (blk, nk), 0)
    col = lax.broadcasted_iota(jnp.int32, (blk, nk), 1)
    dist = jnp.abs(col - DIL_SIDE - row)
    in_band = dist <= DIL_SIDE
    distf = dist.astype(F32)
    lane_head = lax.broadcasted_iota(jnp.int32, (1, LANE), 1) // LSE_LANES
    lse_tiles = [jnp.zeros((blk, LANE), F32) for _ in range(t // blk)]
    for h in range(DIL_HEADS_PER_GROUP):
        ql = slice(h * DIL_HEAD_DIM, (h + 1) * DIL_HEAD_DIM)
        kl = slice(BRANCH_W + h * DIL_HEAD_DIM, BRANCH_W + (h + 1) * DIL_HEAD_DIM)
        vl = slice(2 * BRANCH_W + h * DIL_HEAD_DIM, 2 * BRANCH_W + (h + 1) * DIL_HEAD_DIM)
        k_ext = jnp.concatenate([prev_ref[blk - DIL_SIDE:, kl], cur_ref[:, kl], next_ref[:DIL_SIDE, kl]], axis=0)
        v_ext = jnp.concatenate([prev_ref[blk - DIL_SIDE:, vl], cur_ref[:, vl], next_ref[:DIL_SIDE, vl]], axis=0)
        bias = jnp.where(in_band, -penalties[h] * distf, NEG_BIG)
        for b in range(t // blk):
            rows = slice(b * blk, (b + 1) * blk)
            kpos = i * t + b * blk + col - DIL_SIDE
            s = lax.dot_general(cur_ref[rows, ql], k_ext[b * blk:b * blk + nk], (((1,), (1,)), ((), ())),
                                preferred_element_type=F32) * (DIL_HEAD_DIM ** -0.5)
            s = jnp.where((kpos >= 0) & (kpos < sub_len), s + bias, NEG_BIG)
            m = jnp.max(s, axis=-1, keepdims=True)
            p = jnp.exp(s - m)
            den = jnp.sum(p, axis=-1, keepdims=True)
            o_ref[rows, ql] = jnp.dot((p / den).astype(BF16), v_ext[b * blk:b * blk + nk],
                                      preferred_element_type=F32).astype(BF16)
            lse_tiles[b] = jnp.where(lane_head == h, m + jnp.log(den), lse_tiles[b])
    for b, tile in enumerate(lse_tiles):
        lse_ref[b * blk:(b + 1) * blk, :] = tile


def dil_attn(qkv, group, *, blk=128, max_blocks=16):
    d, sub_len, width = qkv.shape
    n_blk = sub_len // blk
    qb = min(max_blocks, n_blk)
    t = blk * qb
    slopes = _alibi_slopes()[group * DIL_HEADS_PER_GROUP:(group + 1) * DIL_HEADS_PER_GROUP]
    penalties = tuple(float(np.float32(x) * np.float32(d)) for x in slopes)

    def halo(step):
        return pl.BlockSpec((None, blk, width), lambda r, i: (r, jnp.clip(i * qb + step, 0, n_blk - 1), 0))

    return pl.pallas_call(
        functools.partial(_dil_attn_body, sub_len=sub_len, penalties=penalties),
        grid=(d, sub_len // t),
        in_specs=[pl.BlockSpec((None, t, width), lambda r, i: (r, i, 0)), halo(-1), halo(qb)],
        out_specs=[pl.BlockSpec((None, t, BRANCH_W), lambda r, i: (r, i, 0)),
                   pl.BlockSpec((None, t, LANE), lambda r, i: (r, i, 0))],
        out_shape=[jax.ShapeDtypeStruct((d, sub_len, BRANCH_W), BF16),
                   jax.ShapeDtypeStruct((d, sub_len, LANE), F32)],
        compiler_params=_params("parallel", "parallel"),
        name=f"dil_attn_g{group}",
    )(qkv, qkv, qkv)


def _dil_combine_body(o0, o1, o2, l0, l1, l2, z_ref, so1, so2, sl1, sl2):
    tm = z_ref.shape[0]
    for src, scr, (_, d) in ((o1, so1, DIL_PATTERNS[1]), (o2, so2, DIL_PATTERNS[2])):
        for r in range(d):
            for a in range(DIL_HEADS_PER_GROUP):
                sl = slice(a * DIL_HEAD_DIM, (a + 1) * DIL_HEAD_DIM)
                scr[a, pl.ds(r, tm // d, stride=d), :] = src[r, :, sl].astype(F32)
    for src, scr, (_, d) in ((l1, sl1, DIL_PATTERNS[1]), (l2, sl2, DIL_PATTERNS[2])):
        for r in range(d):
            scr[pl.ds(r, tm // d, stride=d), :] = src[r]
    la, lb, lc = l0[0], sl1[...], sl2[...]
    m = jnp.maximum(jnp.maximum(la, lb), lc)
    ea, eb, ec = jnp.exp(la - m), jnp.exp(lb - m), jnp.exp(lc - m)
    tot = ea + eb + ec
    wa, wb, wc = ea / tot, eb / tot, ec / tot
    for a in range(DIL_HEADS_PER_GROUP):
        sl = slice(a * DIL_HEAD_DIM, (a + 1) * DIL_HEAD_DIM)
        one = slice(a * LSE_LANES, a * LSE_LANES + 1)
        shape = (tm, DIL_HEAD_DIM)
        z = (jnp.broadcast_to(wa[:, one], shape) * o0[0, :, sl].astype(F32)
             + jnp.broadcast_to(wb[:, one], shape) * so1[a]
             + jnp.broadcast_to(wc[:, one], shape) * so2[a])
        z_ref[:, sl] = z.astype(BF16)


def dil_combine_part(outs, lses, tm):
    s = outs[0].shape[1]
    o_specs = [pl.BlockSpec((d, tm // d, BRANCH_W), lambda i: (0, i, 0)) for _, d in DIL_PATTERNS]
    l_specs = [pl.BlockSpec((d, tm // d, LANE), lambda i: (0, i, 0)) for _, d in DIL_PATTERNS]
    slab = pltpu.VMEM((DIL_HEADS_PER_GROUP, tm, DIL_HEAD_DIM), F32)
    return _part(
        _dil_combine_body,
        o_specs + l_specs,
        [*outs, *lses],
        [pl.BlockSpec((tm, BRANCH_W), lambda i: (i, 0))],
        [jax.ShapeDtypeStruct((s, BRANCH_W), BF16)],
        [slab, slab, pltpu.VMEM((tm, LANE), F32), pltpu.VMEM((tm, LANE), F32)],
    )


def _sgu_body(u_ref, v_ref, g_ref, ws_ref, b_ref, o_ref):
    tm = u_ref.shape[0]
    u = jax.nn.gelu(u_ref[...])
    vn = _rms(jax.nn.gelu(v_ref[...]), g_ref[...]).astype(BF16)
    for n in range(tm // SGU_CHUNK):
        rows = slice(n * SGU_CHUNK, (n + 1) * SGU_CHUNK)
        for g in range(SGU_GROUPS):
            cols = slice(g * LANE, (g + 1) * LANE)
            mixed = jnp.dot(ws_ref[g], vn[rows, cols], preferred_element_type=F32) + b_ref[g]
            o_ref[rows, cols] = (u[rows, cols] * mixed).astype(BF16)


def sgu_part(p_rest, v_norm_g, ws, b_full, layer, tm):
    s = p_rest.shape[0]
    return _part(
        _sgu_body,
        [
            pl.BlockSpec((tm, BRANCH_W), lambda i: (i, REST_SGU // BRANCH_W)),
            pl.BlockSpec((tm, BRANCH_W), lambda i: (i, REST_SGU // BRANCH_W + 1)),
            pl.BlockSpec((1, BRANCH_W), lambda i: (0, 0)),
            pl.BlockSpec((None, SGU_GROUPS, SGU_CHUNK, SGU_CHUNK), lambda i: (layer, 0, 0, 0)),
            pl.BlockSpec((SGU_GROUPS, SGU_CHUNK, LANE), lambda i: (0, 0, 0)),
        ],
        [p_rest, p_rest, v_norm_g.reshape(1, BRANCH_W), ws, b_full],
        [pl.BlockSpec((tm, BRANCH_W), lambda i: (i, 0))],
        [jax.ShapeDtypeStruct((s, BRANCH_W), BF16)],
    )


def _swap_rope_halves(y):
    width = y.shape[1]
    lane = lax.broadcasted_iota(jnp.int32, y.shape, 1)
    first_half = (lane % MLA_ROPE) < MLA_ROPE // 2
    return jnp.where(first_half, pltpu.roll(y, width - MLA_ROPE // 2, 1),
                     pltpu.roll(y, MLA_ROPE // 2, 1))


def _mla_prep_body(cq_ref, ckv_ref, kr_ref, cos_ref, sin_ref, qa_g, wuq_ref, kva_g, wukv_ref,
                   gqn_ref, gqr_ref, gkn_ref, gkr_ref, q_ref, k_ref, v_ref):
    nope_w = MLA_HEADS * MLA_NOPE
    qall = jnp.dot(_rms(cq_ref[...], qa_g[...]).astype(BF16), wuq_ref[...], preferred_element_type=F32)
    kv = jnp.dot(_rms(ckv_ref[...], kva_g[...]).astype(BF16), wukv_ref[...], preferred_element_type=F32)
    qn, qr = qall[:, :nope_w], qall[:, nope_w:]
    kn = kv[:, :nope_w]
    v_ref[...] = kv[:, nope_w:].astype(BF16)
    kr_blk = kr_ref[...]
    lane128 = lax.broadcasted_iota(jnp.int32, kr_blk.shape, 1)
    kr2 = jnp.where(lane128 < MLA_ROPE, kr_blk, pltpu.roll(kr_blk, MLA_ROPE, 1))
    kr = jnp.concatenate([kr2] * (MLA_HEADS // 2), axis=1)

    rope_w = MLA_HEADS * MLA_ROPE
    head_of_lane = lax.broadcasted_iota(jnp.int32, (1, rope_w), 1) // MLA_ROPE
    qr_sq = qr * qr
    kr_ssq = jnp.sum(jnp.where(head_of_lane == 0, kr * kr, 0.0), axis=-1, keepdims=True)
    rq, rk = [], []
    for h in range(MLA_HEADS):
        sl = slice(h * MLA_NOPE, (h + 1) * MLA_NOPE)
        q_ssq = (jnp.sum(qn[:, sl] * qn[:, sl], axis=-1, keepdims=True)
                 + jnp.sum(jnp.where(head_of_lane == h, qr_sq, 0.0), axis=-1, keepdims=True))
        k_ssq = jnp.sum(kn[:, sl] * kn[:, sl], axis=-1, keepdims=True) + kr_ssq
        rq.append(lax.rsqrt(q_ssq / MLA_QK + EPS))
        rk.append(lax.rsqrt(k_ssq / MLA_QK + EPS))

    def per_lane(rs):
        out = jnp.zeros((kr.shape[0], rope_w), F32)
        for h in range(MLA_HEADS):
            out = jnp.where(head_of_lane == h, rs[h], out)
        return out

    def rotate(x, rs, g_ref):
        y = x * per_lane(rs) * g_ref[...]
        return y * cos_ref[...] + _swap_rope_halves(y) * sin_ref[...]

    q_rope = rotate(qr, rq, gqr_ref) * MLA_EXP2_SCALE
    k_rope = rotate(kr, rk, gkr_ref)
    lane_tile = lax.broadcasted_iota(jnp.int32, (1, LANE), 1) // MLA_ROPE
    for h in range(MLA_HEADS):
        sl = slice(h * MLA_NOPE, (h + 1) * MLA_NOPE)
        base = h * MLA_PAD_QK
        tile = slice((h // 2) * LANE, (h // 2 + 1) * LANE)
        q_ref[:, base:base + MLA_NOPE] = (qn[:, sl] * rq[h] * gqn_ref[:, sl] * MLA_EXP2_SCALE).astype(BF16)
        k_ref[:, base:base + MLA_NOPE] = (kn[:, sl] * rk[h] * gkn_ref[:, sl]).astype(BF16)
        q_ref[:, base + MLA_NOPE:base + MLA_PAD_QK] = jnp.where(
            lane_tile == h % 2, q_rope[:, tile], 0.0).astype(BF16)
        k_ref[:, base + MLA_NOPE:base + MLA_PAD_QK] = k_rope[:, tile].astype(BF16)


def mla_prep_part(p_rest, cos4, sin4, qa_g, wuq, kva_g, wukv, gqn, gqr, gkn, gkr, tm):
    s = p_rest.shape[0]
    rope_w = MLA_HEADS * MLA_ROPE

    def const(shape):
        return pl.BlockSpec(shape, lambda i: (0,) * len(shape))

    widths = (MLA_HEADS * MLA_PAD_QK, MLA_HEADS * MLA_PAD_QK, MLA_HEADS * MLA_V)
    return _part(
        _mla_prep_body,
        [
            pl.BlockSpec((tm, MLA_Q_LORA), lambda i: (i, REST_CQ // MLA_Q_LORA)),
            pl.BlockSpec((tm, MLA_KV_LORA), lambda i: (i, REST_CKV // MLA_KV_LORA)),
            pl.BlockSpec((tm, LANE), lambda i: (i, REST_KR // LANE)),
            pl.BlockSpec((tm, rope_w), lambda i: (i, 0)),
            pl.BlockSpec((tm, rope_w), lambda i: (i, 0)),
            const((1, MLA_Q_LORA)), const(wuq.shape), const((1, MLA_KV_LORA)), const(wukv.shape),
            const(gqn.shape), const(gqr.shape), const(gkn.shape), const(gkr.shape),
        ],
        [p_rest, p_rest, p_rest, cos4, sin4, qa_g.reshape(1, -1), wuq, kva_g.reshape(1, -1), wukv,
         gqn, gqr, gkn, gkr],
        [pl.BlockSpec((tm, w), lambda i: (i, 0)) for w in widths],
        [jax.ShapeDtypeStruct((s, w), BF16) for w in widths],
    )


def _mla_attn_body(q_ref, k_ref, v_ref, o_ref, m_ref, l_ref, acc_ref):
    j = pl.program_id(2)

    @pl.when(j == 0)
    def _():
        m_ref[...] = jnp.full(m_ref.shape, NEG_BIG, F32)
        l_ref[...] = jnp.zeros(l_ref.shape, F32)
        acc_ref[...] = jnp.zeros(acc_ref.shape, F32)

    tk = k_ref.shape[0]
    for h in range(MLA_HEADS_PER_STEP):
        qk = slice(h * MLA_PAD_QK, (h + 1) * MLA_PAD_QK)
        vs = slice(h * MLA_V, (h + 1) * MLA_V)
        s = lax.dot_general(q_ref[:, qk], k_ref[:, qk], (((1,), (1,)), ((), ())),
                            preferred_element_type=F32)
        m_prev = m_ref[h]
        m_new = jnp.maximum(m_prev, jnp.max(s, axis=-1, keepdims=True))
        alpha = jnp.exp2(m_prev - m_new)
        p = jnp.exp2(s - jnp.tile(m_new, (1, tk // LANE)))
        l_ref[h] = alpha * l_ref[h] + jnp.sum(p, axis=-1, keepdims=True)
        acc_ref[h] = alpha * acc_ref[h] + jnp.dot(p.astype(BF16), v_ref[:, vs], preferred_element_type=F32)
        m_ref[h] = m_new

    @pl.when(j == pl.num_programs(2) - 1)
    def _():
        for h in range(MLA_HEADS_PER_STEP):
            o_ref[:, h * MLA_V:(h + 1) * MLA_V] = (acc_ref[h] / l_ref[h]).astype(BF16)


def mla_attn(q, k, v, *, tq=1024, tk=2048):
    s = q.shape[0]
    hp = MLA_HEADS_PER_STEP
    stat = pltpu.VMEM((hp, tq, LANE), F32)
    return pl.pallas_call(
        _mla_attn_body,
        grid=(s // tq, MLA_HEADS // hp, s // tk),
        in_specs=[
            pl.BlockSpec((tq, hp * MLA_PAD_QK), lambda i, h, j: (i, h)),
            pl.BlockSpec((tk, hp * MLA_PAD_QK), lambda i, h, j: (j, h)),
            pl.BlockSpec((tk, hp * MLA_V), lambda i, h, j: (j, h)),
        ],
        out_specs=pl.BlockSpec((tq, hp * MLA_V), lambda i, h, j: (i, h)),
        out_shape=jax.ShapeDtypeStruct((s, MLA_HEADS * MLA_V), BF16),
        scratch_shapes=[stat, stat, stat],
        compiler_params=_params("parallel", "parallel", "arbitrary"),
        name="mla_attn",
    )(q, k, v)


def _merge_body(h_ref, *refs):
    z_refs, wg_refs, wb_refs = refs[:N_BRANCH], refs[N_BRANCH:2 * N_BRANCH], refs[2 * N_BRANCH:3 * N_BRANCH]
    o_ref = refs[3 * N_BRANCH]
    h = h_ref[...]
    acc = None
    for z_ref, wg_ref, wb_ref in zip(z_refs, wg_refs, wb_refs):
        gate = jax.nn.sigmoid(_dot_nt(h, wg_ref[0]))
        term = gate * jnp.dot(z_ref[...], wb_ref[...].astype(BF16), preferred_element_type=F32)
        acc = term if acc is None else acc + term
    o_ref[...] = acc.astype(BF16)


GATE_ROW_ALIGN = 64


def merge(h, zs, wt, w_branch, layer, *, tm=1024, tn=512):
    s, d = h.shape
    ct = d // tn
    assert OFF_GATE % GATE_ROW_ALIGN == 0 and d % GATE_ROW_ALIGN == 0 and tn % GATE_ROW_ALIGN == 0
    z_spec = pl.BlockSpec((tm, BRANCH_W), lambda i, c: (i, 0))
    g_specs = [pl.BlockSpec((pl.Element(1), pl.Element(tn), pl.Element(d)),
                            lambda i, c, n=n: (layer, pl.multiple_of(OFF_GATE + n * d + c * tn, GATE_ROW_ALIGN), 0))
               for n in range(N_BRANCH)]
    b_specs = [pl.BlockSpec((None, None, BRANCH_W, tn), lambda i, c, n=n: (layer, n, 0, c))
               for n in range(N_BRANCH)]
    return pl.pallas_call(
        _merge_body,
        grid=(s // tm, ct),
        in_specs=[pl.BlockSpec((tm, d), lambda i, c: (i, 0))] + [z_spec] * N_BRANCH + g_specs + b_specs,
        out_specs=pl.BlockSpec((tm, tn), lambda i, c: (i, c)),
        out_shape=jax.ShapeDtypeStruct((s, d), BF16),
        compiler_params=_params("parallel", "arbitrary"),
        name="merge",
    )(h, *zs, *([wt] * N_BRANCH), *([w_branch] * N_BRANCH))


def _out_proj_body(x_ref, y_ref, w_ref, o_ref):
    o_ref[...] = x_ref[...] + jnp.dot(y_ref[...], w_ref[...].astype(BF16), preferred_element_type=F32)


def out_proj(x, y, w, layer, *, tm=1024, tn=1024):
    s, d = x.shape
    return pl.pallas_call(
        _out_proj_body,
        grid=(d // tn, s // tm),
        in_specs=[
            pl.BlockSpec((tm, tn), lambda j, i: (i, j)),
            pl.BlockSpec((tm, d), lambda j, i: (i, 0)),
            pl.BlockSpec((None, d, tn), lambda j, i: (layer, 0, j)),
        ],
        out_specs=pl.BlockSpec((tm, tn), lambda j, i: (i, j)),
        out_shape=jax.ShapeDtypeStruct((s, d), F32),
        compiler_params=_params("arbitrary", "arbitrary"),
        name="out_proj",
    )(x, y, w)


def _cast_body(x_ref, o_ref):
    o_ref[...] = x_ref[...].astype(BF16)


def cast_rows(wt, *, tr=1024):
    nl, n, d = wt.shape
    return pl.pallas_call(
        _cast_body,
        grid=(nl, pl.cdiv(n, tr)),
        in_specs=[pl.BlockSpec((None, tr, d), lambda l, r: (l, r, 0))],
        out_specs=pl.BlockSpec((None, tr, d), lambda l, r: (l, r, 0)),
        out_shape=jax.ShapeDtypeStruct((nl, n, d), BF16),
        compiler_params=_params("parallel", "parallel"),
        name="cast_rows",
    )(wt)


def _rope_tables(s):
    pos = jnp.arange(s, dtype=F32)
    inv_freq = ROPE_THETA ** (-jnp.arange(0, MLA_ROPE, 2, dtype=F32) / MLA_ROPE)
    ang = pos[:, None] * inv_freq[None, :]
    cos, sin = jnp.cos(ang), jnp.sin(ang)
    cos4 = jnp.tile(jnp.concatenate([cos, cos], axis=-1), (1, MLA_HEADS))
    sin4 = jnp.tile(jnp.concatenate([-sin, sin], axis=-1), (1, MLA_HEADS))
    return cos4, sin4


def _split_heads(w, widths):
    per_head = sum(widths)
    w3 = w.reshape(w.shape[0], MLA_HEADS, per_head)
    parts, off = [], 0
    for wd in widths:
        parts.append(w3[:, :, off:off + wd].reshape(w.shape[0], MLA_HEADS * wd))
        off += wd
    return jnp.concatenate(parts, axis=1)


def kernel(x, ffn1_norm, ffn1_w_gate, ffn1_w_up, ffn1_w_down, mix_norm, w_in, pool_w, pool_scale,
           dil_q_norm, dil_k_norm, sgu_v_norm, sgu_w, sgu_b, mla_q_a_norm, mla_w_uq, mla_kv_a_norm,
           mla_w_ukv, mla_q_norm, mla_k_norm, w_branch, w_out, ffn2_norm, ffn2_w_gate, ffn2_w_up,
           ffn2_w_down):
    b, s, d = x.shape
    assert d == D_MODEL and s % SEQ_MULTIPLE == 0, (x.shape, SEQ_MULTIPLE)
    assert w_in.shape[2] == OFF_GATE + N_BRANCH * D_MODEL and ffn1_w_gate.shape[2] % 512 == 0
    cos4, sin4 = _rope_tables(s)
    w_main = cast_rows(jnp.swapaxes(w_in, 1, 2))
    pool_wb, sgu_wb = pool_w.astype(BF16), sgu_w.astype(BF16)
    ones = jnp.ones((BRANCH_W,), F32)

    outs = []
    for bi in range(b):
        xb = x[bi]
        for l in range(DEPTH):
            xb = ffn(xb, ffn1_norm[l], ffn1_w_gate, ffn1_w_up, ffn1_w_down, l)

            p_rest, h = proj_rest(xb, mix_norm[l], w_main, l)
            dil_gains = jnp.stack([jnp.tile(dil_q_norm[l], DIL_HEADS_PER_GROUP),
                                   jnp.tile(dil_k_norm[l], DIL_HEADS_PER_GROUP), ones]).reshape(3, 1, BRANCH_W)
            qkvs = proj_dil(h, w_main, dil_gains, l)
            o_l = [dil_attn(qkvs[g], g) for g in range(DIL_GROUPS)]
            b_full = jnp.broadcast_to(sgu_b[l][:, :, None], (SGU_GROUPS, SGU_CHUNK, LANE))
            (z_a,), (z_b,), (z_c,), (q, k, v) = _run_parts([
                pool_part(p_rest, pool_wb, pool_scale[l], l, MIXER_ROWS),
                dil_combine_part([o for o, _ in o_l], [ls for _, ls in o_l], MIXER_ROWS),
                sgu_part(p_rest, sgu_v_norm[l], sgu_wb, b_full, l, MIXER_ROWS),
                mla_prep_part(
                    p_rest, cos4, sin4, mla_q_a_norm[l],
                    _split_heads(mla_w_uq[l], (MLA_NOPE, MLA_ROPE)).astype(BF16),
                    mla_kv_a_norm[l], _split_heads(mla_w_ukv[l], (MLA_NOPE, MLA_V)).astype(BF16),
                    jnp.tile(mla_q_norm[l][:MLA_NOPE], MLA_HEADS).reshape(1, -1),
                    jnp.tile(mla_q_norm[l][MLA_NOPE:], MLA_HEADS).reshape(1, -1),
                    jnp.tile(mla_k_norm[l][:MLA_NOPE], MLA_HEADS).reshape(1, -1),
                    jnp.tile(mla_k_norm[l][MLA_NOPE:], MLA_HEADS).reshape(1, -1), MIXER_ROWS),
            ], (s // MIXER_ROWS,), "mixers")
            z_d = mla_attn(q, k, v)
            merged = merge(h, (z_a, z_b, z_c, z_d), w_main, w_branch, l)
            xb = out_proj(xb, merged, w_out, l)

            xb = ffn(xb, ffn2_norm[l], ffn2_w_gate, ffn2_w_up, ffn2_w_down, l)
        outs.append(xb)
    return jnp.stack(outs, axis=0)
```
